```python
import math
import jax
import jax.numpy as jnp
from jax import lax
import numpy as np

D_MODEL = 1024
BATCH = 2
SEQ = 8192
DEPTH = 2

D_MIX = D_MODEL
GROUP_W = D_MIX // 4
HEAD_DIM = 64
N_GROUP_HEADS = GROUP_W // HEAD_DIM
MLA_Q_RANK = 256
MLA_KV_RANK = 128
MLA_NOPE = HEAD_DIM
MLA_ROPE = 32
MLA_V = HEAD_DIM
ROPE_THETA = 10000.0
Q_BLOCK = 128
HGRN_CHUNK = 64
S5_CH = 16
S5_GROUPS = GROUP_W // S5_CH
S5_P = 64
DT_MIN = 0.001
DT_MAX = 0.1
D_FF = 2816
CONV_W = 3
PLE_DIM = 256
EPS = 1e-6
N_IN = (MLA_Q_RANK + MLA_KV_RANK + MLA_ROPE) + (3 * GROUP_W + N_GROUP_HEADS) + 4 * GROUP_W + GROUP_W

kernel_name = 'hybrid_mla_fox_hgrn2_s5_block'


def rmsnorm(x, g):
    xf = x.astype(jnp.float32)
    y = xf * lax.rsqrt(jnp.mean(xf * xf, axis=-1, keepdims=True) + EPS)
    return (y * g.astype(jnp.float32)).astype(x.dtype)


def rope_tables(positions):
    half = MLA_ROPE // 2
    inv_freq = ROPE_THETA ** (-jnp.arange(half, dtype=jnp.float32) / half)
    ang = positions.astype(jnp.float32)[..., None] * inv_freq
    return jnp.cos(ang)[:, :, None, :], jnp.sin(ang)[:, :, None, :]


def apply_rope(t, cos, sin):
    half = MLA_ROPE // 2
    tf = t.astype(jnp.float32)
    t1, t2 = tf[..., :half], tf[..., half:]
    return jnp.concatenate([t1 * cos - t2 * sin, t1 * sin + t2 * cos], axis=-1).astype(t.dtype)


def causal_block_attention(q, k, v, scale, cum=None):
    b, s, h, _ = q.shape
    nb = s // Q_BLOCK
    q_blocks = q.reshape(b, nb, Q_BLOCK, h, q.shape[-1]).swapaxes(0, 1)
    key_pos = jnp.arange(s)

    def one_block(args):
        i, q_i = args[0], args[1]
        logits = jnp.einsum('bqhd,bkhd->bhqk', q_i, k).astype(jnp.float32) * scale
        if cum is not None:
            logits = logits + args[2][..., None] - cum[:, :, None, :]
        q_pos = i * Q_BLOCK + jnp.arange(Q_BLOCK)
        logits = jnp.where(key_pos[None, :] <= q_pos[:, None], logits, -jnp.inf)
        probs = jax.nn.softmax(logits, axis=-1).astype(v.dtype)
        return jnp.einsum('bhqk,bkhd->bqhd', probs, v)

    xs = (jnp.arange(nb), q_blocks)
    if cum is not None:
        xs = xs + (cum.reshape(b, h, nb, Q_BLOCK).transpose(2, 0, 1, 3),)
    out = lax.map(one_block, xs)
    return out.swapaxes(0, 1).reshape(b, s, h, v.shape[-1])


def mla_group(c_q, c_kv, k_rope, positions, q_norm_g, w_uq, kv_norm_g, w_ukv):
    b, s, _ = c_q.shape
    q = (rmsnorm(c_q, q_norm_g) @ w_uq).reshape(b, s, N_GROUP_HEADS, MLA_NOPE + MLA_ROPE)
    kv = (rmsnorm(c_kv, kv_norm_g) @ w_ukv).reshape(b, s, N_GROUP_HEADS, MLA_NOPE + MLA_V)
    cos, sin = rope_tables(positions)
    q_pe = apply_rope(q[..., MLA_NOPE:], cos, sin)
    k_pe = apply_rope(k_rope[:, :, None, :], cos, sin)
    q_full = jnp.concatenate([q[..., :MLA_NOPE], q_pe], axis=-1)
    k_full = jnp.concatenate([kv[..., :MLA_NOPE], jnp.broadcast_to(k_pe, (b, s, N_GROUP_HEADS, MLA_ROPE))], axis=-1)
    o = causal_block_attention(q_full, k_full, kv[..., MLA_NOPE:], (MLA_NOPE + MLA_ROPE) ** -0.5)
    return o.reshape(b, s, GROUP_W)


def fox_group(qkv, f_logit, b_f):
    b, s, _ = qkv.shape
    qkv = qkv.reshape(b, s, 3, N_GROUP_HEADS, HEAD_DIM)
    log_f = jax.nn.log_sigmoid(f_logit.astype(jnp.float32) + b_f.astype(jnp.float32))
    cum = jnp.cumsum(log_f, axis=1).transpose(0, 2, 1)
    o = causal_block_attention(qkv[:, :, 0], qkv[:, :, 1], qkv[:, :, 2], HEAD_DIM ** -0.5, cum)
    return o.reshape(b, s, GROUP_W)


def hgrn2_group(q, f_logit, i_in, lb):
    b, s, _ = q.shape
    nc = s // HGRN_CHUNK
    z = f_logit.astype(jnp.float32)
    lbf = lb.astype(jnp.float32)
    log_f = jnp.logaddexp(jnp.log(lbf), jnp.log1p(-lbf) + jax.nn.log_sigmoid(z))
    k = (1.0 - lbf) * jax.nn.sigmoid(-z)

    def to_chunks(t):
        return t.astype(jnp.float32).reshape(b, nc, HGRN_CHUNK, N_GROUP_HEADS, HEAD_DIM).transpose(1, 0, 3, 2, 4)

    causal = jnp.tril(jnp.ones((HGRN_CHUNK, HGRN_CHUNK), dtype=bool))

    def chunk_step(state, inp):
        q_c, k_c, v_c, lf_c = inp
        bcum = jnp.cumsum(lf_c, axis=2)
        diff = bcum[:, :, :, None, :] - bcum[:, :, None, :, :]
        decay = jnp.exp(jnp.where(causal[:, :, None], diff, -jnp.inf))
        scores = jnp.einsum('bhtk,bhsk,bhtsk->bhts', q_c, k_c, decay)
        o = jnp.einsum('bhts,bhsv->bhtv', scores, v_c) + jnp.einsum('bhtk,bhkv->bhtv', q_c * jnp.exp(bcum), state)
        b_last = bcum[:, :, -1:, :]
        state = jnp.exp(b_last[:, :, 0, :, None]) * state + jnp.einsum('bhsk,bhsv->bhkv', k_c * jnp.exp(b_last - bcum), v_c)
        return state, o

    state0 = jnp.zeros((b, N_GROUP_HEADS, HEAD_DIM, HEAD_DIM), jnp.float32)
    _, o = lax.scan(chunk_step, state0, (to_chunks(q), to_chunks(k), to_chunks(i_in), to_chunks(log_f)))
    return o.transpose(1, 0, 3, 2, 4).reshape(b, s, GROUP_W).astype(q.dtype)


def s5_combine(earlier, later):
    a_re, a_im, x_re, x_im = earlier
    b_re, b_im, y_re, y_im = later
    return (b_re * a_re - b_im * a_im, b_re * a_im + b_im * a_re,
            b_re * x_re - b_im * x_im + y_re, b_re * x_im + b_im * x_re + y_im)


def s5_group(u, lam_re, lam_im, log_step, b_re, b_im, c_re, c_im, d_skip, w_glu, b_glu):
    b, s, _ = u.shape
    uf = u.astype(jnp.float32)
    step = jnp.exp(log_step.astype(jnp.float32))[:, None]
    lre = jnp.minimum(lam_re.astype(jnp.float32), -1e-4)
    lim = lam_im.astype(jnp.float32)
    mag = jnp.exp(lre * step)
    a_re, a_im = mag * jnp.cos(lim * step), mag * jnp.sin(lim * step)
    den = lre * lre + lim * lim
    coef_re = ((a_re - 1.0) * lre + a_im * lim) / den
    coef_im = (a_im * lre - (a_re - 1.0) * lim) / den
    br, bi = b_re.astype(jnp.float32), b_im.astype(jnp.float32)
    bb_re = coef_re[..., None] * br - coef_im[..., None] * bi
    bb_im = coef_re[..., None] * bi + coef_im[..., None] * br
    ug = uf.reshape(b, s, S5_GROUPS, S5_CH)
    bu_re = jnp.einsum('bsgh,gph->bsgp', ug, bb_re)
    bu_im = jnp.einsum('bsgh,gph->bsgp', ug, bb_im)
    _, _, x_re, x_im = lax.associative_scan(
        s5_combine,
        (jnp.broadcast_to(a_re, bu_re.shape), jnp.broadcast_to(a_im, bu_re.shape), bu_re, bu_im),
        axis=1)
    y = jnp.einsum('bsgp,ghp->bsgh', x_re, c_re.astype(jnp.float32)) - jnp.einsum('bsgp,ghp->bsgh', x_im, c_im.astype(jnp.float32))
    y = y.reshape(b, s, GROUP_W) + d_skip.astype(jnp.float32) * uf
    zact = jax.nn.gelu(y)
    out = zact * jax.nn.sigmoid(zact @ w_glu.astype(jnp.float32) + b_glu.astype(jnp.float32))
    return out.astype(u.dtype)


def conv_ffn(x, w_up, conv_w, conv_b, w_down):
    s = x.shape[1]
    up = x @ w_up
    padded = jnp.pad(up, ((0, 0), (CONV_W - 1, 0), (0, 0)))
    conv = conv_b + conv_w[0] * padded[:, 0:s]
    for j in range(1, CONV_W):
        conv = conv + conv_w[j] * padded[:, j:j + s]
    gate, val = jnp.split(conv, 2, axis=-1)
    return (jax.nn.silu(gate) * val) @ w_down


def setup_inputs(seed: int = 0) -> dict:
    key = jax.random.key(seed)
    ks = iter(jax.random.split(key, 40))
    nrm = lambda shape, scale: scale * jax.random.normal(next(ks), shape, jnp.float32)
    gain = lambda shape: 1.0 + nrm(shape, 0.02)
    L = DEPTH
    inp = {}
    inp['x'] = nrm((BATCH, SEQ, D_MODEL), 1.0)
    inp['p'] = nrm((DEPTH, BATCH, SEQ, PLE_DIM), 1.0)
    offs = jax.random.randint(next(ks), (BATCH, 1), 0, 4096, dtype=jnp.int32)
    inp['positions'] = (jnp.arange(SEQ, dtype=jnp.int32)[None, :] + offs).astype(jnp.int32)
    inp['attn_norm_g'] = gain((L, D_MODEL))
    inp['w_in'] = nrm((L, D_MODEL, N_IN), D_MODEL ** -0.5)
    inp['mla_q_norm_g'] = gain((L, MLA_Q_RANK))
    inp['mla_w_uq'] = nrm((L, MLA_Q_RANK, N_GROUP_HEADS * (MLA_NOPE + MLA_ROPE)), MLA_Q_RANK ** -0.5)
    inp['mla_kv_norm_g'] = gain((L, MLA_KV_RANK))
    inp['mla_w_ukv'] = nrm((L, MLA_KV_RANK, N_GROUP_HEADS * (MLA_NOPE + MLA_V)), MLA_KV_RANK ** -0.5)
    inp['fox_b_f'] = jnp.linspace(1.0, 6.0, N_GROUP_HEADS, dtype=jnp.float32)[None, :] + nrm((L, N_GROUP_HEADS), 0.1)
    inp['hgrn_lb_param'] = nrm((L, GROUP_W), 0.1)
    inp['s5_lam_re'] = -0.5 + nrm((L, S5_GROUPS, S5_P), 0.01)
    inp['s5_lam_im'] = jnp.broadcast_to(math.pi * jnp.arange(S5_P, dtype=jnp.float32), (L, S5_GROUPS, S5_P))
    inp['s5_log_step'] = jax.random.uniform(next(ks), (L, S5_GROUPS), jnp.float32, math.log(DT_MIN), math.log(DT_MAX))
    inp['s5_b_re'] = nrm((L, S5_GROUPS, S5_P, S5_CH), (2 * S5_CH) ** -0.5)
    inp['s5_b_im'] = nrm((L, S5_GROUPS, S5_P, S5_CH), (2 * S5_CH) ** -0.5)
    inp['s5_c_re'] = nrm((L, S5_GROUPS, S5_CH, S5_P), (2 * S5_P) ** -0.5)
    inp['s5_c_im'] = nrm((L, S5_GROUPS, S5_CH, S5_P), (2 * S5_P) ** -0.5)
    inp['s5_d'] = nrm((L, GROUP_W), 1.0)
    inp['s5_w_glu'] = nrm((L, GROUP_W, GROUP_W), GROUP_W ** -0.5)
    inp['s5_b_glu'] = nrm((L, GROUP_W), 0.01)
    inp['group_norm_g'] = gain((L, D_MIX))
    inp['w_out'] = nrm((L, D_MIX, D_MODEL), D_MIX ** -0.5)
    inp['ffn_norm_g'] = gain((L, D_MODEL))
    inp['w_up'] = nrm((L, D_MODEL, 2 * D_FF), D_MODEL ** -0.5)
    inp['conv_w'] = nrm((L, CONV_W, 2 * D_FF), CONV_W ** -0.5)
    inp['conv_b'] = nrm((L, 2 * D_FF), 0.01)
    inp['w_down'] = nrm((L, D_FF, D_MODEL), D_FF ** -0.5)
    inp['ple_norm_g'] = gain((L, D_MODEL))
    inp['w_ple_gate'] = nrm((L, D_MODEL, D_MODEL), D_MODEL ** -0.5)
    inp['w_ple'] = nrm((L, PLE_DIM, D_MODEL), PLE_DIM ** -0.5)
    inp['final_norm_g'] = gain((D_MODEL,))
    return inp


def reference(x, p, positions, attn_norm_g, w_in, mla_q_norm_g, mla_w_uq, mla_kv_norm_g, mla_w_ukv,
              fox_b_f, hgrn_lb_param, s5_lam_re, s5_lam_im, s5_log_step, s5_b_re, s5_b_im, s5_c_re, s5_c_im,
              s5_d, s5_w_glu, s5_b_glu, group_norm_g, w_out, ffn_norm_g, w_up, conv_w, conv_b, w_down,
              ple_norm_g, w_ple_gate, w_ple, final_norm_g):
    lb_all = jnp.cumsum(jax.nn.softmax(hgrn_lb_param.astype(jnp.float32), axis=0), axis=0)
    lb_all = lb_all - lb_all[0:1]
    sizes = (MLA_Q_RANK, MLA_KV_RANK, MLA_ROPE, 3 * GROUP_W, N_GROUP_HEADS, GROUP_W, GROUP_W, GROUP_W, GROUP_W, GROUP_W)
    offsets = [int(o) for o in np.cumsum(sizes)[:-1]]
    gw = GROUP_W
    h = x
    for i in range(DEPTH):
        hn = rmsnorm(h, attn_norm_g[i])
        proj = hn @ w_in[i]
        (c_q, c_kv, k_rope, fox_qkv, fox_f, hg_q, hg_f, hg_i, hg_g, s5_u) = jnp.split(proj, offsets, axis=-1)
        y_a = mla_group(c_q, c_kv, k_rope, positions, mla_q_norm_g[i], mla_w_uq[i], mla_kv_norm_g[i], mla_w_ukv[i])
        y_b = fox_group(fox_qkv, fox_f, fox_b_f[i])
        y_c = hgrn2_group(hg_q, hg_f, hg_i, lb_all[i])
        y_d = s5_group(s5_u, s5_lam_re[i], s5_lam_im[i], s5_log_step[i], s5_b_re[i], s5_b_im[i],
                       s5_c_re[i], s5_c_im[i], s5_d[i], s5_w_glu[i], s5_b_glu[i])
        gn = group_norm_g[i]
        y = jnp.concatenate([
            rmsnorm(y_a, gn[0:gw]),
            rmsnorm(y_b, gn[gw:2 * gw]),
            rmsnorm(y_c, gn[2 * gw:3 * gw]) * jax.nn.sigmoid(hg_g),
            rmsnorm(y_d, gn[3 * gw:4 * gw]),
        ], axis=-1)
        h = h + y @ w_out[i]
        h = h + conv_ffn(rmsnorm(h, ffn_norm_g[i]), w_up[i], conv_w[i], conv_b[i], w_down[i])
        ple_gate = jax.nn.sigmoid(rmsnorm(h, ple_norm_g[i]) @ w_ple_gate[i])
        h = h + ple_gate * (p[i] @ w_ple[i])
    return rmsnorm(h, final_norm_g)
```

```python
import functools
import math

import numpy as np
import jax
import jax.numpy as jnp
from jax import lax
from jax.experimental import pallas as pl
from jax.experimental.pallas import tpu as pltpu

F32 = jnp.float32
BF16 = jnp.bfloat16

D_MODEL = 1024
N_HEADS = 4
HEAD_DIM = 64
GROUP_W = 256
MLA_Q_RANK = 256
MLA_KV_RANK = 128
MLA_NOPE = 64
MLA_ROPE = 32
ROPE_THETA = 10000.0
S5_GROUPS = 16
S5_CH = 16
S5_P = 64
D_FF = 2816
PLE_DIM = 256
EPS = 1e-6
N_IN = 2468

V7X_LANES = 128
V7X_SUBLANES = 8
V7X_VMEM_BYTES = 64 * 1024 * 1024
VMEM_CAP_BYTES = 58 * 1024 * 1024

HEAD_PAD = V7X_LANES
SUB = 16
NEG_BIG = -1e30

SEG_CQ = 0
SEG_CKV = 256
SEG_KR = 384
SEG_FQ = 512
SEG_FK = 1024
SEG_FV = 1536
SEG_FF = 1792
SEG_HG = 1920
SEG_S5 = 2944
N_PERM = 3200
ROPE_LANE0 = 64
BIAS_LANE0 = 64


def _vmem_limit(*byte_counts):
    need = int(sum(byte_counts))
    return int(min(VMEM_CAP_BYTES, need + need // 4 + (4 << 20)))


def _nbytes(shape, dtype):
    return int(np.prod(shape)) * jnp.dtype(dtype).itemsize


def _rms(x, g):
    return x * lax.rsqrt(jnp.mean(x * x, axis=-1, keepdims=True) + EPS) * g


def _log_sigmoid(z):
    return jnp.minimum(z, 0.0) - jnp.log1p(jnp.exp(-jnp.abs(z)))


def _sigmoid(z):
    return 1.0 / (1.0 + jnp.exp(-z))


def _iota(shape, dim):
    return lax.broadcasted_iota(jnp.int32, shape, dim)


def _shift_rows(x, k):
    if k == 0:
        return x
    return pltpu.roll(x, k, 0)


def _rope_kernel(pos_ref, freq_ref, ct_ref, s1_ref, s2_ref):
    ang = pos_ref[...].astype(F32) * freq_ref[...]
    lane = _iota(ang.shape, 1)
    half = MLA_ROPE // 2
    sin = jnp.sin(ang)
    ct_ref[...] = jnp.cos(ang)
    s1_ref[...] = jnp.where((lane >= ROPE_LANE0) & (lane < ROPE_LANE0 + half), -sin, 0.0)
    s2_ref[...] = jnp.where((lane >= ROPE_LANE0 + half) & (lane < ROPE_LANE0 + 2 * half), sin, 0.0)


def _rope_tables(positions, tm):
    t = positions.size
    half = MLA_ROPE // 2
    inv_freq = ROPE_THETA ** (-jnp.arange(half, dtype=F32) / half)
    freq = jnp.zeros((1, HEAD_PAD), F32).at[0, ROPE_LANE0:ROPE_LANE0 + 2 * half].set(jnp.tile(inv_freq, 2))
    pos = positions.reshape(t, 1)
    out = jax.ShapeDtypeStruct((t, HEAD_PAD), F32)
    spec = pl.BlockSpec((tm, HEAD_PAD), lambda i: (i, 0))
    return pl.pallas_call(
        _rope_kernel,
        out_shape=(out, out, out),
        grid=(t // tm,),
        in_specs=[pl.BlockSpec((tm, 1), lambda i: (i, 0)), pl.BlockSpec((1, HEAD_PAD), lambda i: (0, 0))],
        out_specs=(spec, spec, spec),
        name="rope_tables",
        compiler_params=pltpu.CompilerParams(dimension_semantics=("arbitrary",)),
    )(pos, freq)


def _inproj_kernel(tiles_per_seq, h_ref, g_ref, w_ref, qg_ref, wuq_ref, kvg_ref, wukv_ref, bf_ref,
                   ct_ref, s1_ref, s2_ref, selq_ref, selk_ref,
                   mq_ref, mk_ref, mv_ref, fq_ref, fk_ref, fv_ref, hg_ref, su_ref, carry_ref):
    i = pl.program_id(0)

    @pl.when(i % tiles_per_seq == 0)
    def _():
        carry_ref[...] = jnp.zeros_like(carry_ref)

    tm = h_ref.shape[0]
    xn = _rms(h_ref[...], g_ref[...]).astype(BF16)

    def seg(a, b):
        return jnp.dot(xn, w_ref[:, a:b], preferred_element_type=F32)

    q = jnp.dot(_rms(seg(SEG_CQ, SEG_CKV), qg_ref[...]).astype(BF16), wuq_ref[...],
                preferred_element_type=F32)
    kv = jnp.dot(_rms(seg(SEG_CKV, SEG_KR), kvg_ref[...]).astype(BF16), wukv_ref[...],
                 preferred_element_type=F32)
    ct, s1, s2 = ct_ref[...], s1_ref[...], s2_ref[...]
    half = MLA_ROPE // 2

    def rope(t):
        return t * ct + pltpu.roll(t, HEAD_PAD - half, 1) * s1 + pltpu.roll(t, half, 1) * s2

    k_pe = rope(seg(SEG_KR, SEG_FQ))
    mla_scale = (MLA_NOPE + MLA_ROPE) ** -0.5
    for hd in range(N_HEADS):
        sl = slice(hd * HEAD_PAD, (hd + 1) * HEAD_PAD)
        mq_ref[:, sl] = (rope(q[:, sl]) * mla_scale).astype(BF16)
        mk_ref[:, sl] = (kv[:, sl] + k_pe).astype(BF16)
    mv_ref[...] = kv[:, N_HEADS * HEAD_PAD:].astype(BF16)

    lf = _log_sigmoid(seg(SEG_FF, SEG_HG) + bf_ref[...])
    tril = (_iota((tm, tm), 0) >= _iota((tm, tm), 1)).astype(BF16)

    def split3(x):
        a = x.astype(BF16)
        r = x - a.astype(F32)
        b = r.astype(BF16)
        c = (r - b.astype(F32)).astype(BF16)
        return a, b, c

    l1, l2, l3 = split3(lf)
    cum = (jnp.dot(tril, l1, preferred_element_type=F32) + jnp.dot(tril, l2, preferred_element_type=F32)
           + jnp.dot(tril, l3, preferred_element_type=F32)) + carry_ref[...]
    carry_ref[...] = cum[tm - 1:tm, :]
    lane = _iota(cum.shape, 1)
    c1, c2, c3 = split3(cum)
    keep = lane < N_HEADS
    parts = (jnp.where(keep, c1.astype(F32), 0.0)
             + pltpu.roll(jnp.where(keep, c2.astype(F32), 0.0), N_HEADS, 1)
             + pltpu.roll(jnp.where(keep, c3.astype(F32), 0.0), 2 * N_HEADS, 1)
             + jnp.where(lane == 3 * N_HEADS, 1.0, 0.0)).astype(BF16)
    fox_scale = HEAD_DIM ** -0.5
    fq_ref[...] = (seg(SEG_FQ, SEG_FK) * fox_scale
                   + jnp.dot(parts, selq_ref[...], preferred_element_type=F32)).astype(BF16)
    fk_ref[...] = (seg(SEG_FK, SEG_FV)
                   + jnp.dot(parts, selk_ref[...], preferred_element_type=F32)).astype(BF16)
    fv_ref[...] = seg(SEG_FV, SEG_FF).astype(BF16)

    hg_ref[...] = seg(SEG_HG, SEG_S5)
    su_ref[...] = seg(SEG_S5, N_PERM)


def _inproj_index():
    zero = N_IN
    idx = list(range(0, 384))
    kr = [zero] * HEAD_PAD
    kr[ROPE_LANE0:ROPE_LANE0 + MLA_ROPE] = range(384, 416)
    idx += kr
    for base in (416, 672):
        for hd in range(N_HEADS):
            blk = [zero] * HEAD_PAD
            blk[0:HEAD_DIM] = range(base + HEAD_DIM * hd, base + HEAD_DIM * (hd + 1))
            idx += blk
    idx += range(928, 1184)
    ff = [zero] * HEAD_PAD
    ff[0:N_HEADS] = range(1184, 1188)
    idx += ff
    idx += range(1188, N_IN)
    assert len(idx) == N_PERM
    return np.asarray(idx, np.int32)


def _bias_selectors():
    selq = np.zeros((HEAD_PAD, N_HEADS * HEAD_PAD), np.float32)
    selk = np.zeros((HEAD_PAD, N_HEADS * HEAD_PAD), np.float32)
    one = 3 * N_HEADS
    for hd in range(N_HEADS):
        for j in range(3):
            selq[N_HEADS * j + hd, hd * HEAD_PAD + BIAS_LANE0 + j] = 1.0
            selq[one, hd * HEAD_PAD + BIAS_LANE0 + 3 + j] = 1.0
            selk[one, hd * HEAD_PAD + BIAS_LANE0 + j] = 1.0
            selk[N_HEADS * j + hd, hd * HEAD_PAD + BIAS_LANE0 + 3 + j] = -1.0
    return jnp.asarray(selq, BF16), jnp.asarray(selk, BF16)


def _inproj(h, seq, attn_g, w_in, q_g, w_uq, kv_g, w_ukv, b_f, tables, tm):
    t = h.shape[0]
    w_perm = jnp.take(jnp.concatenate([w_in, jnp.zeros((D_MODEL, 1), w_in.dtype)], axis=1),
                      _inproj_index(), axis=1).astype(BF16)
    wuq = jnp.pad(w_uq.reshape(MLA_Q_RANK, N_HEADS, MLA_NOPE + MLA_ROPE),
                  ((0, 0), (0, 0), (0, HEAD_PAD - MLA_NOPE - MLA_ROPE))).reshape(MLA_Q_RANK, -1).astype(BF16)
    wkv = w_ukv.reshape(MLA_KV_RANK, N_HEADS, 2 * HEAD_DIM)
    wk = jnp.pad(wkv[:, :, :MLA_NOPE], ((0, 0), (0, 0), (0, HEAD_PAD - MLA_NOPE))).reshape(MLA_KV_RANK, -1)
    wv = wkv[:, :, MLA_NOPE:].reshape(MLA_KV_RANK, -1)
    wukv = jnp.concatenate([wk, wv], axis=1).astype(BF16)
    bf = jnp.zeros((1, HEAD_PAD), F32).at[0, :N_HEADS].set(b_f.astype(F32))
    selq, selk = _bias_selectors()
    ct, s1, s2 = tables
    hp4 = N_HEADS * HEAD_PAD

    def rows(width):
        return pl.BlockSpec((tm, width), lambda i: (i, 0))

    def whole(a):
        return pl.BlockSpec(a.shape, lambda i: (0,) * a.ndim)

    args = (h, attn_g.reshape(1, -1), w_perm, q_g.reshape(1, -1), wuq, kv_g.reshape(1, -1), wukv, bf,
            ct, s1, s2, selq, selk)
    in_specs = [rows(D_MODEL)] + [whole(a) for a in args[1:8]] + [rows(HEAD_PAD)] * 3 + [whole(selq), whole(selk)]
    out_widths = (hp4, hp4, GROUP_W, hp4, hp4, GROUP_W)
    out_shape = tuple(jax.ShapeDtypeStruct((t, w), BF16) for w in out_widths) + (
        jax.ShapeDtypeStruct((t, 4 * GROUP_W), F32), jax.ShapeDtypeStruct((t, GROUP_W), F32))
    out_specs = tuple(rows(w) for w in out_widths) + (rows(4 * GROUP_W), rows(GROUP_W))
    vmem = _vmem_limit(2 * _nbytes(w_perm.shape, BF16), 2 * _nbytes((tm, D_MODEL), F32),
                       2 * sum(_nbytes((tm, w), BF16) for w in out_widths),
                       2 * _nbytes((tm, 5 * GROUP_W), F32), _nbytes((tm, N_PERM), F32),
                       _nbytes((tm, D_MODEL), F32))
    return pl.pallas_call(
        functools.partial(_inproj_kernel, seq // tm),
        out_shape=out_shape,
        grid=(t // tm,),
        in_specs=in_specs,
        out_specs=out_specs,
        scratch_shapes=[pltpu.VMEM((1, HEAD_PAD), F32)],
        name="inproj",
        compiler_params=pltpu.CompilerParams(dimension_semantics=("arbitrary",), vmem_limit_bytes=vmem),
    )(*args)


def _attn_kernel(tq, tkv, q_ref, k_ref, v_ref, o_ref, m_ref, l_ref, acc_ref):
    qi = pl.program_id(1)
    m_ref[...] = jnp.full_like(m_ref, NEG_BIG)
    l_ref[...] = jnp.zeros_like(l_ref)
    acc_ref[...] = jnp.zeros_like(acc_ref)
    per_q = tq // tkv

    def block(j, diag_offset):
        start = pl.multiple_of(j * tkv, tkv)
        kblk = k_ref[pl.ds(start, tkv), :]
        vblk = v_ref[pl.ds(start, tkv), :]
        for hd in range(N_HEADS):
            sl = slice(hd * HEAD_PAD, (hd + 1) * HEAD_PAD)
            s = lax.dot_general(q_ref[:, sl], kblk[:, sl], (((1,), (1,)), ((), ())),
                                preferred_element_type=F32)
            if diag_offset is not None:
                visible = _iota(s.shape, 1) + diag_offset <= _iota(s.shape, 0)
                s = jnp.where(visible, s, NEG_BIG)
            m_old = m_ref[hd]
            m_new = jnp.maximum(m_old, jnp.max(s, axis=1, keepdims=True))
            alpha = jnp.exp(m_old - m_new)
            p = jnp.exp(s - m_new)
            l_ref[hd] = alpha * l_ref[hd] + jnp.sum(p, axis=1, keepdims=True)
            m_ref[hd] = m_new
            acc_ref[hd] = alpha * acc_ref[hd] + jnp.dot(p.astype(BF16), vblk, preferred_element_type=F32)

    def full_block(j, carry):
        block(j, None)
        return carry

    lax.fori_loop(0, qi * per_q, full_block, 0)
    for r in range(per_q):
        block(qi * per_q + r, r * tkv)

    lane = _iota((tq, GROUP_W), 1)
    out = jnp.zeros((tq, GROUP_W), F32)
    for hd in range(N_HEADS):
        in_head = (lane >= hd * HEAD_DIM) & (lane < (hd + 1) * HEAD_DIM)
        out = jnp.where(in_head, acc_ref[hd] / l_ref[hd], out)
    o_ref[...] = out


def _attention(q, k, v, batch, seq, tq, tkv):
    t = q.shape[0]
    hp4 = N_HEADS * HEAD_PAD
    vmem = _vmem_limit(2 * _nbytes((seq, hp4 + GROUP_W), BF16), 2 * _nbytes((tq, hp4), BF16),
                       2 * _nbytes((tq, GROUP_W), F32), N_HEADS * _nbytes((tq, GROUP_W + 2 * V7X_LANES), F32),
                       4 * _nbytes((tq, tkv), F32))
    return pl.pallas_call(
        functools.partial(_attn_kernel, tq, tkv),
        out_shape=jax.ShapeDtypeStruct((t, GROUP_W), F32),
        grid=(batch, seq // tq),
        in_specs=[pl.BlockSpec((tq, hp4), lambda b, i: (b * (seq // tq) + i, 0)),
                  pl.BlockSpec((seq, hp4), lambda b, i: (b, 0)),
                  pl.BlockSpec((seq, GROUP_W), lambda b, i: (b, 0))],
        out_specs=pl.BlockSpec((tq, GROUP_W), lambda b, i: (b * (seq // tq) + i, 0)),
        scratch_shapes=[pltpu.VMEM((N_HEADS, tq, 1), F32), pltpu.VMEM((N_HEADS, tq, 1), F32),
                        pltpu.VMEM((N_HEADS, tq, GROUP_W), F32)],
        name="causal_attention",
        compiler_params=pltpu.CompilerParams(dimension_semantics=("arbitrary", "arbitrary"),
                                             vmem_limit_bytes=vmem),
    )(q, k, v)


def _segment_cumsum(x, rsub, reverse=False):
    n = x.shape[0]
    k = 1
    while k < SUB:
        if reverse:
            x = x + jnp.where(rsub < SUB - k, pltpu.roll(x, n - k, 0), 0.0)
        else:
            x = x + jnp.where(rsub >= k, pltpu.roll(x, k, 0), 0.0)
        k *= 2
    return x


def _hgrn_kernel(win, q_ref, f_ref, v_ref, lb_ref, ee_ref, o_ref,
                 st_ref, bc_ref, ke_ref, vt_ref, od_ref):
    @pl.when(pl.program_id(1) == 0)
    def _():
        st_ref[...] = jnp.zeros_like(st_ref)

    r = q_ref.shape[0]
    z = f_ref[...]
    lb = lb_ref[...]
    a = jnp.log(lb)
    b = jnp.log1p(-lb) + _log_sigmoid(z)
    log_f = jnp.maximum(a, b) + jnp.log1p(jnp.exp(-jnp.abs(a - b)))
    kk = (1.0 - lb) * _sigmoid(-z)
    rsub = _iota((r, GROUP_W), 0) & (SUB - 1)
    bc = _segment_cumsum(log_f, rsub)
    tail = _segment_cumsum(log_f, rsub, reverse=True) - log_f
    q = q_ref[...]
    v = v_ref[...]

    od = jnp.zeros((r, GROUP_W), F32)
    for d in range(SUB):
        e = jnp.where(rsub >= d, bc - _shift_rows(bc, d), -jnp.inf)
        prod = (q * _shift_rows(kk, d) * jnp.exp(e)).astype(BF16)
        od = od + jnp.dot(prod, ee_ref[...], preferred_element_type=F32) * _shift_rows(v, d)
    od_ref[...] = od
    bc_ref[...] = bc
    ke_ref[...] = kk * jnp.exp(tail)
    for w in range(r // win):
        vt_ref[w] = v[w * win:(w + 1) * win, :].T.astype(BF16)

    lane_head = _iota((GROUP_W, GROUP_W), 1) // HEAD_DIM
    row_head = _iota((GROUP_W, GROUP_W), 0) // HEAD_DIM
    same_head = lane_head == row_head
    wrow = _iota((win, GROUP_W), 0)

    def step(c, carry):
        base = pl.multiple_of(c * SUB, SUB)
        w = c // (win // SUB)
        wbase = pl.multiple_of(w * win, win)
        bcc = bc_ref[pl.ds(base, SUB), :]
        st = st_ref[...]
        qe = (q_ref[pl.ds(base, SUB), :] * jnp.exp(bcc)).astype(BF16)
        o_state = lax.dot_general(qe, st.astype(BF16), (((1,), (1,)), ((), ())), preferred_element_type=F32)
        o_ref[pl.ds(base, SUB), :] = od_ref[pl.ds(base, SUB), :] + o_state
        in_sub = (wrow >= base - wbase) & (wrow < base - wbase + SUB)
        ke = jnp.where(in_sub, ke_ref[pl.ds(wbase, win), :], 0.0).astype(BF16)
        upd = jnp.dot(vt_ref[w], ke, preferred_element_type=F32)
        st_ref[...] = st * jnp.exp(bcc[SUB - 1:SUB, :]) + jnp.where(same_head, upd, 0.0)
        return carry

    lax.fori_loop(0, r // SUB, step, 0)


def _hgrn(hg, lb, batch, seq, r):
    t = hg.shape[0]
    win = min(r, V7X_LANES)
    ee = jnp.asarray(np.kron(np.eye(N_HEADS, dtype=np.float32), np.ones((HEAD_DIM, HEAD_DIM), np.float32)), BF16)
    nblk = seq // r

    def col(c):
        return pl.BlockSpec((r, GROUP_W), lambda b, i: (b * nblk + i, c))

    vmem = _vmem_limit(6 * _nbytes((r, GROUP_W), F32), 2 * _nbytes((r, GROUP_W), F32),
                       4 * _nbytes((r, GROUP_W), F32), 12 * _nbytes((r, GROUP_W), F32))
    return pl.pallas_call(
        functools.partial(_hgrn_kernel, win),
        out_shape=jax.ShapeDtypeStruct((t, GROUP_W), F32),
        grid=(batch, nblk),
        in_specs=[col(0), col(1), col(2), pl.BlockSpec((1, GROUP_W), lambda b, i: (0, 0)),
                  pl.BlockSpec((GROUP_W, GROUP_W), lambda b, i: (0, 0))],
        out_specs=pl.BlockSpec((r, GROUP_W), lambda b, i: (b * nblk + i, 0)),
        scratch_shapes=[pltpu.VMEM((GROUP_W, GROUP_W), F32), pltpu.VMEM((r, GROUP_W), F32),
                        pltpu.VMEM((r, GROUP_W), F32), pltpu.VMEM((r // win, GROUP_W, win), BF16),
                        pltpu.VMEM((r, GROUP_W), F32)],
        name="hgrn2",
        compiler_params=pltpu.CompilerParams(dimension_semantics=("arbitrary", "arbitrary"),
                                             vmem_limit_bytes=vmem),
    )(hg, hg, hg, lb.reshape(1, -1), ee)


def _s5_local_kernel(u_ref, bd_ref, bbd_ref, we_ref, d_ref, y_ref, e_ref):
    r = u_ref.shape[0]
    u = u_ref[...]
    rsub = _iota((r, GROUP_W), 0) & (SUB - 1)
    y = d_ref[...] * u
    for j in range(SUB):
        uj = jnp.where(rsub >= j, _shift_rows(u, j), 0.0).astype(BF16)
        y = y + jnp.dot(uj, bd_ref[j], preferred_element_type=F32)
    y_ref[...] = y
    n = S5_GROUPS * S5_P
    bu = jnp.dot(u.astype(BF16), bbd_ref[...], preferred_element_type=F32).reshape(r // SUB, SUB, 2 * n)
    bur, bui = bu[:, :, :n], bu[:, :, n:]
    wr, wi = we_ref[:, :n], we_ref[:, n:]
    e_ref[:, :n] = jnp.sum(wr * bur - wi * bui, axis=1)
    e_ref[:, n:] = jnp.sum(wr * bui + wi * bur, axis=1)


def _s5_scan_kernel(e_ref, a_ref, x_ref):
    nchunk = e_ref.shape[0]
    n = S5_GROUPS * S5_P
    xr, xi = e_ref[:, :n], e_ref[:, n:]
    pr, pi = a_ref[:, :n], a_ref[:, n:]
    row = _iota((nchunk, n), 0)
    k = 1
    while k < nchunk:
        sr = jnp.where(row >= k, pltpu.roll(xr, k, 0), 0.0)
        si = jnp.where(row >= k, pltpu.roll(xi, k, 0), 0.0)
        xr, xi = xr + pr * sr - pi * si, xi + pr * si + pi * sr
        pr, pi = pr * pr - pi * pi, 2.0 * pr * pi
        k *= 2
    x_ref[:, :n] = jnp.where(row >= 1, pltpu.roll(xr, 1, 0), 0.0)
    x_ref[:, n:] = jnp.where(row >= 1, pltpu.roll(xi, 1, 0), 0.0)


def _s5_out_kernel(y1_ref, x_ref, a1_ref, cbd_ref, wg_ref, bg_ref, o_ref, xs_ref):
    r = y1_ref.shape[0]
    n = S5_GROUPS * S5_P
    ar, ai = a1_ref[:, :n], a1_ref[:, n:]

    def chunk(c, carry):
        base = pl.multiple_of(c * SUB, SUB)
        xr = x_ref[pl.ds(c, 1), :n]
        xi = x_ref[pl.ds(c, 1), n:]
        xs_ref[pl.ds(base, SUB), :n] = (ar * xr - ai * xi).astype(BF16)
        xs_ref[pl.ds(base, SUB), n:] = (ar * xi + ai * xr).astype(BF16)
        return carry

    lax.fori_loop(0, r // SUB, chunk, 0)
    y = y1_ref[...] + jnp.dot(xs_ref[...], cbd_ref[...], preferred_element_type=F32)
    zact = 0.5 * y * (1.0 + jnp.tanh(math.sqrt(2.0 / math.pi) * (y + 0.044715 * (y * y * y))))
    gate = jnp.dot(zact.astype(BF16), wg_ref[...], preferred_element_type=F32) + bg_ref[...]
    o_ref[...] = zact * _sigmoid(gate)


def _s5_operators(lam_re, lam_im, log_step, b_re, b_im, c_re, c_im):
    hi = lax.Precision.HIGHEST
    step = jnp.exp(log_step.astype(F32))[:, None]
    lre = jnp.minimum(lam_re.astype(F32), -1e-4)
    lim = lam_im.astype(F32)
    mag = jnp.exp(lre * step)
    a_re, a_im = mag * jnp.cos(lim * step), mag * jnp.sin(lim * step)
    den = lre * lre + lim * lim
    coef_re = ((a_re - 1.0) * lre + a_im * lim) / den
    coef_im = (a_im * lre - (a_re - 1.0) * lim) / den
    br, bi = b_re.astype(F32), b_im.astype(F32)
    bb_re = coef_re[..., None] * br - coef_im[..., None] * bi
    bb_im = coef_re[..., None] * bi + coef_im[..., None] * br
    cr, ci = c_re.astype(F32), c_im.astype(F32)
    pows_r, pows_i = [jnp.ones_like(a_re)], [jnp.zeros_like(a_re)]
    for _ in range(SUB):
        pr, pi = pows_r[-1], pows_i[-1]
        pows_r.append(pr * a_re - pi * a_im)
        pows_i.append(pr * a_im + pi * a_re)
    pw_r, pw_i = jnp.stack(pows_r), jnp.stack(pows_i)
    eye = jnp.eye(S5_GROUPS, dtype=F32)
    cp_r = cr[None] * pw_r[:SUB, :, None, :] - ci[None] * pw_i[:SUB, :, None, :]
    cp_i = cr[None] * pw_i[:SUB, :, None, :] + ci[None] * pw_r[:SUB, :, None, :]
    k_lag = (jnp.einsum("jghp,gpk->jghk", cp_r, bb_re, precision=hi)
             - jnp.einsum("jghp,gpk->jghk", cp_i, bb_im, precision=hi))
    bd = jnp.einsum("jghk,gf->jgkfh", k_lag, eye).reshape(SUB, GROUP_W, GROUP_W)
    n = S5_GROUPS * S5_P
    bbd = jnp.concatenate([jnp.einsum("gpk,gf->gkfp", bb_re, eye).reshape(GROUP_W, n),
                           jnp.einsum("gpk,gf->gkfp", bb_im, eye).reshape(GROUP_W, n)], axis=1)
    cbd = jnp.concatenate([jnp.einsum("ghp,gf->gpfh", cr, eye).reshape(n, GROUP_W),
                           -jnp.einsum("ghp,gf->gpfh", ci, eye).reshape(n, GROUP_W)], axis=0)
    w_end = jnp.concatenate([pw_r[SUB - 1::-1].reshape(SUB, n)[:SUB], pw_i[SUB - 1::-1].reshape(SUB, n)[:SUB]], axis=1)
    a_in = jnp.concatenate([pw_r[1:].reshape(SUB, n), pw_i[1:].reshape(SUB, n)], axis=1)
    a_chunk = jnp.concatenate([pw_r[SUB].reshape(1, n), pw_i[SUB].reshape(1, n)], axis=1)
    return bd.astype(BF16), bbd.astype(BF16), cbd.astype(BF16), w_end, a_in, a_chunk


def _s5(u, ops, d_skip, w_glu, b_glu, batch, seq, r):
    t = u.shape[0]
    bd, bbd, cbd, w_end, a_in, a_chunk = ops
    n2 = 2 * S5_GROUPS * S5_P
    nck = seq // SUB

    def whole(a):
        return pl.BlockSpec(a.shape, lambda *_: (0,) * a.ndim)

    rows = pl.BlockSpec((r, GROUP_W), lambda i: (i, 0))
    crow = pl.BlockSpec((r // SUB, n2), lambda i: (i, 0))
    d2 = d_skip.astype(F32).reshape(1, -1)
    y1, e = pl.pallas_call(
        _s5_local_kernel,
        out_shape=(jax.ShapeDtypeStruct((t, GROUP_W), F32), jax.ShapeDtypeStruct((t // SUB, n2), F32)),
        grid=(t // r,),
        in_specs=[rows, whole(bd), whole(bbd), whole(w_end), whole(d2)],
        out_specs=(rows, crow),
        name="s5_local",
        compiler_params=pltpu.CompilerParams(
            dimension_semantics=("arbitrary",),
            vmem_limit_bytes=_vmem_limit(2 * _nbytes(bd.shape, BF16), 2 * _nbytes(bbd.shape, BF16),
                                         6 * _nbytes((r, n2), F32), 8 * _nbytes((r, GROUP_W), F32))),
    )(u, bd, bbd, w_end, d2)
    xin = pl.pallas_call(
        _s5_scan_kernel,
        out_shape=jax.ShapeDtypeStruct((t // SUB, n2), F32),
        grid=(batch,),
        in_specs=[pl.BlockSpec((nck, n2), lambda b: (b, 0)), whole(a_chunk)],
        out_specs=pl.BlockSpec((nck, n2), lambda b: (b, 0)),
        name="s5_scan",
        compiler_params=pltpu.CompilerParams(
            dimension_semantics=("arbitrary",), vmem_limit_bytes=_vmem_limit(10 * _nbytes((nck, n2), F32))),
    )(e, a_chunk)
    wg = w_glu.astype(BF16)
    bg = b_glu.astype(F32).reshape(1, -1)
    return pl.pallas_call(
        _s5_out_kernel,
        out_shape=jax.ShapeDtypeStruct((t, GROUP_W), F32),
        grid=(t // r,),
        in_specs=[rows, crow, whole(a_in), whole(cbd), whole(wg), whole(bg)],
        out_specs=rows,
        scratch_shapes=[pltpu.VMEM((r, n2), BF16)],
        name="s5_out",
        compiler_params=pltpu.CompilerParams(
            dimension_semantics=("arbitrary",),
            vmem_limit_bytes=_vmem_limit(2 * _nbytes(cbd.shape, BF16), 3 * _nbytes((r, n2), BF16),
                                         10 * _nbytes((r, GROUP_W), F32))),
    )(y1, xin, a_in, cbd, wg, bg)


def _outproj_kernel(ya_ref, yb_ref, yc_ref, gate_ref, yd_ref, h_ref, gn_ref, w_ref, o_ref):
    gw = GROUP_W
    parts = (_rms(ya_ref[...], gn_ref[:, 0:gw]),
             _rms(yb_ref[...], gn_ref[:, gw:2 * gw]),
             _rms(yc_ref[...], gn_ref[:, 2 * gw:3 * gw]) * _sigmoid(gate_ref[...]),
             _rms(yd_ref[...], gn_ref[:, 3 * gw:4 * gw]))
    acc = h_ref[...]
    for g, part in enumerate(parts):
        acc = acc + jnp.dot(part.astype(BF16), w_ref[g * gw:(g + 1) * gw, :], preferred_element_type=F32)
    o_ref[...] = acc


def _outproj(ya, yb, yc, hg, yd, h, gn, w_out, tm):
    t = h.shape[0]
    rows = pl.BlockSpec((tm, GROUP_W), lambda i: (i, 0))
    wide = pl.BlockSpec((tm, D_MODEL), lambda i: (i, 0))
    w = w_out.astype(BF16)
    return pl.pallas_call(
        _outproj_kernel,
        out_shape=jax.ShapeDtypeStruct((t, D_MODEL), F32),
        grid=(t // tm,),
        in_specs=[rows, rows, rows, pl.BlockSpec((tm, GROUP_W), lambda i: (i, 3)), rows, wide,
                  pl.BlockSpec((1, D_MODEL), lambda i: (0, 0)), pl.BlockSpec((D_MODEL, D_MODEL), lambda i: (0, 0))],
        out_specs=wide,
        name="outproj",
        compiler_params=pltpu.CompilerParams(
            dimension_semantics=("arbitrary",),
            vmem_limit_bytes=_vmem_limit(2 * _nbytes(w.shape, BF16), 14 * _nbytes((tm, GROUP_W), F32),
                                         6 * _nbytes((tm, D_MODEL), F32))),
    )(ya, yb, yc, hg, yd, h, gn.reshape(1, -1), w)


HALO = 2 * V7X_SUBLANES


def _ffn_kernel(tiles_per_seq, h_ref, halo_ref, g_ref, wg_ref, wv_ref, cwg_ref, cwv_ref, cbg_ref, cbv_ref,
                wd_ref, o_ref, xn_ref, up_ref, acc_ref):
    i = pl.program_id(0)
    j = pl.program_id(1)
    tm = h_ref.shape[0]

    @pl.when(j == 0)
    def _():
        first = i % tiles_per_seq == 0
        xh = _rms(halo_ref[...], g_ref[...])
        xn_ref[0:HALO, :] = jnp.where(first, 0.0, xh).astype(BF16)
        xn_ref[HALO:, :] = _rms(h_ref[...], g_ref[...]).astype(BF16)
        acc_ref[...] = jnp.zeros_like(acc_ref)

    def conv(w_ref, cw_ref, cb_ref):
        up_ref[...] = jnp.dot(xn_ref[...], w_ref[...], preferred_element_type=F32)
        return (cb_ref[...] + cw_ref[0:1, :] * up_ref[pl.ds(HALO - 2, tm), :]
                + cw_ref[1:2, :] * up_ref[pl.ds(HALO - 1, tm), :]
                + cw_ref[2:3, :] * up_ref[pl.ds(HALO, tm), :])

    gate = conv(wg_ref, cwg_ref, cbg_ref)
    act = gate * _sigmoid(gate)
    act = (act * conv(wv_ref, cwv_ref, cbv_ref)).astype(BF16)
    acc_ref[...] += jnp.dot(act, wd_ref[...], preferred_element_type=F32)

    @pl.when(j == pl.num_programs(1) - 1)
    def _():
        o_ref[...] = h_ref[...] + acc_ref[...]


def _ffn(h, seq, g, w_up, conv_w, conv_b, w_down, tm, tn):
    t = h.shape[0]
    nff = D_FF // tn
    wu = w_up.astype(BF16)
    wd = w_down.astype(BF16)
    cb = conv_b.reshape(1, -1)
    per = tm // HALO
    vmem = _vmem_limit(4 * _nbytes((D_MODEL, tn), BF16), 2 * _nbytes((tn, D_MODEL), BF16),
                       5 * _nbytes((tm, D_MODEL), F32), _nbytes((tm, D_MODEL), BF16),
                       6 * _nbytes((tm + HALO, tn), F32))
    return pl.pallas_call(
        functools.partial(_ffn_kernel, seq // tm),
        out_shape=jax.ShapeDtypeStruct((t, D_MODEL), F32),
        grid=(t // tm, nff),
        in_specs=[pl.BlockSpec((tm, D_MODEL), lambda i, j: (i, 0)),
                  pl.BlockSpec((HALO, D_MODEL), lambda i, j: (jnp.maximum(i * per - 1, 0), 0)),
                  pl.BlockSpec((1, D_MODEL), lambda i, j: (0, 0)),
                  pl.BlockSpec((D_MODEL, tn), lambda i, j: (0, j)),
                  pl.BlockSpec((D_MODEL, tn), lambda i, j: (0, j + nff)),
                  pl.BlockSpec((3, tn), lambda i, j: (0, j)),
                  pl.BlockSpec((3, tn), lambda i, j: (0, j + nff)),
                  pl.BlockSpec((1, tn), lambda i, j: (0, j)),
                  pl.BlockSpec((1, tn), lambda i, j: (0, j + nff)),
                  pl.BlockSpec((tn, D_MODEL), lambda i, j: (j, 0))],
        out_specs=pl.BlockSpec((tm, D_MODEL), lambda i, j: (i, 0)),
        scratch_shapes=[pltpu.VMEM((tm + HALO, D_MODEL), BF16), pltpu.VMEM((tm + HALO, tn), F32),
                        pltpu.VMEM((tm, D_MODEL), F32)],
        name="conv_ffn",
        compiler_params=pltpu.CompilerParams(dimension_semantics=("arbitrary", "arbitrary"),
                                             vmem_limit_bytes=vmem),
    )(h, h, g.reshape(1, -1), wu, wu, conv_w, conv_w, cb, cb, wd)


def _ple_kernel(final, h_ref, p_ref, g_ref, wg_ref, wp_ref, fg_ref, o_ref):
    h = h_ref[...]
    gate = _sigmoid(jnp.dot(_rms(h, g_ref[...]).astype(BF16), wg_ref[...], preferred_element_type=F32))
    out = h + gate * jnp.dot(p_ref[...].astype(BF16), wp_ref[...], preferred_element_type=F32)
    o_ref[...] = _rms(out, fg_ref[...]) if final else out


def _ple(h, p, g, w_gate, w_ple, final_g, final, tm):
    t = h.shape[0]
    wide = pl.BlockSpec((tm, D_MODEL), lambda i: (i, 0))
    row = pl.BlockSpec((1, D_MODEL), lambda i: (0, 0))
    wg, wp = w_gate.astype(BF16), w_ple.astype(BF16)
    return pl.pallas_call(
        functools.partial(_ple_kernel, final),
        out_shape=jax.ShapeDtypeStruct((t, D_MODEL), F32),
        grid=(t // tm,),
        in_specs=[wide, pl.BlockSpec((tm, PLE_DIM), lambda i: (i, 0)), row,
                  pl.BlockSpec((D_MODEL, D_MODEL), lambda i: (0, 0)),
                  pl.BlockSpec((PLE_DIM, D_MODEL), lambda i: (0, 0)), row],
        out_specs=wide,
        name="ple",
        compiler_params=pltpu.CompilerParams(
            dimension_semantics=("arbitrary",),
            vmem_limit_bytes=_vmem_limit(2 * _nbytes(wg.shape, BF16), 2 * _nbytes(wp.shape, BF16),
                                         8 * _nbytes((tm, D_MODEL), F32))),
    )(h, p, g.reshape(1, -1), wg, wp, final_g.reshape(1, -1))


def _tiles(seq):
    tm = min(512, seq)
    return dict(tm=tm, tq=min(512, seq), tkv=min(256, seq), r=min(512, seq), tn=1408)


def kernel(x, p, positions, attn_norm_g, w_in, mla_q_norm_g, mla_w_uq, mla_kv_norm_g, mla_w_ukv, fox_b_f,
           hgrn_lb_param, s5_lam_re, s5_lam_im, s5_log_step, s5_b_re, s5_b_im, s5_c_re, s5_c_im, s5_d,
           s5_w_glu, s5_b_glu, group_norm_g, w_out, ffn_norm_g, w_up, conv_w, conv_b, w_down, ple_norm_g,
           w_ple_gate, w_ple, final_norm_g):
    batch, seq, _ = x.shape
    depth = w_in.shape[0]
    t = batch * seq
    ts = _tiles(seq)
    assert seq % ts["tm"] == 0 and seq % ts["tq"] == 0 and ts["tq"] % ts["tkv"] == 0 and seq % ts["r"] == 0
    assert (seq // SUB) & (seq // SUB - 1) == 0, "chunk scan assumes a power-of-two chunk count"

    lb_all = jnp.cumsum(jax.nn.softmax(hgrn_lb_param.astype(F32), axis=0), axis=0)
    lb_all = lb_all - lb_all[0:1]
    tables = _rope_tables(positions, ts["tm"])
    h = x.reshape(t, D_MODEL)
    for i in range(depth):
        mq, mk, mv, fq, fk, fv, hg, su = _inproj(
            h, seq, attn_norm_g[i], w_in[i], mla_q_norm_g[i], mla_w_uq[i], mla_kv_norm_g[i], mla_w_ukv[i],
            fox_b_f[i], tables, ts["tm"])
        y_a = _attention(mq, mk, mv, batch, seq, ts["tq"], ts["tkv"])
        y_b = _attention(fq, fk, fv, batch, seq, ts["tq"], ts["tkv"])
        y_c = _hgrn(hg, lb_all[i], batch, seq, ts["r"])
        ops = _s5_operators(s5_lam_re[i], s5_lam_im[i], s5_log_step[i], s5_b_re[i], s5_b_im[i],
                            s5_c_re[i], s5_c_im[i])
        y_d = _s5(su, ops, s5_d[i], s5_w_glu[i], s5_b_glu[i], batch, seq, ts["r"])
        h = _outproj(y_a, y_b, y_c, hg, y_d, h, group_norm_g[i], w_out[i], ts["tm"])
        h = _ffn(h, seq, ffn_norm_g[i], w_up[i], conv_w[i], conv_b[i], w_down[i], ts["tm"], ts["tn"])
        h = _ple(h, p[i].reshape(t, PLE_DIM), ple_norm_g[i], w_ple_gate[i], w_ple[i], final_norm_g,
                 i == depth - 1, ts["tm"])
    return h.reshape(batch, seq, D_MODEL)
```

```python
import functools
import math

import numpy as np
import jax
import jax.numpy as jnp
from jax import lax
from jax.experimental import pallas as pl
from jax.experimental.pallas import tpu as pltpu

F32 = jnp.float32
BF16 = jnp.bfloat16

D_MODEL = 1024
N_HEADS = 4
HEAD_DIM = 64
GROUP_W = 256
MLA_Q_RANK = 256
MLA_KV_RANK = 128
MLA_NOPE = 64
MLA_ROPE = 32
ROPE_THETA = 10000.0
S5_GROUPS = 16
S5_CH = 16
S5_P = 64
D_FF = 2816
PLE_DIM = 256
EPS = 1e-6
N_IN = 2468

V7X_LANES = 128
V7X_SUBLANES = 8
V7X_VMEM_BYTES = 64 * 1024 * 1024
VMEM_CAP_BYTES = 58 * 1024 * 1024

HEAD_PAD = V7X_LANES
SUB = 16
NEG_BIG = -1e30

SEG_CQ = 0
SEG_CKV = 256
SEG_KR = 384
SEG_FQ = 512
SEG_FK = 1024
SEG_FV = 1536
SEG_FF = 2048
SEG_HG = 2176
SEG_S5 = 3200
N_PERM = 3456
ROPE_LANE0 = 64
BIAS_LANE0 = 64
LOG2E = math.log2(math.e)


def _vmem_limit(*byte_counts):
    need = int(sum(byte_counts))
    return int(min(VMEM_CAP_BYTES, need + need // 4 + (4 << 20)))


def _nbytes(shape, dtype):
    return int(np.prod(shape)) * jnp.dtype(dtype).itemsize


def _rms(x, g):
    return x * lax.rsqrt(jnp.mean(x * x, axis=-1, keepdims=True) + EPS) * g


def _log_sigmoid(z):
    return jnp.minimum(z, 0.0) - jnp.log1p(jnp.exp(-jnp.abs(z)))


def _sigmoid(z):
    return 1.0 / (1.0 + jnp.exp(-z))


def _iota(shape, dim):
    return lax.broadcasted_iota(jnp.int32, shape, dim)


def _shift_rows(x, k):
    if k == 0:
        return x
    return pltpu.roll(x, k, 0)


def _rope_kernel(pos_ref, freq_ref, ct_ref, s1_ref, s2_ref):
    ang = pos_ref[...].astype(F32) * freq_ref[...]
    lane = _iota(ang.shape, 1)
    half = MLA_ROPE // 2
    sin = jnp.sin(ang)
    ct_ref[...] = jnp.cos(ang)
    s1_ref[...] = jnp.where((lane >= ROPE_LANE0) & (lane < ROPE_LANE0 + half), -sin, 0.0)
    s2_ref[...] = jnp.where((lane >= ROPE_LANE0 + half) & (lane < ROPE_LANE0 + 2 * half), sin, 0.0)


def _rope_tables(positions, tm):
    t = positions.size
    half = MLA_ROPE // 2
    inv_freq = ROPE_THETA ** (-jnp.arange(half, dtype=F32) / half)
    freq = jnp.zeros((1, HEAD_PAD), F32).at[0, ROPE_LANE0:ROPE_LANE0 + 2 * half].set(jnp.tile(inv_freq, 2))
    pos = positions.reshape(t, 1)
    out = jax.ShapeDtypeStruct((t, HEAD_PAD), F32)
    spec = pl.BlockSpec((tm, HEAD_PAD), lambda i: (i, 0))
    return pl.pallas_call(
        _rope_kernel,
        out_shape=(out, out, out),
        grid=(t // tm,),
        in_specs=[pl.BlockSpec((tm, 1), lambda i: (i, 0)), pl.BlockSpec((1, HEAD_PAD), lambda i: (0, 0))],
        out_specs=(spec, spec, spec),
        name="rope_tables",
        compiler_params=pltpu.CompilerParams(dimension_semantics=("arbitrary",)),
    )(pos, freq)


def _inproj_kernel(tiles_per_seq, h_ref, g_ref, w_ref, qg_ref, wuq_ref, kvg_ref, wukv_ref, bf_ref,
                   ct_ref, s1_ref, s2_ref, selq_ref, selk_ref,
                   mq_ref, mk_ref, mv_ref, fq_ref, fk_ref, fv_ref, hg_ref, su_ref, carry_ref):
    i = pl.program_id(0)

    @pl.when(i % tiles_per_seq == 0)
    def _():
        carry_ref[...] = jnp.zeros_like(carry_ref)

    tm = h_ref.shape[0]
    xn = _rms(h_ref[...], g_ref[...]).astype(BF16)

    def seg(a, b):
        return jnp.dot(xn, w_ref[:, a:b], preferred_element_type=F32)

    q = jnp.dot(_rms(seg(SEG_CQ, SEG_CKV), qg_ref[...]).astype(BF16), wuq_ref[...],
                preferred_element_type=F32)
    kv = jnp.dot(_rms(seg(SEG_CKV, SEG_KR), kvg_ref[...]).astype(BF16), wukv_ref[...],
                 preferred_element_type=F32)
    ct, s1, s2 = ct_ref[...], s1_ref[...], s2_ref[...]
    half = MLA_ROPE // 2

    def rope(t):
        return t * ct + pltpu.roll(t, HEAD_PAD - half, 1) * s1 + pltpu.roll(t, half, 1) * s2

    k_pe = rope(seg(SEG_KR, SEG_FQ))
    mla_scale = (MLA_NOPE + MLA_ROPE) ** -0.5 * LOG2E
    for hd in range(N_HEADS):
        sl = slice(hd * HEAD_PAD, (hd + 1) * HEAD_PAD)
        mq_ref[:, sl] = (rope(q[:, sl]) * mla_scale).astype(BF16)
        mk_ref[:, sl] = (kv[:, sl] + k_pe).astype(BF16)
    hp4 = N_HEADS * HEAD_PAD
    ones_pad = jnp.where((_iota((1, hp4), 1) & (HEAD_PAD - 1)) >= HEAD_DIM, 1.0, 0.0)
    mv_ref[...] = (kv[:, hp4:] + ones_pad).astype(BF16)

    lf = _log_sigmoid(seg(SEG_FF, SEG_HG) + bf_ref[...])
    tril = (_iota((tm, tm), 0) >= _iota((tm, tm), 1)).astype(BF16)

    def split3(x):
        a = x.astype(BF16)
        r = x - a.astype(F32)
        b = r.astype(BF16)
        c = (r - b.astype(F32)).astype(BF16)
        return a, b, c

    l1, l2, l3 = split3(lf)
    cum = (jnp.dot(tril, l1, preferred_element_type=F32) + jnp.dot(tril, l2, preferred_element_type=F32)
           + jnp.dot(tril, l3, preferred_element_type=F32)) + carry_ref[...]
    carry_ref[...] = cum[tm - 1:tm, :]
    lane = _iota(cum.shape, 1)
    c1, c2, c3 = split3(cum * LOG2E)
    keep = lane < N_HEADS
    parts = (jnp.where(keep, c1.astype(F32), 0.0)
             + pltpu.roll(jnp.where(keep, c2.astype(F32), 0.0), N_HEADS, 1)
             + pltpu.roll(jnp.where(keep, c3.astype(F32), 0.0), 2 * N_HEADS, 1)
             + jnp.where(lane == 3 * N_HEADS, 1.0, 0.0)).astype(BF16)
    fox_scale = HEAD_DIM ** -0.5 * LOG2E
    fq_ref[...] = (seg(SEG_FQ, SEG_FK) * fox_scale
                   + jnp.dot(parts, selq_ref[...], preferred_element_type=F32)).astype(BF16)
    fk_ref[...] = (seg(SEG_FK, SEG_FV)
                   + jnp.dot(parts, selk_ref[...], preferred_element_type=F32)).astype(BF16)
    fv_ref[...] = (seg(SEG_FV, SEG_FF) + ones_pad).astype(BF16)

    hg_ref[...] = seg(SEG_HG, SEG_S5)
    su_ref[...] = seg(SEG_S5, N_PERM)


def _inproj_index():
    zero = N_IN
    idx = list(range(0, 384))
    kr = [zero] * HEAD_PAD
    kr[ROPE_LANE0:ROPE_LANE0 + MLA_ROPE] = range(384, 416)
    idx += kr
    for base in (416, 672, 928):
        for hd in range(N_HEADS):
            blk = [zero] * HEAD_PAD
            blk[0:HEAD_DIM] = range(base + HEAD_DIM * hd, base + HEAD_DIM * (hd + 1))
            idx += blk
    ff = [zero] * HEAD_PAD
    ff[0:N_HEADS] = range(1184, 1188)
    idx += ff
    idx += range(1188, N_IN)
    assert len(idx) == N_PERM
    return np.asarray(idx, np.int32)


def _bias_selectors():
    selq = np.zeros((HEAD_PAD, N_HEADS * HEAD_PAD), np.float32)
    selk = np.zeros((HEAD_PAD, N_HEADS * HEAD_PAD), np.float32)
    one = 3 * N_HEADS
    for hd in range(N_HEADS):
        for j in range(3):
            selq[N_HEADS * j + hd, hd * HEAD_PAD + BIAS_LANE0 + j] = 1.0
            selq[one, hd * HEAD_PAD + BIAS_LANE0 + 3 + j] = 1.0
            selk[one, hd * HEAD_PAD + BIAS_LANE0 + j] = 1.0
            selk[N_HEADS * j + hd, hd * HEAD_PAD + BIAS_LANE0 + 3 + j] = -1.0
    return jnp.asarray(selq, BF16), jnp.asarray(selk, BF16)


def _inproj(h, seq, attn_g, w_in, q_g, w_uq, kv_g, w_ukv, b_f, tables, tm):
    t = h.shape[0]
    w_perm = jnp.take(jnp.concatenate([w_in, jnp.zeros((D_MODEL, 1), w_in.dtype)], axis=1),
                      _inproj_index(), axis=1).astype(BF16)
    wuq = jnp.pad(w_uq.reshape(MLA_Q_RANK, N_HEADS, MLA_NOPE + MLA_ROPE),
                  ((0, 0), (0, 0), (0, HEAD_PAD - MLA_NOPE - MLA_ROPE))).reshape(MLA_Q_RANK, -1).astype(BF16)
    wkv = w_ukv.reshape(MLA_KV_RANK, N_HEADS, 2 * HEAD_DIM)
    head_pad = ((0, 0), (0, 0), (0, HEAD_PAD - HEAD_DIM))
    wk = jnp.pad(wkv[:, :, :MLA_NOPE], head_pad).reshape(MLA_KV_RANK, -1)
    wv = jnp.pad(wkv[:, :, MLA_NOPE:], head_pad).reshape(MLA_KV_RANK, -1)
    wukv = jnp.concatenate([wk, wv], axis=1).astype(BF16)
    bf = jnp.zeros((1, HEAD_PAD), F32).at[0, :N_HEADS].set(b_f.astype(F32))
    selq, selk = _bias_selectors()
    ct, s1, s2 = tables
    hp4 = N_HEADS * HEAD_PAD

    def rows(width):
        return pl.BlockSpec((tm, width), lambda i: (i, 0))

    def whole(a):
        return pl.BlockSpec(a.shape, lambda i: (0,) * a.ndim)

    args = (h, attn_g.reshape(1, -1), w_perm, q_g.reshape(1, -1), wuq, kv_g.reshape(1, -1), wukv, bf,
            ct, s1, s2, selq, selk)
    in_specs = [rows(D_MODEL)] + [whole(a) for a in args[1:8]] + [rows(HEAD_PAD)] * 3 + [whole(selq), whole(selk)]
    out_widths = (hp4,) * 6
    out_shape = tuple(jax.ShapeDtypeStruct((t, w), BF16) for w in out_widths) + (
        jax.ShapeDtypeStruct((t, 4 * GROUP_W), F32), jax.ShapeDtypeStruct((t, GROUP_W), F32))
    out_specs = tuple(rows(w) for w in out_widths) + (rows(4 * GROUP_W), rows(GROUP_W))
    vmem = _vmem_limit(2 * _nbytes(w_perm.shape, BF16), 2 * _nbytes((tm, D_MODEL), F32),
                       2 * sum(_nbytes((tm, w), BF16) for w in out_widths),
                       2 * _nbytes((tm, 5 * GROUP_W), F32), _nbytes((tm, N_PERM), F32),
                       _nbytes((tm, D_MODEL), F32))
    return pl.pallas_call(
        functools.partial(_inproj_kernel, seq // tm),
        out_shape=out_shape,
        grid=(t // tm,),
        in_specs=in_specs,
        out_specs=out_specs,
        scratch_shapes=[pltpu.VMEM((1, HEAD_PAD), F32)],
        name="inproj",
        compiler_params=pltpu.CompilerParams(dimension_semantics=("arbitrary",), vmem_limit_bytes=vmem),
    )(*args)


def _attn_kernel(tq, tkv, q_ref, k_ref, v_ref, o_ref, m_ref, acc_ref):
    qi = pl.program_id(1)
    m_ref[...] = jnp.full_like(m_ref, NEG_BIG)
    acc_ref[...] = jnp.zeros_like(acc_ref)
    per_q = tq // tkv
    reps = tkv // HEAD_PAD

    def block(j, diag_offset):
        rows = pl.ds(pl.multiple_of(j * tkv, tkv), tkv)
        for hd in range(N_HEADS):
            sl = slice(hd * HEAD_PAD, (hd + 1) * HEAD_PAD)
            s = lax.dot_general(q_ref[:, sl], k_ref[rows, sl], (((1,), (1,)), ((), ())),
                                preferred_element_type=F32)
            if diag_offset is not None:
                visible = _iota(s.shape, 1) + diag_offset <= _iota(s.shape, 0)
                s = jnp.where(visible, s, NEG_BIG)
            m_old = m_ref[hd]
            m_new = jnp.maximum(m_old, jnp.max(s, axis=1, keepdims=True))
            p = jnp.exp2(s - jnp.concatenate([m_new] * reps, axis=1))
            m_ref[hd] = m_new
            acc_ref[hd] = (jnp.exp2(m_old - m_new) * acc_ref[hd]
                           + jnp.dot(p.astype(BF16), v_ref[rows, sl], preferred_element_type=F32))

    def full_block(j, carry):
        block(j, None)
        return carry

    lax.fori_loop(0, qi * per_q, full_block, 0)
    for r in range(per_q):
        block(qi * per_q + r, r * tkv)

    low = _iota((tq, HEAD_PAD), 1) < HEAD_DIM
    for pair in range(N_HEADS // 2):
        a0, a1 = acc_ref[2 * pair], acc_ref[2 * pair + 1]
        n0 = a0 / pltpu.roll(a0, HEAD_DIM, 1)
        n1 = a1 / pltpu.roll(a1, HEAD_DIM, 1)
        o_ref[:, pair * HEAD_PAD:(pair + 1) * HEAD_PAD] = jnp.where(low, n0, pltpu.roll(n1, HEAD_DIM, 1))


def _attention(q, k, v, batch, seq, tq, tkv):
    t = q.shape[0]
    hp4 = N_HEADS * HEAD_PAD
    vmem = _vmem_limit(4 * _nbytes((seq, hp4), BF16), 2 * _nbytes((tq, hp4), BF16),
                       2 * _nbytes((tq, GROUP_W), F32), 2 * N_HEADS * _nbytes((tq, HEAD_PAD), F32),
                       6 * _nbytes((tq, tkv), F32))
    return pl.pallas_call(
        functools.partial(_attn_kernel, tq, tkv),
        out_shape=jax.ShapeDtypeStruct((t, GROUP_W), F32),
        grid=(batch, seq // tq),
        in_specs=[pl.BlockSpec((tq, hp4), lambda b, i: (b * (seq // tq) + i, 0)),
                  pl.BlockSpec((seq, hp4), lambda b, i: (b, 0)),
                  pl.BlockSpec((seq, hp4), lambda b, i: (b, 0))],
        out_specs=pl.BlockSpec((tq, GROUP_W), lambda b, i: (b * (seq // tq) + i, 0)),
        scratch_shapes=[pltpu.VMEM((N_HEADS, tq, HEAD_PAD), F32), pltpu.VMEM((N_HEADS, tq, HEAD_PAD), F32)],
        name="causal_attention",
        compiler_params=pltpu.CompilerParams(dimension_semantics=("arbitrary", "arbitrary"),
                                             vmem_limit_bytes=vmem),
    )(q, k, v)


def _segment_cumsum(x, rsub, reverse=False):
    n = x.shape[0]
    k = 1
    while k < SUB:
        if reverse:
            x = x + jnp.where(rsub < SUB - k, pltpu.roll(x, n - k, 0), 0.0)
        else:
            x = x + jnp.where(rsub >= k, pltpu.roll(x, k, 0), 0.0)
        k *= 2
    return x


def _hgrn_kernel(win, q_ref, f_ref, v_ref, lb_ref, ee_ref, o_ref,
                 st_ref, bc_ref, ke_ref, vt_ref, od_ref):
    @pl.when(pl.program_id(1) == 0)
    def _():
        st_ref[...] = jnp.zeros_like(st_ref)

    r = q_ref.shape[0]
    z = f_ref[...]
    lb = lb_ref[...]
    a = jnp.log(lb)
    b = jnp.log1p(-lb) + _log_sigmoid(z)
    log_f = jnp.maximum(a, b) + jnp.log1p(jnp.exp(-jnp.abs(a - b)))
    kk = (1.0 - lb) * _sigmoid(-z)
    rsub = _iota((r, GROUP_W), 0) & (SUB - 1)
    bc = _segment_cumsum(log_f, rsub)
    tail = _segment_cumsum(log_f, rsub, reverse=True) - log_f
    q = q_ref[...]
    v = v_ref[...]

    od = jnp.zeros((r, GROUP_W), F32)
    for d in range(SUB):
        e = jnp.where(rsub >= d, bc - _shift_rows(bc, d), -jnp.inf)
        prod = (q * _shift_rows(kk, d) * jnp.exp(e)).astype(BF16)
        od = od + jnp.dot(prod, ee_ref[...], preferred_element_type=F32) * _shift_rows(v, d)
    od_ref[...] = od
    bc_ref[...] = bc
    ke_ref[...] = kk * jnp.exp(tail)
    for w in range(r // win):
        vt_ref[w] = v[w * win:(w + 1) * win, :].T.astype(BF16)

    lane_head = _iota((GROUP_W, GROUP_W), 1) // HEAD_DIM
    row_head = _iota((GROUP_W, GROUP_W), 0) // HEAD_DIM
    same_head = lane_head == row_head
    wrow = _iota((win, GROUP_W), 0)

    def step(c, carry):
        base = pl.multiple_of(c * SUB, SUB)
        w = c // (win // SUB)
        wbase = pl.multiple_of(w * win, win)
        bcc = bc_ref[pl.ds(base, SUB), :]
        st = st_ref[...]
        qe = (q_ref[pl.ds(base, SUB), :] * jnp.exp(bcc)).astype(BF16)
        o_state = lax.dot_general(qe, st.astype(BF16), (((1,), (1,)), ((), ())), preferred_element_type=F32)
        o_ref[pl.ds(base, SUB), :] = od_ref[pl.ds(base, SUB), :] + o_state
        in_sub = (wrow >= base - wbase) & (wrow < base - wbase + SUB)
        ke = jnp.where(in_sub, ke_ref[pl.ds(wbase, win), :], 0.0).astype(BF16)
        upd = jnp.dot(vt_ref[w], ke, preferred_element_type=F32)
        st_ref[...] = st * jnp.exp(bcc[SUB - 1:SUB, :]) + jnp.where(same_head, upd, 0.0)
        return carry

    lax.fori_loop(0, r // SUB, step, 0)


def _hgrn(hg, lb, batch, seq, r):
    t = hg.shape[0]
    win = min(r, V7X_LANES)
    ee = jnp.asarray(np.kron(np.eye(N_HEADS, dtype=np.float32), np.ones((HEAD_DIM, HEAD_DIM), np.float32)), BF16)
    nblk = seq // r

    def col(c):
        return pl.BlockSpec((r, GROUP_W), lambda b, i: (b * nblk + i, c))

    vmem = _vmem_limit(6 * _nbytes((r, GROUP_W), F32), 2 * _nbytes((r, GROUP_W), F32),
                       4 * _nbytes((r, GROUP_W), F32), 12 * _nbytes((r, GROUP_W), F32))
    return pl.pallas_call(
        functools.partial(_hgrn_kernel, win),
        out_shape=jax.ShapeDtypeStruct((t, GROUP_W), F32),
        grid=(batch, nblk),
        in_specs=[col(0), col(1), col(2), pl.BlockSpec((1, GROUP_W), lambda b, i: (0, 0)),
                  pl.BlockSpec((GROUP_W, GROUP_W), lambda b, i: (0, 0))],
        out_specs=pl.BlockSpec((r, GROUP_W), lambda b, i: (b * nblk + i, 0)),
        scratch_shapes=[pltpu.VMEM((GROUP_W, GROUP_W), F32), pltpu.VMEM((r, GROUP_W), F32),
                        pltpu.VMEM((r, GROUP_W), F32), pltpu.VMEM((r // win, GROUP_W, win), BF16),
                        pltpu.VMEM((r, GROUP_W), F32)],
        name="hgrn2",
        compiler_params=pltpu.CompilerParams(dimension_semantics=("arbitrary", "arbitrary"),
                                             vmem_limit_bytes=vmem),
    )(hg, hg, hg, lb.reshape(1, -1), ee)


def _s5_local_kernel(u_ref, bd_ref, bbd_ref, we_ref, d_ref, y_ref, e_ref):
    r = u_ref.shape[0]
    u = u_ref[...]
    rsub = _iota((r, GROUP_W), 0) & (SUB - 1)
    y = d_ref[...] * u
    for j in range(SUB):
        uj = jnp.where(rsub >= j, _shift_rows(u, j), 0.0).astype(BF16)
        y = y + jnp.dot(uj, bd_ref[j], preferred_element_type=F32)
    y_ref[...] = y
    n = S5_GROUPS * S5_P
    bu = jnp.dot(u.astype(BF16), bbd_ref[...], preferred_element_type=F32).reshape(r // SUB, SUB, 2 * n)
    bur, bui = bu[:, :, :n], bu[:, :, n:]
    wr, wi = we_ref[:, :n], we_ref[:, n:]
    e_ref[:, :n] = jnp.sum(wr * bur - wi * bui, axis=1)
    e_ref[:, n:] = jnp.sum(wr * bui + wi * bur, axis=1)


def _s5_scan_kernel(e_ref, a_ref, x_ref):
    nchunk = e_ref.shape[0]
    n = S5_GROUPS * S5_P
    xr, xi = e_ref[:, :n], e_ref[:, n:]
    pr, pi = a_ref[:, :n], a_ref[:, n:]
    row = _iota((nchunk, n), 0)
    k = 1
    while k < nchunk:
        sr = jnp.where(row >= k, pltpu.roll(xr, k, 0), 0.0)
        si = jnp.where(row >= k, pltpu.roll(xi, k, 0), 0.0)
        xr, xi = xr + pr * sr - pi * si, xi + pr * si + pi * sr
        pr, pi = pr * pr - pi * pi, 2.0 * pr * pi
        k *= 2
    x_ref[:, :n] = jnp.where(row >= 1, pltpu.roll(xr, 1, 0), 0.0)
    x_ref[:, n:] = jnp.where(row >= 1, pltpu.roll(xi, 1, 0), 0.0)


def _s5_out_kernel(y1_ref, x_ref, a1_ref, cbd_ref, wg_ref, bg_ref, o_ref, xs_ref):
    r = y1_ref.shape[0]
    n = S5_GROUPS * S5_P
    ar, ai = a1_ref[:, :n], a1_ref[:, n:]

    def chunk(c, carry):
        base = pl.multiple_of(c * SUB, SUB)
        xr = x_ref[pl.ds(c, 1), :n]
        xi = x_ref[pl.ds(c, 1), n:]
        xs_ref[pl.ds(base, SUB), :n] = (ar * xr - ai * xi).astype(BF16)
        xs_ref[pl.ds(base, SUB), n:] = (ar * xi + ai * xr).astype(BF16)
        return carry

    lax.fori_loop(0, r // SUB, chunk, 0)
    y = y1_ref[...] + jnp.dot(xs_ref[...], cbd_ref[...], preferred_element_type=F32)
    zact = 0.5 * y * (1.0 + jnp.tanh(math.sqrt(2.0 / math.pi) * (y + 0.044715 * (y * y * y))))
    gate = jnp.dot(zact.astype(BF16), wg_ref[...], preferred_element_type=F32) + bg_ref[...]
    o_ref[...] = zact * _sigmoid(gate)


def _s5_operators(lam_re, lam_im, log_step, b_re, b_im, c_re, c_im):
    hi = lax.Precision.HIGHEST
    step = jnp.exp(log_step.astype(F32))[:, None]
    lre = jnp.minimum(lam_re.astype(F32), -1e-4)
    lim = lam_im.astype(F32)
    mag = jnp.exp(lre * step)
    a_re, a_im = mag * jnp.cos(lim * step), mag * jnp.sin(lim * step)
    den = lre * lre + lim * lim
    coef_re = ((a_re - 1.0) * lre + a_im * lim) / den
    coef_im = (a_im * lre - (a_re - 1.0) * lim) / den
    br, bi = b_re.astype(F32), b_im.astype(F32)
    bb_re = coef_re[..., None] * br - coef_im[..., None] * bi
    bb_im = coef_re[..., None] * bi + coef_im[..., None] * br
    cr, ci = c_re.astype(F32), c_im.astype(F32)
    pows_r, pows_i = [jnp.ones_like(a_re)], [jnp.zeros_like(a_re)]
    for _ in range(SUB):
        pr, pi = pows_r[-1], pows_i[-1]
        pows_r.append(pr * a_re - pi * a_im)
        pows_i.append(pr * a_im + pi * a_re)
    pw_r, pw_i = jnp.stack(pows_r), jnp.stack(pows_i)
    eye = jnp.eye(S5_GROUPS, dtype=F32)
    cp_r = cr[None] * pw_r[:SUB, :, None, :] - ci[None] * pw_i[:SUB, :, None, :]
    cp_i = cr[None] * pw_i[:SUB, :, None, :] + ci[None] * pw_r[:SUB, :, None, :]
    k_lag = (jnp.einsum("jghp,gpk->jghk", cp_r, bb_re, precision=hi)
             - jnp.einsum("jghp,gpk->jghk", cp_i, bb_im, precision=hi))
    bd = jnp.einsum("jghk,gf->jgkfh", k_lag, eye).reshape(SUB, GROUP_W, GROUP_W)
    n = S5_GROUPS * S5_P
    bbd = jnp.concatenate([jnp.einsum("gpk,gf->gkfp", bb_re, eye).reshape(GROUP_W, n),
                           jnp.einsum("gpk,gf->gkfp", bb_im, eye).reshape(GROUP_W, n)], axis=1)
    cbd = jnp.concatenate([jnp.einsum("ghp,gf->gpfh", cr, eye).reshape(n, GROUP_W),
                           -jnp.einsum("ghp,gf->gpfh", ci, eye).reshape(n, GROUP_W)], axis=0)
    w_end = jnp.concatenate([pw_r[SUB - 1::-1].reshape(SUB, n)[:SUB], pw_i[SUB - 1::-1].reshape(SUB, n)[:SUB]], axis=1)
    a_in = jnp.concatenate([pw_r[1:].reshape(SUB, n), pw_i[1:].reshape(SUB, n)], axis=1)
    a_chunk = jnp.concatenate([pw_r[SUB].reshape(1, n), pw_i[SUB].reshape(1, n)], axis=1)
    return bd.astype(BF16), bbd.astype(BF16), cbd.astype(BF16), w_end, a_in, a_chunk


def _s5(u, ops, d_skip, w_glu, b_glu, batch, seq, r):
    t = u.shape[0]
    bd, bbd, cbd, w_end, a_in, a_chunk = ops
    n2 = 2 * S5_GROUPS * S5_P
    nck = seq // SUB

    def whole(a):
        return pl.BlockSpec(a.shape, lambda *_: (0,) * a.ndim)

    rows = pl.BlockSpec((r, GROUP_W), lambda i: (i, 0))
    crow = pl.BlockSpec((r // SUB, n2), lambda i: (i, 0))
    d2 = d_skip.astype(F32).reshape(1, -1)
    y1, e = pl.pallas_call(
        _s5_local_kernel,
        out_shape=(jax.ShapeDtypeStruct((t, GROUP_W), F32), jax.ShapeDtypeStruct((t // SUB, n2), F32)),
        grid=(t // r,),
        in_specs=[rows, whole(bd), whole(bbd), whole(w_end), whole(d2)],
        out_specs=(rows, crow),
        name="s5_local",
        compiler_params=pltpu.CompilerParams(
            dimension_semantics=("arbitrary",),
            vmem_limit_bytes=_vmem_limit(2 * _nbytes(bd.shape, BF16), 2 * _nbytes(bbd.shape, BF16),
                                         6 * _nbytes((r, n2), F32), 8 * _nbytes((r, GROUP_W), F32))),
    )(u, bd, bbd, w_end, d2)
    xin = pl.pallas_call(
        _s5_scan_kernel,
        out_shape=jax.ShapeDtypeStruct((t // SUB, n2), F32),
        grid=(batch,),
        in_specs=[pl.BlockSpec((nck, n2), lambda b: (b, 0)), whole(a_chunk)],
        out_specs=pl.BlockSpec((nck, n2), lambda b: (b, 0)),
        name="s5_scan",
        compiler_params=pltpu.CompilerParams(
            dimension_semantics=("arbitrary",), vmem_limit_bytes=_vmem_limit(10 * _nbytes((nck, n2), F32))),
    )(e, a_chunk)
    wg = w_glu.astype(BF16)
    bg = b_glu.astype(F32).reshape(1, -1)
    return pl.pallas_call(
        _s5_out_kernel,
        out_shape=jax.ShapeDtypeStruct((t, GROUP_W), F32),
        grid=(t // r,),
        in_specs=[rows, crow, whole(a_in), whole(cbd), whole(wg), whole(bg)],
        out_specs=rows,
        scratch_shapes=[pltpu.VMEM((r, n2), BF16)],
        name="s5_out",
        compiler_params=pltpu.CompilerParams(
            dimension_semantics=("arbitrary",),
            vmem_limit_bytes=_vmem_limit(2 * _nbytes(cbd.shape, BF16), 3 * _nbytes((r, n2), BF16),
                                         10 * _nbytes((r, GROUP_W), F32))),
    )(y1, xin, a_in, cbd, wg, bg)


def _outproj_kernel(ya_ref, yb_ref, yc_ref, gate_ref, yd_ref, h_ref, gn_ref, w_ref, o_ref):
    gw = GROUP_W
    parts = (_rms(ya_ref[...], gn_ref[:, 0:gw]),
             _rms(yb_ref[...], gn_ref[:, gw:2 * gw]),
             _rms(yc_ref[...], gn_ref[:, 2 * gw:3 * gw]) * _sigmoid(gate_ref[...]),
             _rms(yd_ref[...], gn_ref[:, 3 * gw:4 * gw]))
    acc = h_ref[...]
    for g, part in enumerate(parts):
        acc = acc + jnp.dot(part.astype(BF16), w_ref[g * gw:(g + 1) * gw, :], preferred_element_type=F32)
    o_ref[...] = acc


def _outproj(ya, yb, yc, hg, yd, h, gn, w_out, tm):
    t = h.shape[0]
    rows = pl.BlockSpec((tm, GROUP_W), lambda i: (i, 0))
    wide = pl.BlockSpec((tm, D_MODEL), lambda i: (i, 0))
    w = w_out.astype(BF16)
    return pl.pallas_call(
        _outproj_kernel,
        out_shape=jax.ShapeDtypeStruct((t, D_MODEL), F32),
        grid=(t // tm,),
        in_specs=[rows, rows, rows, pl.BlockSpec((tm, GROUP_W), lambda i: (i, 3)), rows, wide,
                  pl.BlockSpec((1, D_MODEL), lambda i: (0, 0)), pl.BlockSpec((D_MODEL, D_MODEL), lambda i: (0, 0))],
        out_specs=wide,
        name="outproj",
        compiler_params=pltpu.CompilerParams(
            dimension_semantics=("arbitrary",),
            vmem_limit_bytes=_vmem_limit(2 * _nbytes(w.shape, BF16), 14 * _nbytes((tm, GROUP_W), F32),
                                         6 * _nbytes((tm, D_MODEL), F32))),
    )(ya, yb, yc, hg, yd, h, gn.reshape(1, -1), w)


HALO = 2 * V7X_SUBLANES


def _ffn_kernel(tiles_per_seq, h_ref, halo_ref, g_ref, wg_ref, wv_ref, cwg_ref, cwv_ref, cbg_ref, cbv_ref,
                wd_ref, o_ref, xn_ref, up_ref, acc_ref):
    i = pl.program_id(0)
    j = pl.program_id(1)
    tm = h_ref.shape[0]

    @pl.when(j == 0)
    def _():
        first = i % tiles_per_seq == 0
        xh = _rms(halo_ref[...], g_ref[...])
        xn_ref[0:HALO, :] = jnp.where(first, 0.0, xh).astype(BF16)
        xn_ref[HALO:, :] = _rms(h_ref[...], g_ref[...]).astype(BF16)
        acc_ref[...] = jnp.zeros_like(acc_ref)

    def conv(w_ref, cw_ref, cb_ref):
        up_ref[...] = jnp.dot(xn_ref[...], w_ref[...], preferred_element_type=F32)
        return (cb_ref[...] + cw_ref[0:1, :] * up_ref[pl.ds(HALO - 2, tm), :]
                + cw_ref[1:2, :] * up_ref[pl.ds(HALO - 1, tm), :]
                + cw_ref[2:3, :] * up_ref[pl.ds(HALO, tm), :])

    gate = conv(wg_ref, cwg_ref, cbg_ref)
    act = gate * _sigmoid(gate)
    act = (act * conv(wv_ref, cwv_ref, cbv_ref)).astype(BF16)
    acc_ref[...] += jnp.dot(act, wd_ref[...], preferred_element_type=F32)

    @pl.when(j == pl.num_programs(1) - 1)
    def _():
        o_ref[...] = h_ref[...] + acc_ref[...]


def _ffn(h, seq, g, w_up, conv_w, conv_b, w_down, tm, tn):
    t = h.shape[0]
    nff = D_FF // tn
    wu = w_up.astype(BF16)
    wd = w_down.astype(BF16)
    cb = conv_b.reshape(1, -1)
    per = tm // HALO
    vmem = _vmem_limit(4 * _nbytes((D_MODEL, tn), BF16), 2 * _nbytes((tn, D_MODEL), BF16),
                       5 * _nbytes((tm, D_MODEL), F32), _nbytes((tm, D_MODEL), BF16),
                       6 * _nbytes((tm + HALO, tn), F32))
    return pl.pallas_call(
        functools.partial(_ffn_kernel, seq // tm),
        out_shape=jax.ShapeDtypeStruct((t, D_MODEL), F32),
        grid=(t // tm, nff),
        in_specs=[pl.BlockSpec((tm, D_MODEL), lambda i, j: (i, 0)),
                  pl.BlockSpec((HALO, D_MODEL), lambda i, j: (jnp.maximum(i * per - 1, 0), 0)),
                  pl.BlockSpec((1, D_MODEL), lambda i, j: (0, 0)),
                  pl.BlockSpec((D_MODEL, tn), lambda i, j: (0, j)),
                  pl.BlockSpec((D_MODEL, tn), lambda i, j: (0, j + nff)),
                  pl.BlockSpec((3, tn), lambda i, j: (0, j)),
                  pl.BlockSpec((3, tn), lambda i, j: (0, j + nff)),
                  pl.BlockSpec((1, tn), lambda i, j: (0, j)),
                  pl.BlockSpec((1, tn), lambda i, j: (0, j + nff)),
                  pl.BlockSpec((tn, D_MODEL), lambda i, j: (j, 0))],
        out_specs=pl.BlockSpec((tm, D_MODEL), lambda i, j: (i, 0)),
        scratch_shapes=[pltpu.VMEM((tm + HALO, D_MODEL), BF16), pltpu.VMEM((tm + HALO, tn), F32),
                        pltpu.VMEM((tm, D_MODEL), F32)],
        name="conv_ffn",
        compiler_params=pltpu.CompilerParams(dimension_semantics=("arbitrary", "arbitrary"),
                                             vmem_limit_bytes=vmem),
    )(h, h, g.reshape(1, -1), wu, wu, conv_w, conv_w, cb, cb, wd)


def _ple_kernel(final, h_ref, p_ref, g_ref, wg_ref, wp_ref, fg_ref, o_ref):
    h = h_ref[...]
    gate = _sigmoid(jnp.dot(_rms(h, g_ref[...]).astype(BF16), wg_ref[...], preferred_element_type=F32))
    out = h + gate * jnp.dot(p_ref[...].astype(BF16), wp_ref[...], preferred_element_type=F32)
    o_ref[...] = _rms(out, fg_ref[...]) if final else out


def _ple(h, p, g, w_gate, w_ple, final_g, final, tm):
    t = h.shape[0]
    wide = pl.BlockSpec((tm, D_MODEL), lambda i: (i, 0))
    row = pl.BlockSpec((1, D_MODEL), lambda i: (0, 0))
    wg, wp = w_gate.astype(BF16), w_ple.astype(BF16)
    return pl.pallas_call(
        functools.partial(_ple_kernel, final),
        out_shape=jax.ShapeDtypeStruct((t, D_MODEL), F32),
        grid=(t // tm,),
        in_specs=[wide, pl.BlockSpec((tm, PLE_DIM), lambda i: (i, 0)), row,
                  pl.BlockSpec((D_MODEL, D_MODEL), lambda i: (0, 0)),
                  pl.BlockSpec((PLE_DIM, D_MODEL), lambda i: (0, 0)), row],
        out_specs=wide,
        name="ple",
        compiler_params=pltpu.CompilerParams(
            dimension_semantics=("arbitrary",),
            vmem_limit_bytes=_vmem_limit(2 * _nbytes(wg.shape, BF16), 2 * _nbytes(wp.shape, BF16),
                                         8 * _nbytes((tm, D_MODEL), F32))),
    )(h, p, g.reshape(1, -1), wg, wp, final_g.reshape(1, -1))


def _tiles(seq):
    tm = min(512, seq)
    return dict(tm=tm, tq=min(512, seq), tkv=min(512, seq), r=min(512, seq), tn=1408)


def kernel(x, p, positions, attn_norm_g, w_in, mla_q_norm_g, mla_w_uq, mla_kv_norm_g, mla_w_ukv, fox_b_f,
           hgrn_lb_param, s5_lam_re, s5_lam_im, s5_log_step, s5_b_re, s5_b_im, s5_c_re, s5_c_im, s5_d,
           s5_w_glu, s5_b_glu, group_norm_g, w_out, ffn_norm_g, w_up, conv_w, conv_b, w_down, ple_norm_g,
           w_ple_gate, w_ple, final_norm_g):
    batch, seq, _ = x.shape
    depth = w_in.shape[0]
    t = batch * seq
    ts = _tiles(seq)
    assert seq % ts["tm"] == 0 and seq % ts["tq"] == 0 and ts["tq"] % ts["tkv"] == 0 and seq % ts["r"] == 0
    assert (seq // SUB) & (seq // SUB - 1) == 0, "chunk scan assumes a power-of-two chunk count"

    lb_all = jnp.cumsum(jax.nn.softmax(hgrn_lb_param.astype(F32), axis=0), axis=0)
    lb_all = lb_all - lb_all[0:1]
    tables = _rope_tables(positions, ts["tm"])
    h = x.reshape(t, D_MODEL)
    for i in range(depth):
        mq, mk, mv, fq, fk, fv, hg, su = _inproj(
            h, seq, attn_norm_g[i], w_in[i], mla_q_norm_g[i], mla_w_uq[i], mla_kv_norm_g[i], mla_w_ukv[i],
            fox_b_f[i], tables, ts["tm"])
        y_a = _attention(mq, mk, mv, batch, seq, ts["tq"], ts["tkv"])
        y_b = _attention(fq, fk, fv, batch, seq, ts["tq"], ts["tkv"])
        y_c = _hgrn(hg, lb_all[i], batch, seq, ts["r"])
        ops = _s5_operators(s5_lam_re[i], s5_lam_im[i], s5_log_step[i], s5_b_re[i], s5_b_im[i],
                            s5_c_re[i], s5_c_im[i])
        y_d = _s5(su, ops, s5_d[i], s5_w_glu[i], s5_b_glu[i], batch, seq, ts["r"])
        h = _outproj(y_a, y_b, y_c, hg, y_d, h, group_norm_g[i], w_out[i], ts["tm"])
        h = _ffn(h, seq, ffn_norm_g[i], w_up[i], conv_w[i], conv_b[i], w_down[i], ts["tm"], ts["tn"])
        h = _ple(h, p[i].reshape(t, PLE_DIM), ple_norm_g[i], w_ple_gate[i], w_ple[i], final_norm_g,
                 i == depth - 1, ts["tm"])
    return h.reshape(batch, seq, D_MODEL)
```

```python
import functools
import math

import numpy as np
import jax
import jax.numpy as jnp
from jax import lax
from jax.experimental import pallas as pl
from jax.experimental.pallas import tpu as pltpu

F32 = jnp.float32
BF16 = jnp.bfloat16

D_MODEL = 1024
N_HEADS = 4
HEAD_DIM = 64
GROUP_W = 256
MLA_Q_RANK = 256
MLA_KV_RANK = 128
MLA_NOPE = 64
MLA_ROPE = 32
ROPE_THETA = 10000.0
S5_GROUPS = 16
S5_CH = 16
S5_P = 64
D_FF = 2816
PLE_DIM = 256
EPS = 1e-6
N_IN = 2468

V7X_LANES = 128
V7X_SUBLANES = 8
V7X_VMEM_BYTES = 64 * 1024 * 1024
MXU_TILE = 256
VMEM_CAP_BYTES = 58 * 1024 * 1024

HEAD_PAD = V7X_LANES
SUB = 16
NEG_BIG = -1e30

SEG_CQ = 0
SEG_CKV = 256
SEG_KR = 384
SEG_FQ = 512
SEG_FK = 1024
SEG_FV = 1536
SEG_FF = 2048
SEG_HG = 2176
SEG_S5 = 3200
N_PERM = 3456
ROPE_LANE0 = 64
BIAS_LANE0 = 64
LOG2E = math.log2(math.e)


def _vmem_limit(*byte_counts):
    need = int(sum(byte_counts))
    return int(min(VMEM_CAP_BYTES, need + need // 4 + (4 << 20)))


def _nbytes(shape, dtype):
    return int(np.prod(shape)) * jnp.dtype(dtype).itemsize


def _rms(x, g):
    return x * lax.rsqrt(jnp.mean(x * x, axis=-1, keepdims=True) + EPS) * g


def _log_sigmoid(z):
    return jnp.minimum(z, 0.0) - jnp.log1p(jnp.exp(-jnp.abs(z)))


def _sigmoid(z):
    return 1.0 / (1.0 + jnp.exp(-z))


def _iota(shape, dim):
    return lax.broadcasted_iota(jnp.int32, shape, dim)


def _shift_rows(x, k):
    if k == 0:
        return x
    return pltpu.roll(x, k, 0)


def _rope_kernel(pos_ref, freq_ref, ct_ref, s1_ref, s2_ref):
    ang = pos_ref[...].astype(F32) * freq_ref[...]
    lane = _iota(ang.shape, 1)
    half = MLA_ROPE // 2
    sin = jnp.sin(ang)
    ct_ref[...] = jnp.cos(ang)
    s1_ref[...] = jnp.where((lane >= ROPE_LANE0) & (lane < ROPE_LANE0 + half), -sin, 0.0)
    s2_ref[...] = jnp.where((lane >= ROPE_LANE0 + half) & (lane < ROPE_LANE0 + 2 * half), sin, 0.0)


def _rope_tables(positions, tm):
    t = positions.size
    half = MLA_ROPE // 2
    inv_freq = ROPE_THETA ** (-jnp.arange(half, dtype=F32) / half)
    freq = jnp.zeros((1, HEAD_PAD), F32).at[0, ROPE_LANE0:ROPE_LANE0 + 2 * half].set(jnp.tile(inv_freq, 2))
    pos = positions.reshape(t, 1)
    out = jax.ShapeDtypeStruct((t, HEAD_PAD), F32)
    spec = pl.BlockSpec((tm, HEAD_PAD), lambda i: (i, 0))
    return pl.pallas_call(
        _rope_kernel,
        out_shape=(out, out, out),
        grid=(t // tm,),
        in_specs=[pl.BlockSpec((tm, 1), lambda i: (i, 0)), pl.BlockSpec((1, HEAD_PAD), lambda i: (0, 0))],
        out_specs=(spec, spec, spec),
        name="rope_tables",
        compiler_params=pltpu.CompilerParams(dimension_semantics=("arbitrary",)),
    )(pos, freq)


def _inproj_kernel(tiles_per_seq, h_ref, g_ref, w_ref, qg_ref, wuq_ref, kvg_ref, wukv_ref, bf_ref,
                   ct_ref, s1_ref, s2_ref, selq_ref, selk_ref,
                   mq_ref, mk_ref, mv_ref, fq_ref, fk_ref, fv_ref, hg_ref, su_ref, carry_ref):
    i = pl.program_id(0)

    @pl.when(i % tiles_per_seq == 0)
    def _():
        carry_ref[...] = jnp.zeros_like(carry_ref)

    tm = h_ref.shape[0]
    xn = _rms(h_ref[...], g_ref[...]).astype(BF16)

    def seg(a, b):
        return jnp.dot(xn, w_ref[:, a:b], preferred_element_type=F32)

    q = jnp.dot(_rms(seg(SEG_CQ, SEG_CKV), qg_ref[...]).astype(BF16), wuq_ref[...],
                preferred_element_type=F32)
    kv = jnp.dot(_rms(seg(SEG_CKV, SEG_KR), kvg_ref[...]).astype(BF16), wukv_ref[...],
                 preferred_element_type=F32)
    ct, s1, s2 = ct_ref[...], s1_ref[...], s2_ref[...]
    half = MLA_ROPE // 2

    def rope(t):
        return t * ct + pltpu.roll(t, HEAD_PAD - half, 1) * s1 + pltpu.roll(t, half, 1) * s2

    k_pe = rope(seg(SEG_KR, SEG_FQ))
    mla_scale = (MLA_NOPE + MLA_ROPE) ** -0.5 * LOG2E
    for hd in range(N_HEADS):
        sl = slice(hd * HEAD_PAD, (hd + 1) * HEAD_PAD)
        mq_ref[:, sl] = (rope(q[:, sl]) * mla_scale).astype(BF16)
        mk_ref[:, sl] = (kv[:, sl] + k_pe).astype(BF16)
    hp4 = N_HEADS * HEAD_PAD
    ones_pad = jnp.where((_iota((1, hp4), 1) & (HEAD_PAD - 1)) >= HEAD_DIM, 1.0, 0.0)
    mv_ref[...] = (kv[:, hp4:] + ones_pad).astype(BF16)

    lf = _log_sigmoid(seg(SEG_FF, SEG_HG) + bf_ref[...])
    tril = (_iota((tm, tm), 0) >= _iota((tm, tm), 1)).astype(BF16)

    def split3(x):
        a = x.astype(BF16)
        r = x - a.astype(F32)
        b = r.astype(BF16)
        c = (r - b.astype(F32)).astype(BF16)
        return a, b, c

    l1, l2, l3 = split3(lf)
    cum = (jnp.dot(tril, l1, preferred_element_type=F32) + jnp.dot(tril, l2, preferred_element_type=F32)
           + jnp.dot(tril, l3, preferred_element_type=F32)) + carry_ref[...]
    carry_ref[...] = cum[tm - 1:tm, :]
    lane = _iota(cum.shape, 1)
    c1, c2, c3 = split3(cum * LOG2E)
    keep = lane < N_HEADS
    parts = (jnp.where(keep, c1.astype(F32), 0.0)
             + pltpu.roll(jnp.where(keep, c2.astype(F32), 0.0), N_HEADS, 1)
             + pltpu.roll(jnp.where(keep, c3.astype(F32), 0.0), 2 * N_HEADS, 1)
             + jnp.where(lane == 3 * N_HEADS, 1.0, 0.0)).astype(BF16)
    fox_scale = HEAD_DIM ** -0.5 * LOG2E
    fq_ref[...] = (seg(SEG_FQ, SEG_FK) * fox_scale
                   + jnp.dot(parts, selq_ref[...], preferred_element_type=F32)).astype(BF16)
    fk_ref[...] = (seg(SEG_FK, SEG_FV)
                   + jnp.dot(parts, selk_ref[...], preferred_element_type=F32)).astype(BF16)
    fv_ref[...] = (seg(SEG_FV, SEG_FF) + ones_pad).astype(BF16)

    hg_ref[...] = seg(SEG_HG, SEG_S5)
    su_ref[...] = seg(SEG_S5, N_PERM)


def _inproj_index():
    zero = N_IN
    idx = list(range(0, 384))
    kr = [zero] * HEAD_PAD
    kr[ROPE_LANE0:ROPE_LANE0 + MLA_ROPE] = range(384, 416)
    idx += kr
    for base in (416, 672, 928):
        for hd in range(N_HEADS):
            blk = [zero] * HEAD_PAD
            blk[0:HEAD_DIM] = range(base + HEAD_DIM * hd, base + HEAD_DIM * (hd + 1))
            idx += blk
    ff = [zero] * HEAD_PAD
    ff[0:N_HEADS] = range(1184, 1188)
    idx += ff
    idx += range(1188, N_IN)
    assert len(idx) == N_PERM
    return np.asarray(idx, np.int32)


def _bias_selectors():
    selq = np.zeros((HEAD_PAD, N_HEADS * HEAD_PAD), np.float32)
    selk = np.zeros((HEAD_PAD, N_HEADS * HEAD_PAD), np.float32)
    one = 3 * N_HEADS
    for hd in range(N_HEADS):
        for j in range(3):
            selq[N_HEADS * j + hd, hd * HEAD_PAD + BIAS_LANE0 + j] = 1.0
            selq[one, hd * HEAD_PAD + BIAS_LANE0 + 3 + j] = 1.0
            selk[one, hd * HEAD_PAD + BIAS_LANE0 + j] = 1.0
            selk[N_HEADS * j + hd, hd * HEAD_PAD + BIAS_LANE0 + 3 + j] = -1.0
    return jnp.asarray(selq, BF16), jnp.asarray(selk, BF16)


def _inproj(h, seq, attn_g, w_in, q_g, w_uq, kv_g, w_ukv, b_f, tables, tm):
    t = h.shape[0]
    w_perm = jnp.take(jnp.concatenate([w_in, jnp.zeros((D_MODEL, 1), w_in.dtype)], axis=1),
                      _inproj_index(), axis=1).astype(BF16)
    wuq = jnp.pad(w_uq.reshape(MLA_Q_RANK, N_HEADS, MLA_NOPE + MLA_ROPE),
                  ((0, 0), (0, 0), (0, HEAD_PAD - MLA_NOPE - MLA_ROPE))).reshape(MLA_Q_RANK, -1).astype(BF16)
    wkv = w_ukv.reshape(MLA_KV_RANK, N_HEADS, 2 * HEAD_DIM)
    head_pad = ((0, 0), (0, 0), (0, HEAD_PAD - HEAD_DIM))
    wk = jnp.pad(wkv[:, :, :MLA_NOPE], head_pad).reshape(MLA_KV_RANK, -1)
    wv = jnp.pad(wkv[:, :, MLA_NOPE:], head_pad).reshape(MLA_KV_RANK, -1)
    wukv = jnp.concatenate([wk, wv], axis=1).astype(BF16)
    bf = jnp.zeros((1, HEAD_PAD), F32).at[0, :N_HEADS].set(b_f.astype(F32))
    selq, selk = _bias_selectors()
    ct, s1, s2 = tables
    hp4 = N_HEADS * HEAD_PAD

    def rows(width):
        return pl.BlockSpec((tm, width), lambda i: (i, 0))

    def whole(a):
        return pl.BlockSpec(a.shape, lambda i: (0,) * a.ndim)

    args = (h, attn_g.reshape(1, -1), w_perm, q_g.reshape(1, -1), wuq, kv_g.reshape(1, -1), wukv, bf,
            ct, s1, s2, selq, selk)
    in_specs = [rows(D_MODEL)] + [whole(a) for a in args[1:8]] + [rows(HEAD_PAD)] * 3 + [whole(selq), whole(selk)]
    out_widths = (hp4,) * 6
    out_shape = tuple(jax.ShapeDtypeStruct((t, w), BF16) for w in out_widths) + (
        jax.ShapeDtypeStruct((t, 4 * GROUP_W), F32), jax.ShapeDtypeStruct((t, GROUP_W), F32))
    out_specs = tuple(rows(w) for w in out_widths) + (rows(4 * GROUP_W), rows(GROUP_W))
    vmem = _vmem_limit(2 * _nbytes(w_perm.shape, BF16), 2 * _nbytes((tm, D_MODEL), F32),
                       2 * sum(_nbytes((tm, w), BF16) for w in out_widths),
                       2 * _nbytes((tm, 5 * GROUP_W), F32), _nbytes((tm, N_PERM), F32),
                       _nbytes((tm, D_MODEL), F32))
    return pl.pallas_call(
        functools.partial(_inproj_kernel, seq // tm),
        out_shape=out_shape,
        grid=(t // tm,),
        in_specs=in_specs,
        out_specs=out_specs,
        scratch_shapes=[pltpu.VMEM((1, HEAD_PAD), F32)],
        name="inproj",
        compiler_params=pltpu.CompilerParams(dimension_semantics=("arbitrary",), vmem_limit_bytes=vmem),
    )(*args)


def _attn_kernel(tq, tkv, q_ref, k_ref, v_ref, o_ref, m_ref, acc_ref):
    qi = pl.program_id(1)
    m_ref[...] = jnp.full_like(m_ref, NEG_BIG)
    acc_ref[...] = jnp.zeros_like(acc_ref)
    per_q = tq // tkv

    def block(start, width, diag_offset):
        rows = pl.ds(pl.multiple_of(start, tkv), width)
        for hd in range(N_HEADS):
            sl = slice(hd * HEAD_PAD, (hd + 1) * HEAD_PAD)
            s = lax.dot_general(q_ref[:, sl], k_ref[rows, sl], (((1,), (1,)), ((), ())),
                                preferred_element_type=F32)
            if diag_offset is not None:
                visible = _iota(s.shape, 1) + diag_offset <= _iota(s.shape, 0)
                s = jnp.where(visible, s, NEG_BIG)
            m_old = m_ref[hd]
            m_new = jnp.maximum(m_old, jnp.max(s, axis=1, keepdims=True))
            p = jnp.exp2(s - jnp.concatenate([m_new] * (width // HEAD_PAD), axis=1))
            m_ref[hd] = m_new
            acc_ref[hd] = (jnp.exp2(m_old - m_new) * acc_ref[hd]
                           + jnp.dot(p.astype(BF16), v_ref[rows, sl], preferred_element_type=F32))

    n_full = qi * per_q

    def wide_block(jj, carry):
        block(jj * (2 * tkv), 2 * tkv, None)
        return carry

    lax.fori_loop(0, n_full // 2, wide_block, 0)

    @pl.when(n_full % 2 == 1)
    def _():
        block((n_full - 1) * tkv, tkv, None)

    for r in range(per_q):
        block((n_full + r) * tkv, tkv, r * tkv)

    low = _iota((tq, HEAD_PAD), 1) < HEAD_DIM
    for pair in range(N_HEADS // 2):
        a0, a1 = acc_ref[2 * pair], acc_ref[2 * pair + 1]
        n0 = a0 / pltpu.roll(a0, HEAD_DIM, 1)
        n1 = a1 / pltpu.roll(a1, HEAD_DIM, 1)
        o_ref[:, pair * HEAD_PAD:(pair + 1) * HEAD_PAD] = jnp.where(low, n0, pltpu.roll(n1, HEAD_DIM, 1))


def _attention(q, k, v, batch, seq, tq, tkv):
    t = q.shape[0]
    hp4 = N_HEADS * HEAD_PAD
    vmem = _vmem_limit(2 * _nbytes((seq, hp4), BF16), 2 * _nbytes((tq, hp4), BF16),
                       2 * _nbytes((tq, GROUP_W), F32), 2 * N_HEADS * _nbytes((tq, HEAD_PAD), F32),
                       6 * _nbytes((tq, 2 * tkv), F32))
    resident = pl.BlockSpec((seq, hp4), lambda b, i: (b, 0), pipeline_mode=pl.Buffered(1))
    return pl.pallas_call(
        functools.partial(_attn_kernel, tq, tkv),
        out_shape=jax.ShapeDtypeStruct((t, GROUP_W), F32),
        grid=(batch, seq // tq),
        in_specs=[pl.BlockSpec((tq, hp4), lambda b, i: (b * (seq // tq) + i, 0)), resident, resident],
        out_specs=pl.BlockSpec((tq, GROUP_W), lambda b, i: (b * (seq // tq) + i, 0)),
        scratch_shapes=[pltpu.VMEM((N_HEADS, tq, HEAD_PAD), F32), pltpu.VMEM((N_HEADS, tq, HEAD_PAD), F32)],
        name="causal_attention",
        compiler_params=pltpu.CompilerParams(dimension_semantics=("arbitrary", "arbitrary"),
                                             vmem_limit_bytes=vmem),
    )(q, k, v)


def _hgrn_kernel(win, q0_ref, q1_ref, f0_ref, f1_ref, v0_ref, v1_ref, lb_ref, ee_ref, o_ref,
                 st_ref, bc_ref, kk_ref, vs_ref, qe_ref, ke_ref, od_ref, dec_ref, vt_ref):
    @pl.when(pl.program_id(1) == 0)
    def _():
        st_ref[...] = jnp.zeros_like(st_ref)

    q_ref, f_ref, v_ref = (q0_ref, q1_ref), (f0_ref, f1_ref), (v0_ref, v1_ref)
    r = q0_ref.shape[0]
    g = r // SUB
    lb = lb_ref[...]
    log_lb = jnp.log(lb)
    log_1m = jnp.log1p(-lb)

    def slab(halves, s):
        return jnp.concatenate([h[pl.ds(s, g, stride=SUB), :] for h in halves], axis=1)

    def put_rows(ref, s, x):
        for half in range(2):
            ref[half, pl.ds(s, g, stride=SUB), :] = x[:, half * V7X_LANES:(half + 1) * V7X_LANES]

    def get_rows(ref, row0):
        return jnp.concatenate([ref[0, pl.ds(row0, SUB), :], ref[1, pl.ds(row0, SUB), :]], axis=1)

    run = None
    for s in range(SUB):
        z = slab(f_ref, s)
        b = log_1m + _log_sigmoid(z)
        log_f = jnp.maximum(log_lb, b) + jnp.log1p(jnp.exp(-jnp.abs(log_lb - b)))
        run = log_f if s == 0 else run + log_f
        bc_ref[s] = run
        kk_ref[s] = (1.0 - lb) * _sigmoid(-z)
        vs_ref[s] = slab(v_ref, s)
    total = bc_ref[SUB - 1]
    dec_ref[...] = jnp.exp(total)

    for s in range(SUB):
        qs = slab(q_ref, s)
        bcs = bc_ref[s]
        prods = [(qs * kk_ref[j] * jnp.exp(bcs - bc_ref[j])).astype(BF16) for j in range(s)]
        prods.append((qs * kk_ref[s]).astype(BF16))
        red = jnp.dot(jnp.concatenate(prods, axis=0), ee_ref[...], preferred_element_type=F32)
        od = red[0:g] * vs_ref[0]
        for j in range(1, s + 1):
            od = od + red[j * g:(j + 1) * g] * vs_ref[j]
        put_rows(od_ref, s, od)
        put_rows(qe_ref, s, qs * jnp.exp(bcs))
        put_rows(ke_ref, s, kk_ref[s] * jnp.exp(total - bcs))
    for w in range(r // win):
        for half in range(2):
            vt_ref[w, half * V7X_LANES:(half + 1) * V7X_LANES, :] = (
                v_ref[half][w * win:(w + 1) * win, :].T.astype(BF16))

    lane_head = _iota((GROUP_W, GROUP_W), 1) // HEAD_DIM
    row_head = _iota((GROUP_W, GROUP_W), 0) // HEAD_DIM
    same_head = lane_head == row_head
    per_win = win // SUB

    def window(w, carry):
        vt = vt_ref[w]
        st = st_ref[...]
        for c in range(per_win):
            row0 = pl.multiple_of(w * win + c * SUB, SUB)
            qe = get_rows(qe_ref, row0).astype(BF16)
            o_state = lax.dot_general(qe, st.astype(BF16), (((1,), (1,)), ((), ())),
                                      preferred_element_type=F32)
            o_ref[pl.ds(row0, SUB), :] = get_rows(od_ref, row0) + o_state
            pieces = [get_rows(ke_ref, row0).astype(BF16)]
            if c:
                pieces.insert(0, jnp.zeros((c * SUB, GROUP_W), BF16))
            if c < per_win - 1:
                pieces.append(jnp.zeros(((per_win - 1 - c) * SUB, GROUP_W), BF16))
            upd = jnp.dot(vt, jnp.concatenate(pieces, axis=0), preferred_element_type=F32)
            st = st * dec_ref[pl.ds(w * per_win + c, 1), :] + jnp.where(same_head, upd, 0.0)
        st_ref[...] = st
        return carry

    lax.fori_loop(0, r // win, window, 0)


def _hgrn(hg, lb, batch, seq, r):
    t = hg.shape[0]
    win = min(r, V7X_LANES)
    ee = jnp.asarray(np.kron(np.eye(N_HEADS, dtype=np.float32), np.ones((HEAD_DIM, HEAD_DIM), np.float32)), BF16)
    nblk = seq // r

    def half(c):
        return pl.BlockSpec((r, V7X_LANES), lambda b, i: (b * nblk + i, c))

    tile = _nbytes((r, GROUP_W), F32)
    vmem = _vmem_limit(8 * tile, 7 * tile, 8 * tile)
    slabs = pltpu.VMEM((SUB, r // SUB, GROUP_W), F32)
    rows = pltpu.VMEM((2, r, V7X_LANES), F32)
    return pl.pallas_call(
        functools.partial(_hgrn_kernel, win),
        out_shape=jax.ShapeDtypeStruct((t, GROUP_W), F32),
        grid=(batch, nblk),
        in_specs=[half(0), half(1), half(2), half(3), half(4), half(5),
                  pl.BlockSpec((1, GROUP_W), lambda b, i: (0, 0)),
                  pl.BlockSpec((GROUP_W, GROUP_W), lambda b, i: (0, 0))],
        out_specs=pl.BlockSpec((r, GROUP_W), lambda b, i: (b * nblk + i, 0)),
        scratch_shapes=[pltpu.VMEM((GROUP_W, GROUP_W), F32), slabs, slabs, slabs, rows, rows, rows,
                        pltpu.VMEM((r // SUB, GROUP_W), F32), pltpu.VMEM((r // win, GROUP_W, win), BF16)],
        name="hgrn2",
        compiler_params=pltpu.CompilerParams(dimension_semantics=("arbitrary", "arbitrary"),
                                             vmem_limit_bytes=vmem),
    )(hg, hg, hg, hg, hg, hg, lb.reshape(1, -1), ee)


def _s5_local_kernel(u_ref, bd_ref, bbd_ref, we_ref, d_ref, y_ref, e_ref):
    r = u_ref.shape[0]
    u = u_ref[...]
    rsub = _iota((r, GROUP_W), 0) & (SUB - 1)
    y = d_ref[...] * u
    for j in range(SUB):
        uj = jnp.where(rsub >= j, _shift_rows(u, j), 0.0).astype(BF16)
        y = y + jnp.dot(uj, bd_ref[j], preferred_element_type=F32)
    y_ref[...] = y
    n = S5_GROUPS * S5_P
    bu = jnp.dot(u.astype(BF16), bbd_ref[...], preferred_element_type=F32).reshape(r // SUB, SUB, 2 * n)
    bur, bui = bu[:, :, :n], bu[:, :, n:]
    wr, wi = we_ref[:, :n], we_ref[:, n:]
    e_ref[:, :n] = jnp.sum(wr * bur - wi * bui, axis=1)
    e_ref[:, n:] = jnp.sum(wr * bui + wi * bur, axis=1)


def _s5_scan_kernel(e_ref, a_ref, x_ref):
    nchunk = e_ref.shape[0]
    n = S5_GROUPS * S5_P
    xr, xi = e_ref[:, :n], e_ref[:, n:]
    pr, pi = a_ref[:, :n], a_ref[:, n:]
    row = _iota((nchunk, n), 0)
    k = 1
    while k < nchunk:
        sr = jnp.where(row >= k, pltpu.roll(xr, k, 0), 0.0)
        si = jnp.where(row >= k, pltpu.roll(xi, k, 0), 0.0)
        xr, xi = xr + pr * sr - pi * si, xi + pr * si + pi * sr
        pr, pi = pr * pr - pi * pi, 2.0 * pr * pi
        k *= 2
    x_ref[:, :n] = jnp.where(row >= 1, pltpu.roll(xr, 1, 0), 0.0)
    x_ref[:, n:] = jnp.where(row >= 1, pltpu.roll(xi, 1, 0), 0.0)


def _s5_out_kernel(y1_ref, x_ref, a1_ref, cbd_ref, wg_ref, bg_ref, o_ref, xs_ref):
    r = y1_ref.shape[0]
    n = S5_GROUPS * S5_P
    ar, ai = a1_ref[:, :n], a1_ref[:, n:]

    def chunk(c, carry):
        base = pl.multiple_of(c * SUB, SUB)
        xr = x_ref[pl.ds(c, 1), :n]
        xi = x_ref[pl.ds(c, 1), n:]
        xs_ref[pl.ds(base, SUB), :n] = (ar * xr - ai * xi).astype(BF16)
        xs_ref[pl.ds(base, SUB), n:] = (ar * xi + ai * xr).astype(BF16)
        return carry

    lax.fori_loop(0, r // SUB, chunk, 0)
    y = y1_ref[...] + jnp.dot(xs_ref[...], cbd_ref[...], preferred_element_type=F32)
    zact = 0.5 * y * (1.0 + jnp.tanh(math.sqrt(2.0 / math.pi) * (y + 0.044715 * (y * y * y))))
    gate = jnp.dot(zact.astype(BF16), wg_ref[...], preferred_element_type=F32) + bg_ref[...]
    o_ref[...] = zact * _sigmoid(gate)


def _s5_operators(lam_re, lam_im, log_step, b_re, b_im, c_re, c_im):
    hi = lax.Precision.HIGHEST
    step = jnp.exp(log_step.astype(F32))[:, None]
    lre = jnp.minimum(lam_re.astype(F32), -1e-4)
    lim = lam_im.astype(F32)
    mag = jnp.exp(lre * step)
    a_re, a_im = mag * jnp.cos(lim * step), mag * jnp.sin(lim * step)
    den = lre * lre + lim * lim
    coef_re = ((a_re - 1.0) * lre + a_im * lim) / den
    coef_im = (a_im * lre - (a_re - 1.0) * lim) / den
    br, bi = b_re.astype(F32), b_im.astype(F32)
    bb_re = coef_re[..., None] * br - coef_im[..., None] * bi
    bb_im = coef_re[..., None] * bi + coef_im[..., None] * br
    cr, ci = c_re.astype(F32), c_im.astype(F32)
    pows_r, pows_i = [jnp.ones_like(a_re)], [jnp.zeros_like(a_re)]
    for _ in range(SUB):
        pr, pi = pows_r[-1], pows_i[-1]
        pows_r.append(pr * a_re - pi * a_im)
        pows_i.append(pr * a_im + pi * a_re)
    pw_r, pw_i = jnp.stack(pows_r), jnp.stack(pows_i)
    eye = jnp.eye(S5_GROUPS, dtype=F32)
    cp_r = cr[None] * pw_r[:SUB, :, None, :] - ci[None] * pw_i[:SUB, :, None, :]
    cp_i = cr[None] * pw_i[:SUB, :, None, :] + ci[None] * pw_r[:SUB, :, None, :]
    k_lag = (jnp.einsum("jghp,gpk->jghk", cp_r, bb_re, precision=hi)
             - jnp.einsum("jghp,gpk->jghk", cp_i, bb_im, precision=hi))
    bd = jnp.einsum("jghk,gf->jgkfh", k_lag, eye).reshape(SUB, GROUP_W, GROUP_W)
    n = S5_GROUPS * S5_P
    bbd = jnp.concatenate([jnp.einsum("gpk,gf->gkfp", bb_re, eye).reshape(GROUP_W, n),
                           jnp.einsum("gpk,gf->gkfp", bb_im, eye).reshape(GROUP_W, n)], axis=1)
    cbd = jnp.concatenate([jnp.einsum("ghp,gf->gpfh", cr, eye).reshape(n, GROUP_W),
                           -jnp.einsum("ghp,gf->gpfh", ci, eye).reshape(n, GROUP_W)], axis=0)
    w_end = jnp.concatenate([pw_r[SUB - 1::-1].reshape(SUB, n)[:SUB], pw_i[SUB - 1::-1].reshape(SUB, n)[:SUB]], axis=1)
    a_in = jnp.concatenate([pw_r[1:].reshape(SUB, n), pw_i[1:].reshape(SUB, n)], axis=1)
    a_chunk = jnp.concatenate([pw_r[SUB].reshape(1, n), pw_i[SUB].reshape(1, n)], axis=1)
    return bd.astype(BF16), bbd.astype(BF16), cbd.astype(BF16), w_end, a_in, a_chunk


def _s5(u, ops, d_skip, w_glu, b_glu, batch, seq, r):
    t = u.shape[0]
    bd, bbd, cbd, w_end, a_in, a_chunk = ops
    n2 = 2 * S5_GROUPS * S5_P
    nck = seq // SUB

    def whole(a):
        return pl.BlockSpec(a.shape, lambda *_: (0,) * a.ndim)

    rows = pl.BlockSpec((r, GROUP_W), lambda i: (i, 0))
    crow = pl.BlockSpec((r // SUB, n2), lambda i: (i, 0))
    d2 = d_skip.astype(F32).reshape(1, -1)
    y1, e = pl.pallas_call(
        _s5_local_kernel,
        out_shape=(jax.ShapeDtypeStruct((t, GROUP_W), F32), jax.ShapeDtypeStruct((t // SUB, n2), F32)),
        grid=(t // r,),
        in_specs=[rows, whole(bd), whole(bbd), whole(w_end), whole(d2)],
        out_specs=(rows, crow),
        name="s5_local",
        compiler_params=pltpu.CompilerParams(
            dimension_semantics=("arbitrary",),
            vmem_limit_bytes=_vmem_limit(2 * _nbytes(bd.shape, BF16), 2 * _nbytes(bbd.shape, BF16),
                                         6 * _nbytes((r, n2), F32), 8 * _nbytes((r, GROUP_W), F32))),
    )(u, bd, bbd, w_end, d2)
    xin = pl.pallas_call(
        _s5_scan_kernel,
        out_shape=jax.ShapeDtypeStruct((t // SUB, n2), F32),
        grid=(batch,),
        in_specs=[pl.BlockSpec((nck, n2), lambda b: (b, 0)), whole(a_chunk)],
        out_specs=pl.BlockSpec((nck, n2), lambda b: (b, 0)),
        name="s5_scan",
        compiler_params=pltpu.CompilerParams(
            dimension_semantics=("arbitrary",), vmem_limit_bytes=_vmem_limit(10 * _nbytes((nck, n2), F32))),
    )(e, a_chunk)
    wg = w_glu.astype(BF16)
    bg = b_glu.astype(F32).reshape(1, -1)
    return pl.pallas_call(
        _s5_out_kernel,
        out_shape=jax.ShapeDtypeStruct((t, GROUP_W), F32),
        grid=(t // r,),
        in_specs=[rows, crow, whole(a_in), whole(cbd), whole(wg), whole(bg)],
        out_specs=rows,
        scratch_shapes=[pltpu.VMEM((r, n2), BF16)],
        name="s5_out",
        compiler_params=pltpu.CompilerParams(
            dimension_semantics=("arbitrary",),
            vmem_limit_bytes=_vmem_limit(2 * _nbytes(cbd.shape, BF16), 3 * _nbytes((r, n2), BF16),
                                         10 * _nbytes((r, GROUP_W), F32))),
    )(y1, xin, a_in, cbd, wg, bg)


def _outproj_kernel(ya_ref, yb_ref, yc_ref, gate_ref, yd_ref, h_ref, gn_ref, w_ref, o_ref):
    gw = GROUP_W
    parts = (_rms(ya_ref[...], gn_ref[:, 0:gw]),
             _rms(yb_ref[...], gn_ref[:, gw:2 * gw]),
             _rms(yc_ref[...], gn_ref[:, 2 * gw:3 * gw]) * _sigmoid(gate_ref[...]),
             _rms(yd_ref[...], gn_ref[:, 3 * gw:4 * gw]))
    acc = h_ref[...]
    for g, part in enumerate(parts):
        acc = acc + jnp.dot(part.astype(BF16), w_ref[g * gw:(g + 1) * gw, :], preferred_element_type=F32)
    o_ref[...] = acc


def _outproj(ya, yb, yc, hg, yd, h, gn, w_out, tm):
    t = h.shape[0]
    rows = pl.BlockSpec((tm, GROUP_W), lambda i: (i, 0))
    wide = pl.BlockSpec((tm, D_MODEL), lambda i: (i, 0))
    w = w_out.astype(BF16)
    return pl.pallas_call(
        _outproj_kernel,
        out_shape=jax.ShapeDtypeStruct((t, D_MODEL), F32),
        grid=(t // tm,),
        in_specs=[rows, rows, rows, pl.BlockSpec((tm, GROUP_W), lambda i: (i, 3)), rows, wide,
                  pl.BlockSpec((1, D_MODEL), lambda i: (0, 0)), pl.BlockSpec((D_MODEL, D_MODEL), lambda i: (0, 0))],
        out_specs=wide,
        name="outproj",
        compiler_params=pltpu.CompilerParams(
            dimension_semantics=("arbitrary",),
            vmem_limit_bytes=_vmem_limit(2 * _nbytes(w.shape, BF16), 14 * _nbytes((tm, GROUP_W), F32),
                                         6 * _nbytes((tm, D_MODEL), F32))),
    )(ya, yb, yc, hg, yd, h, gn.reshape(1, -1), w)


HALO = 2 * V7X_SUBLANES


def _ffn_kernel(tiles_per_seq, h_ref, halo_ref, g_ref, wu_ref, cw_ref, cb_ref, wd_ref, o_ref,
                xn_ref, up_ref, act_ref):
    tm = h_ref.shape[0]
    nchunk = wu_ref.shape[0] // 2
    width = wu_ref.shape[2]
    first = pl.program_id(0) % tiles_per_seq == 0
    xn_ref[0:HALO, :] = jnp.where(first, 0.0, _rms(halo_ref[...], g_ref[...])).astype(BF16)
    xn_ref[HALO:, :] = _rms(h_ref[...], g_ref[...]).astype(BF16)

    def conv(slot, c):
        up_ref[slot] = jnp.dot(xn_ref[...], wu_ref[c], preferred_element_type=F32)
        cw = cw_ref[c]
        return (cb_ref[c] + cw[0:1, :] * up_ref[slot, pl.ds(HALO - 2, tm), :]
                + cw[1:2, :] * up_ref[slot, pl.ds(HALO - 1, tm), :]
                + cw[2:3, :] * up_ref[slot, pl.ds(HALO, tm), :])

    for c in range(nchunk):
        slot = 2 * (c % 2)
        gate = conv(slot, c)
        val = conv(slot + 1, c + nchunk)
        act_ref[:, c * width:(c + 1) * width] = (gate * _sigmoid(gate) * val).astype(BF16)
    o_ref[...] = h_ref[...] + jnp.dot(act_ref[...], wd_ref[...], preferred_element_type=F32)


def _ffn(h, seq, g, w_up, conv_w, conv_b, w_down, tm):
    t = h.shape[0]
    nck = 2 * D_FF // MXU_TILE
    wu = w_up.astype(BF16).reshape(D_MODEL, nck, MXU_TILE).transpose(1, 0, 2)
    cw = conv_w.reshape(conv_w.shape[0], nck, MXU_TILE).transpose(1, 0, 2)
    cb = conv_b.reshape(nck, 1, MXU_TILE)
    wd = w_down.astype(BF16)
    per = tm // HALO
    vmem = _vmem_limit(_nbytes(wu.shape, BF16), _nbytes(wd.shape, BF16), 4 * _nbytes((tm, D_MODEL), F32),
                       _nbytes((tm + HALO, D_MODEL), BF16), 4 * _nbytes((tm + HALO, MXU_TILE), F32),
                       _nbytes((tm, D_FF), BF16), 6 * _nbytes((tm, MXU_TILE), F32))

    def whole(a):
        return pl.BlockSpec(a.shape, lambda i: (0,) * a.ndim, pipeline_mode=pl.Buffered(1))

    return pl.pallas_call(
        functools.partial(_ffn_kernel, seq // tm),
        out_shape=jax.ShapeDtypeStruct((t, D_MODEL), F32),
        grid=(t // tm,),
        in_specs=[pl.BlockSpec((tm, D_MODEL), lambda i: (i, 0)),
                  pl.BlockSpec((HALO, D_MODEL), lambda i: (jnp.maximum(i * per - 1, 0), 0)),
                  pl.BlockSpec((1, D_MODEL), lambda i: (0, 0)),
                  whole(wu), whole(cw), whole(cb), whole(wd)],
        out_specs=pl.BlockSpec((tm, D_MODEL), lambda i: (i, 0)),
        scratch_shapes=[pltpu.VMEM((tm + HALO, D_MODEL), BF16), pltpu.VMEM((4, tm + HALO, MXU_TILE), F32),
                        pltpu.VMEM((tm, D_FF), BF16)],
        name="conv_ffn",
        compiler_params=pltpu.CompilerParams(dimension_semantics=("arbitrary",), vmem_limit_bytes=vmem),
    )(h, h, g.reshape(1, -1), wu, cw, cb, wd)


def _ple_kernel(final, h_ref, p_ref, g_ref, wg_ref, wp_ref, fg_ref, o_ref):
    h = h_ref[...]
    gate = _sigmoid(jnp.dot(_rms(h, g_ref[...]).astype(BF16), wg_ref[...], preferred_element_type=F32))
    out = h + gate * jnp.dot(p_ref[...].astype(BF16), wp_ref[...], preferred_element_type=F32)
    o_ref[...] = _rms(out, fg_ref[...]) if final else out


def _ple(h, p, g, w_gate, w_ple, final_g, final, tm):
    t = h.shape[0]
    wide = pl.BlockSpec((tm, D_MODEL), lambda i: (i, 0))
    row = pl.BlockSpec((1, D_MODEL), lambda i: (0, 0))
    wg, wp = w_gate.astype(BF16), w_ple.astype(BF16)
    return pl.pallas_call(
        functools.partial(_ple_kernel, final),
        out_shape=jax.ShapeDtypeStruct((t, D_MODEL), F32),
        grid=(t // tm,),
        in_specs=[wide, pl.BlockSpec((tm, PLE_DIM), lambda i: (i, 0)), row,
                  pl.BlockSpec((D_MODEL, D_MODEL), lambda i: (0, 0)),
                  pl.BlockSpec((PLE_DIM, D_MODEL), lambda i: (0, 0)), row],
        out_specs=wide,
        name="ple",
        compiler_params=pltpu.CompilerParams(
            dimension_semantics=("arbitrary",),
            vmem_limit_bytes=_vmem_limit(2 * _nbytes(wg.shape, BF16), 2 * _nbytes(wp.shape, BF16),
                                         8 * _nbytes((tm, D_MODEL), F32))),
    )(h, p, g.reshape(1, -1), wg, wp, final_g.reshape(1, -1))


def _tiles(seq):
    tm = min(512, seq)
    return dict(tm=tm, tq=min(512, seq), tkv=min(512, seq), r=min(512, seq))


def kernel(x, p, positions, attn_norm_g, w_in, mla_q_norm_g, mla_w_uq, mla_kv_norm_g, mla_w_ukv, fox_b_f,
           hgrn_lb_param, s5_lam_re, s5_lam_im, s5_log_step, s5_b_re, s5_b_im, s5_c_re, s5_c_im, s5_d,
           s5_w_glu, s5_b_glu, group_norm_g, w_out, ffn_norm_g, w_up, conv_w, conv_b, w_down, ple_norm_g,
           w_ple_gate, w_ple, final_norm_g):
    batch, seq, _ = x.shape
    depth = w_in.shape[0]
    t = batch * seq
    ts = _tiles(seq)
    assert seq % ts["tm"] == 0 and seq % ts["tq"] == 0 and ts["tq"] % ts["tkv"] == 0 and seq % ts["r"] == 0
    assert (seq // SUB) & (seq // SUB - 1) == 0, "chunk scan assumes a power-of-two chunk count"

    lb_all = jnp.cumsum(jax.nn.softmax(hgrn_lb_param.astype(F32), axis=0), axis=0)
    lb_all = lb_all - lb_all[0:1]
    tables = _rope_tables(positions, ts["tm"])
    h = x.reshape(t, D_MODEL)
    for i in range(depth):
        mq, mk, mv, fq, fk, fv, hg, su = _inproj(
            h, seq, attn_norm_g[i], w_in[i], mla_q_norm_g[i], mla_w_uq[i], mla_kv_norm_g[i], mla_w_ukv[i],
            fox_b_f[i], tables, ts["tm"])
        y_a = _attention(mq, mk, mv, batch, seq, ts["tq"], ts["tkv"])
        y_b = _attention(fq, fk, fv, batch, seq, ts["tq"], ts["tkv"])
        y_c = _hgrn(hg, lb_all[i], batch, seq, ts["r"])
        ops = _s5_operators(s5_lam_re[i], s5_lam_im[i], s5_log_step[i], s5_b_re[i], s5_b_im[i],
                            s5_c_re[i], s5_c_im[i])
        y_d = _s5(su, ops, s5_d[i], s5_w_glu[i], s5_b_glu[i], batch, seq, ts["r"])
        h = _outproj(y_a, y_b, y_c, hg, y_d, h, group_norm_g[i], w_out[i], ts["tm"])
        h = _ffn(h, seq, ffn_norm_g[i], w_up[i], conv_w[i], conv_b[i], w_down[i], ts["tm"])
        h = _ple(h, p[i].reshape(t, PLE_DIM), ple_norm_g[i], w_ple_gate[i], w_ple[i], final_norm_g,
                 i == depth - 1, ts["tm"])
    return h.reshape(batch, seq, D_MODEL)
```

```python
import functools
import math

import numpy as np
import jax
import jax.numpy as jnp
from jax import lax
from jax.experimental import pallas as pl
from jax.experimental.pallas import tpu as pltpu

F32 = jnp.float32
BF16 = jnp.bfloat16

D_MODEL = 1024
N_HEADS = 4
HEAD_DIM = 64
GROUP_W = 256
MLA_Q_RANK = 256
MLA_KV_RANK = 128
MLA_NOPE = 64
MLA_ROPE = 32
ROPE_THETA = 10000.0
S5_GROUPS = 16
S5_CH = 16
S5_P = 64
D_FF = 2816
PLE_DIM = 256
EPS = 1e-6
N_IN = 2468

V7X_LANES = 128
V7X_SUBLANES = 8
V7X_VMEM_BYTES = 64 * 1024 * 1024
MXU_TILE = 256
VMEM_CAP_BYTES = 58 * 1024 * 1024

HEAD_PAD = V7X_LANES
SUB = 16
NEG_BIG = -1e30

SEG_CQ = 0
SEG_CKV = 256
SEG_KR = 384
SEG_FQ = 512
SEG_FK = 1024
SEG_FV = 1536
SEG_FF = 2048
SEG_HG = 2176
SEG_S5 = 3200
N_PERM = 3456
ROPE_LANE0 = 64
BIAS_LANE0 = 64
LOG2E = math.log2(math.e)


def _vmem_limit(*byte_counts):
    need = int(sum(byte_counts))
    return int(min(VMEM_CAP_BYTES, need + need // 4 + (4 << 20)))


def _nbytes(shape, dtype):
    return int(np.prod(shape)) * jnp.dtype(dtype).itemsize


def _rms(x, g):
    return x * lax.rsqrt(jnp.mean(x * x, axis=-1, keepdims=True) + EPS) * g


def _log_sigmoid(z):
    return jnp.minimum(z, 0.0) - jnp.log1p(jnp.exp(-jnp.abs(z)))


def _sigmoid(z):
    return 1.0 / (1.0 + jnp.exp(-z))


def _iota(shape, dim):
    return lax.broadcasted_iota(jnp.int32, shape, dim)


def _shift_rows(x, k):
    if k == 0:
        return x
    return pltpu.roll(x, k, 0)


def _rope_kernel(pos_ref, freq_ref, ct_ref, s1_ref, s2_ref):
    ang = pos_ref[...].astype(F32) * freq_ref[...]
    lane = _iota(ang.shape, 1)
    half = MLA_ROPE // 2
    sin = jnp.sin(ang)
    ct_ref[...] = jnp.cos(ang)
    s1_ref[...] = jnp.where((lane >= ROPE_LANE0) & (lane < ROPE_LANE0 + half), -sin, 0.0)
    s2_ref[...] = jnp.where((lane >= ROPE_LANE0 + half) & (lane < ROPE_LANE0 + 2 * half), sin, 0.0)


def _rope_tables(positions, tm):
    t = positions.size
    half = MLA_ROPE // 2
    inv_freq = ROPE_THETA ** (-jnp.arange(half, dtype=F32) / half)
    freq = jnp.zeros((1, HEAD_PAD), F32).at[0, ROPE_LANE0:ROPE_LANE0 + 2 * half].set(jnp.tile(inv_freq, 2))
    pos = positions.reshape(t, 1)
    out = jax.ShapeDtypeStruct((t, HEAD_PAD), F32)
    spec = pl.BlockSpec((tm, HEAD_PAD), lambda i: (i, 0))
    return pl.pallas_call(
        _rope_kernel,
        out_shape=(out, out, out),
        grid=(t // tm,),
        in_specs=[pl.BlockSpec((tm, 1), lambda i: (i, 0)), pl.BlockSpec((1, HEAD_PAD), lambda i: (0, 0))],
        out_specs=(spec, spec, spec),
        name="rope_tables",
        compiler_params=pltpu.CompilerParams(dimension_semantics=("arbitrary",)),
    )(pos, freq)


def _inproj_kernel(tiles_per_seq, h_ref, g_ref, w_ref, qg_ref, wuq_ref, kvg_ref, wukv_ref, bf_ref,
                   ct_ref, s1_ref, s2_ref, selq_ref, selk_ref,
                   mq_ref, mk_ref, mv_ref, fq_ref, fk_ref, fv_ref, hg_ref, su_ref, carry_ref):
    i = pl.program_id(0)

    @pl.when(i % tiles_per_seq == 0)
    def _():
        carry_ref[...] = jnp.zeros_like(carry_ref)

    tm = h_ref.shape[0]
    xn = _rms(h_ref[...], g_ref[...]).astype(BF16)

    def seg(a, b):
        return jnp.dot(xn, w_ref[:, a:b], preferred_element_type=F32)

    q = jnp.dot(_rms(seg(SEG_CQ, SEG_CKV), qg_ref[...]).astype(BF16), wuq_ref[...],
                preferred_element_type=F32)
    kv = jnp.dot(_rms(seg(SEG_CKV, SEG_KR), kvg_ref[...]).astype(BF16), wukv_ref[...],
                 preferred_element_type=F32)
    ct, s1, s2 = ct_ref[...], s1_ref[...], s2_ref[...]
    half = MLA_ROPE // 2

    def rope(t):
        return t * ct + pltpu.roll(t, HEAD_PAD - half, 1) * s1 + pltpu.roll(t, half, 1) * s2

    k_pe = rope(seg(SEG_KR, SEG_FQ))
    mla_scale = (MLA_NOPE + MLA_ROPE) ** -0.5 * LOG2E
    for hd in range(N_HEADS):
        sl = slice(hd * HEAD_PAD, (hd + 1) * HEAD_PAD)
        mq_ref[:, sl] = (rope(q[:, sl]) * mla_scale).astype(BF16)
        mk_ref[:, sl] = (kv[:, sl] + k_pe).astype(BF16)
    hp4 = N_HEADS * HEAD_PAD
    ones_pad = jnp.where((_iota((1, hp4), 1) & (HEAD_PAD - 1)) >= HEAD_DIM, 1.0, 0.0)
    mv_ref[...] = (kv[:, hp4:] + ones_pad).astype(BF16)

    lf = _log_sigmoid(seg(SEG_FF, SEG_HG) + bf_ref[...])
    tril = (_iota((tm, tm), 0) >= _iota((tm, tm), 1)).astype(BF16)

    def split3(x):
        a = x.astype(BF16)
        r = x - a.astype(F32)
        b = r.astype(BF16)
        c = (r - b.astype(F32)).astype(BF16)
        return a, b, c

    l1, l2, l3 = split3(lf)
    cum = (jnp.dot(tril, l1, preferred_element_type=F32) + jnp.dot(tril, l2, preferred_element_type=F32)
           + jnp.dot(tril, l3, preferred_element_type=F32)) + carry_ref[...]
    carry_ref[...] = cum[tm - 1:tm, :]
    lane = _iota(cum.shape, 1)
    c1, c2, c3 = split3(cum * LOG2E)
    keep = lane < N_HEADS
    parts = (jnp.where(keep, c1.astype(F32), 0.0)
             + pltpu.roll(jnp.where(keep, c2.astype(F32), 0.0), N_HEADS, 1)
             + pltpu.roll(jnp.where(keep, c3.astype(F32), 0.0), 2 * N_HEADS, 1)
             + jnp.where(lane == 3 * N_HEADS, 1.0, 0.0)).astype(BF16)
    fox_scale = HEAD_DIM ** -0.5 * LOG2E
    fq_ref[...] = (seg(SEG_FQ, SEG_FK) * fox_scale
                   + jnp.dot(parts, selq_ref[...], preferred_element_type=F32)).astype(BF16)
    fk_ref[...] = (seg(SEG_FK, SEG_FV)
                   + jnp.dot(parts, selk_ref[...], preferred_element_type=F32)).astype(BF16)
    fv_ref[...] = (seg(SEG_FV, SEG_FF) + ones_pad).astype(BF16)

    hg_ref[...] = seg(SEG_HG, SEG_S5)
    su_ref[...] = seg(SEG_S5, N_PERM)


def _permute_inproj(w):
    def zeros(n):
        return jnp.zeros((w.shape[0], n), w.dtype)

    cols = [w[:, 0:384], zeros(ROPE_LANE0), w[:, 384:416], zeros(HEAD_PAD - ROPE_LANE0 - MLA_ROPE)]
    for base in (416, 672, 928):
        for hd in range(N_HEADS):
            cols += [w[:, base + HEAD_DIM * hd:base + HEAD_DIM * (hd + 1)], zeros(HEAD_PAD - HEAD_DIM)]
    cols += [w[:, 1184:1188], zeros(HEAD_PAD - N_HEADS), w[:, 1188:N_IN]]
    out = jnp.concatenate(cols, axis=1)
    assert out.shape[1] == N_PERM
    return out


def _bias_selectors():
    selq = np.zeros((HEAD_PAD, N_HEADS * HEAD_PAD), np.float32)
    selk = np.zeros((HEAD_PAD, N_HEADS * HEAD_PAD), np.float32)
    one = 3 * N_HEADS
    for hd in range(N_HEADS):
        for j in range(3):
            selq[N_HEADS * j + hd, hd * HEAD_PAD + BIAS_LANE0 + j] = 1.0
            selq[one, hd * HEAD_PAD + BIAS_LANE0 + 3 + j] = 1.0
            selk[one, hd * HEAD_PAD + BIAS_LANE0 + j] = 1.0
            selk[N_HEADS * j + hd, hd * HEAD_PAD + BIAS_LANE0 + 3 + j] = -1.0
    return jnp.asarray(selq, BF16), jnp.asarray(selk, BF16)


def _inproj(h, seq, attn_g, w_in, q_g, w_uq, kv_g, w_ukv, b_f, tables, tm):
    t = h.shape[0]
    w_perm = _permute_inproj(w_in.astype(BF16))
    wuq = jnp.pad(w_uq.reshape(MLA_Q_RANK, N_HEADS, MLA_NOPE + MLA_ROPE),
                  ((0, 0), (0, 0), (0, HEAD_PAD - MLA_NOPE - MLA_ROPE))).reshape(MLA_Q_RANK, -1).astype(BF16)
    wkv = w_ukv.reshape(MLA_KV_RANK, N_HEADS, 2 * HEAD_DIM)
    head_pad = ((0, 0), (0, 0), (0, HEAD_PAD - HEAD_DIM))
    wk = jnp.pad(wkv[:, :, :MLA_NOPE], head_pad).reshape(MLA_KV_RANK, -1)
    wv = jnp.pad(wkv[:, :, MLA_NOPE:], head_pad).reshape(MLA_KV_RANK, -1)
    wukv = jnp.concatenate([wk, wv], axis=1).astype(BF16)
    bf = jnp.zeros((1, HEAD_PAD), F32).at[0, :N_HEADS].set(b_f.astype(F32))
    selq, selk = _bias_selectors()
    ct, s1, s2 = tables
    hp4 = N_HEADS * HEAD_PAD

    def rows(width):
        return pl.BlockSpec((tm, width), lambda i: (i, 0))

    def whole(a):
        return pl.BlockSpec(a.shape, lambda i: (0,) * a.ndim)

    args = (h, attn_g.reshape(1, -1), w_perm, q_g.reshape(1, -1), wuq, kv_g.reshape(1, -1), wukv, bf,
            ct, s1, s2, selq, selk)
    in_specs = [rows(D_MODEL)] + [whole(a) for a in args[1:8]] + [rows(HEAD_PAD)] * 3 + [whole(selq), whole(selk)]
    out_widths = (hp4,) * 6
    out_shape = tuple(jax.ShapeDtypeStruct((t, w), BF16) for w in out_widths) + (
        jax.ShapeDtypeStruct((t, 4 * GROUP_W), F32), jax.ShapeDtypeStruct((t, GROUP_W), F32))
    out_specs = tuple(rows(w) for w in out_widths) + (rows(4 * GROUP_W), rows(GROUP_W))
    vmem = _vmem_limit(2 * _nbytes(w_perm.shape, BF16), 2 * _nbytes((tm, D_MODEL), F32),
                       2 * sum(_nbytes((tm, w), BF16) for w in out_widths),
                       2 * _nbytes((tm, 5 * GROUP_W), F32), _nbytes((tm, N_PERM), F32),
                       _nbytes((tm, D_MODEL), F32))
    return pl.pallas_call(
        functools.partial(_inproj_kernel, seq // tm),
        out_shape=out_shape,
        grid=(t // tm,),
        in_specs=in_specs,
        out_specs=out_specs,
        scratch_shapes=[pltpu.VMEM((1, HEAD_PAD), F32)],
        name="inproj",
        compiler_params=pltpu.CompilerParams(dimension_semantics=("arbitrary",), vmem_limit_bytes=vmem),
    )(*args)


def _attn_kernel(tq, tkv, q_ref, k_ref, v_ref, o_ref, m_ref, acc_ref):
    qi = pl.program_id(1)
    m_ref[...] = jnp.full_like(m_ref, NEG_BIG)
    acc_ref[...] = jnp.zeros_like(acc_ref)
    per_q = tq // tkv

    def block(start, width, diag_offset):
        rows = pl.ds(pl.multiple_of(start, tkv), width)
        for hd in range(N_HEADS):
            sl = slice(hd * HEAD_PAD, (hd + 1) * HEAD_PAD)
            s = lax.dot_general(q_ref[:, sl], k_ref[rows, sl], (((1,), (1,)), ((), ())),
                                preferred_element_type=F32)
            if diag_offset is not None:
                visible = _iota(s.shape, 1) + diag_offset <= _iota(s.shape, 0)
                s = jnp.where(visible, s, NEG_BIG)
            m_old = m_ref[hd]
            m_new = jnp.maximum(m_old, jnp.max(s, axis=1, keepdims=True))
            p = jnp.exp2(s - jnp.concatenate([m_new] * (width // HEAD_PAD), axis=1))
            m_ref[hd] = m_new
            acc_ref[hd] = (jnp.exp2(m_old - m_new) * acc_ref[hd]
                           + jnp.dot(p.astype(BF16), v_ref[rows, sl], preferred_element_type=F32))

    n_full = qi * per_q

    def wide_block(jj, carry):
        block(jj * (2 * tkv), 2 * tkv, None)
        return carry

    lax.fori_loop(0, n_full // 2, wide_block, 0)

    @pl.when(n_full % 2 == 1)
    def _():
        block((n_full - 1) * tkv, tkv, None)

    for r in range(per_q):
        block((n_full + r) * tkv, tkv, r * tkv)

    low = _iota((tq, HEAD_PAD), 1) < HEAD_DIM
    for pair in range(N_HEADS // 2):
        a0, a1 = acc_ref[2 * pair], acc_ref[2 * pair + 1]
        n0 = a0 / pltpu.roll(a0, HEAD_DIM, 1)
        n1 = a1 / pltpu.roll(a1, HEAD_DIM, 1)
        o_ref[:, pair * HEAD_PAD:(pair + 1) * HEAD_PAD] = jnp.where(low, n0, pltpu.roll(n1, HEAD_DIM, 1))


def _attention(q, k, v, batch, seq, tq, tkv):
    t = q.shape[0]
    hp4 = N_HEADS * HEAD_PAD
    vmem = _vmem_limit(2 * _nbytes((seq, hp4), BF16), 2 * _nbytes((tq, hp4), BF16),
                       2 * _nbytes((tq, GROUP_W), F32), 2 * N_HEADS * _nbytes((tq, HEAD_PAD), F32),
                       6 * _nbytes((tq, 2 * tkv), F32))
    resident = pl.BlockSpec((seq, hp4), lambda b, i: (b, 0), pipeline_mode=pl.Buffered(1))
    return pl.pallas_call(
        functools.partial(_attn_kernel, tq, tkv),
        out_shape=jax.ShapeDtypeStruct((t, GROUP_W), F32),
        grid=(batch, seq // tq),
        in_specs=[pl.BlockSpec((tq, hp4), lambda b, i: (b * (seq // tq) + i, 0)), resident, resident],
        out_specs=pl.BlockSpec((tq, GROUP_W), lambda b, i: (b * (seq // tq) + i, 0)),
        scratch_shapes=[pltpu.VMEM((N_HEADS, tq, HEAD_PAD), F32), pltpu.VMEM((N_HEADS, tq, HEAD_PAD), F32)],
        name="causal_attention",
        compiler_params=pltpu.CompilerParams(dimension_semantics=("arbitrary", "arbitrary"),
                                             vmem_limit_bytes=vmem),
    )(q, k, v)


def _hgrn_kernel(win, q0_ref, q1_ref, f0_ref, f1_ref, v0_ref, v1_ref, lb_ref, ee_ref, o_ref,
                 st_ref, bc_ref, kk_ref, vs_ref, qe_ref, ke_ref, od_ref, dec_ref, vt_ref):
    @pl.when(pl.program_id(1) == 0)
    def _():
        st_ref[...] = jnp.zeros_like(st_ref)

    q_ref, f_ref, v_ref = (q0_ref, q1_ref), (f0_ref, f1_ref), (v0_ref, v1_ref)
    r = q0_ref.shape[0]
    g = r // SUB
    lb = lb_ref[...]
    log_lb = jnp.log(lb)
    log_1m = jnp.log1p(-lb)

    def slab(halves, s):
        return jnp.concatenate([h[pl.ds(s, g, stride=SUB), :] for h in halves], axis=1)

    def put_rows(ref, s, x):
        for half in range(2):
            ref[half, pl.ds(s, g, stride=SUB), :] = x[:, half * V7X_LANES:(half + 1) * V7X_LANES]

    def get_rows(ref, row0):
        return jnp.concatenate([ref[0, pl.ds(row0, SUB), :], ref[1, pl.ds(row0, SUB), :]], axis=1)

    run = None
    for s in range(SUB):
        z = slab(f_ref, s)
        b = log_1m + _log_sigmoid(z)
        log_f = jnp.maximum(log_lb, b) + jnp.log1p(jnp.exp(-jnp.abs(log_lb - b)))
        run = log_f if s == 0 else run + log_f
        bc_ref[s] = run
        kk_ref[s] = (1.0 - lb) * _sigmoid(-z)
        vs_ref[s] = slab(v_ref, s)
    total = bc_ref[SUB - 1]
    dec_ref[...] = jnp.exp(total)

    for s in range(SUB):
        qs = slab(q_ref, s)
        bcs = bc_ref[s]
        prods = [(qs * kk_ref[j] * jnp.exp(bcs - bc_ref[j])).astype(BF16) for j in range(s)]
        prods.append((qs * kk_ref[s]).astype(BF16))
        red = jnp.dot(jnp.concatenate(prods, axis=0), ee_ref[...], preferred_element_type=F32)
        od = red[0:g] * vs_ref[0]
        for j in range(1, s + 1):
            od = od + red[j * g:(j + 1) * g] * vs_ref[j]
        put_rows(od_ref, s, od)
        put_rows(qe_ref, s, qs * jnp.exp(bcs))
        put_rows(ke_ref, s, kk_ref[s] * jnp.exp(total - bcs))
    for w in range(r // win):
        for half in range(2):
            vt_ref[w, half * V7X_LANES:(half + 1) * V7X_LANES, :] = (
                v_ref[half][w * win:(w + 1) * win, :].T.astype(BF16))

    lane_head = _iota((GROUP_W, GROUP_W), 1) // HEAD_DIM
    row_head = _iota((GROUP_W, GROUP_W), 0) // HEAD_DIM
    same_head = lane_head == row_head
    per_win = win // SUB

    def window(w, carry):
        vt = vt_ref[w]
        st = st_ref[...]
        for c in range(per_win):
            row0 = pl.multiple_of(w * win + c * SUB, SUB)
            qe = get_rows(qe_ref, row0).astype(BF16)
            o_state = lax.dot_general(qe, st.astype(BF16), (((1,), (1,)), ((), ())),
                                      preferred_element_type=F32)
            o_ref[pl.ds(row0, SUB), :] = get_rows(od_ref, row0) + o_state
            pieces = [get_rows(ke_ref, row0).astype(BF16)]
            if c:
                pieces.insert(0, jnp.zeros((c * SUB, GROUP_W), BF16))
            if c < per_win - 1:
                pieces.append(jnp.zeros(((per_win - 1 - c) * SUB, GROUP_W), BF16))
            upd = jnp.dot(vt, jnp.concatenate(pieces, axis=0), preferred_element_type=F32)
            st = st * dec_ref[pl.ds(w * per_win + c, 1), :] + jnp.where(same_head, upd, 0.0)
        st_ref[...] = st
        return carry

    lax.fori_loop(0, r // win, window, 0)


def _hgrn(hg, lb, batch, seq, r):
    t = hg.shape[0]
    win = min(r, V7X_LANES)
    ee = jnp.asarray(np.kron(np.eye(N_HEADS, dtype=np.float32), np.ones((HEAD_DIM, HEAD_DIM), np.float32)), BF16)
    nblk = seq // r

    def half(c):
        return pl.BlockSpec((r, V7X_LANES), lambda b, i: (b * nblk + i, c))

    tile = _nbytes((r, GROUP_W), F32)
    vmem = _vmem_limit(8 * tile, 7 * tile, 8 * tile)
    slabs = pltpu.VMEM((SUB, r // SUB, GROUP_W), F32)
    rows = pltpu.VMEM((2, r, V7X_LANES), F32)
    return pl.pallas_call(
        functools.partial(_hgrn_kernel, win),
        out_shape=jax.ShapeDtypeStruct((t, GROUP_W), F32),
        grid=(batch, nblk),
        in_specs=[half(0), half(1), half(2), half(3), half(4), half(5),
                  pl.BlockSpec((1, GROUP_W), lambda b, i: (0, 0)),
                  pl.BlockSpec((GROUP_W, GROUP_W), lambda b, i: (0, 0))],
        out_specs=pl.BlockSpec((r, GROUP_W), lambda b, i: (b * nblk + i, 0)),
        scratch_shapes=[pltpu.VMEM((GROUP_W, GROUP_W), F32), slabs, slabs, slabs, rows, rows, rows,
                        pltpu.VMEM((r // SUB, GROUP_W), F32), pltpu.VMEM((r // win, GROUP_W, win), BF16)],
        name="hgrn2",
        compiler_params=pltpu.CompilerParams(dimension_semantics=("arbitrary", "arbitrary"),
                                             vmem_limit_bytes=vmem),
    )(hg, hg, hg, hg, hg, hg, lb.reshape(1, -1), ee)


def _s5_local_kernel(u_ref, bd_ref, bbd_ref, we_ref, d_ref, y_ref, e_ref):
    r = u_ref.shape[0]
    u = u_ref[...]
    rsub = _iota((r, GROUP_W), 0) & (SUB - 1)
    y = d_ref[...] * u
    for j in range(SUB):
        uj = jnp.where(rsub >= j, _shift_rows(u, j), 0.0).astype(BF16)
        y = y + jnp.dot(uj, bd_ref[j], preferred_element_type=F32)
    y_ref[...] = y
    n = S5_GROUPS * S5_P
    bu = jnp.dot(u.astype(BF16), bbd_ref[...], preferred_element_type=F32).reshape(r // SUB, SUB, 2 * n)
    bur, bui = bu[:, :, :n], bu[:, :, n:]
    wr, wi = we_ref[:, :n], we_ref[:, n:]
    e_ref[:, :n] = jnp.sum(wr * bur - wi * bui, axis=1)
    e_ref[:, n:] = jnp.sum(wr * bui + wi * bur, axis=1)


def _s5_scan_kernel(e_ref, a_ref, x_ref):
    nchunk = e_ref.shape[0]
    n = S5_GROUPS * S5_P
    xr, xi = e_ref[:, :n], e_ref[:, n:]
    pr, pi = a_ref[:, :n], a_ref[:, n:]
    row = _iota((nchunk, n), 0)
    k = 1
    while k < nchunk:
        sr = jnp.where(row >= k, pltpu.roll(xr, k, 0), 0.0)
        si = jnp.where(row >= k, pltpu.roll(xi, k, 0), 0.0)
        xr, xi = xr + pr * sr - pi * si, xi + pr * si + pi * sr
        pr, pi = pr * pr - pi * pi, 2.0 * pr * pi
        k *= 2
    x_ref[:, :n] = jnp.where(row >= 1, pltpu.roll(xr, 1, 0), 0.0)
    x_ref[:, n:] = jnp.where(row >= 1, pltpu.roll(xi, 1, 0), 0.0)


def _s5_out_kernel(y1_ref, x_ref, a1_ref, cbd_ref, wg_ref, bg_ref, o_ref, xs_ref):
    r = y1_ref.shape[0]
    n = S5_GROUPS * S5_P
    ar, ai = a1_ref[:, :n], a1_ref[:, n:]

    def chunk(c, carry):
        base = pl.multiple_of(c * SUB, SUB)
        xr = x_ref[pl.ds(c, 1), :n]
        xi = x_ref[pl.ds(c, 1), n:]
        xs_ref[pl.ds(base, SUB), :n] = (ar * xr - ai * xi).astype(BF16)
        xs_ref[pl.ds(base, SUB), n:] = (ar * xi + ai * xr).astype(BF16)
        return carry

    lax.fori_loop(0, r // SUB, chunk, 0)
    y = y1_ref[...] + jnp.dot(xs_ref[...], cbd_ref[...], preferred_element_type=F32)
    zact = 0.5 * y * (1.0 + jnp.tanh(math.sqrt(2.0 / math.pi) * (y + 0.044715 * (y * y * y))))
    gate = jnp.dot(zact.astype(BF16), wg_ref[...], preferred_element_type=F32) + bg_ref[...]
    o_ref[...] = zact * _sigmoid(gate)


def _s5_prep_kernel(ar_ref, ai_ref, bre_ref, bim_ref, cre_ref, cim_ref,
                    bd_ref, bbd_ref, cbd_ref, wend_ref, ain_ref, achunk_ref):
    n = S5_GROUPS * S5_P
    hi = lax.Precision.HIGHEST
    ar, ai = ar_ref[...], ai_ref[...]
    bre, bim, cre, cim = bre_ref[...], bim_ref[...], cre_ref[...], cim_ref[...]
    bbd_ref[:, :n] = bre.astype(BF16)
    bbd_ref[:, n:] = bim.astype(BF16)
    cbd_ref[:n, :] = cre.astype(BF16)
    cbd_ref[n:, :] = (-cim).astype(BF16)
    pr, pi = jnp.ones_like(ar), jnp.zeros_like(ar)
    for j in range(SUB):
        k_lag = (jnp.dot(bre * pr - bim * pi, cre, precision=hi, preferred_element_type=F32)
                 - jnp.dot(bre * pi + bim * pr, cim, precision=hi, preferred_element_type=F32))
        bd_ref[j] = k_lag.astype(BF16)
        wend_ref[SUB - 1 - j:SUB - j, :n] = pr
        wend_ref[SUB - 1 - j:SUB - j, n:] = pi
        pr, pi = pr * ar - pi * ai, pr * ai + pi * ar
        ain_ref[j:j + 1, :n] = pr
        ain_ref[j:j + 1, n:] = pi
    achunk_ref[:, :n] = pr
    achunk_ref[:, n:] = pi


def _s5_operators(lam_re, lam_im, log_step, b_re, b_im, c_re, c_im):
    step = jnp.exp(log_step.astype(F32))[:, None]
    lre = jnp.minimum(lam_re.astype(F32), -1e-4)
    lim = lam_im.astype(F32)
    mag = jnp.exp(lre * step)
    a_re, a_im = mag * jnp.cos(lim * step), mag * jnp.sin(lim * step)
    den = lre * lre + lim * lim
    coef_re = ((a_re - 1.0) * lre + a_im * lim) / den
    coef_im = (a_im * lre - (a_re - 1.0) * lim) / den
    br, bi = b_re.astype(F32), b_im.astype(F32)
    bb_re = coef_re[..., None] * br - coef_im[..., None] * bi
    bb_im = coef_re[..., None] * bi + coef_im[..., None] * br
    cr, ci = c_re.astype(F32), c_im.astype(F32)
    n = S5_GROUPS * S5_P
    same_group = (np.arange(GROUP_W)[:, None] // S5_CH) == (np.arange(n)[None, :] // S5_P)
    in_mask = jnp.asarray(same_group, F32)
    out_mask = jnp.asarray(same_group.T, F32)

    def in_side(b):
        return jnp.tile(b.transpose(0, 2, 1).reshape(GROUP_W, S5_P), (1, S5_GROUPS)) * in_mask

    def out_side(c):
        return jnp.tile(c.transpose(0, 2, 1).reshape(n, S5_CH), (1, S5_GROUPS)) * out_mask

    args = (a_re.reshape(1, n), a_im.reshape(1, n), in_side(bb_re), in_side(bb_im), out_side(cr), out_side(ci))
    out_shape = (jax.ShapeDtypeStruct((SUB, GROUP_W, GROUP_W), BF16),
                 jax.ShapeDtypeStruct((GROUP_W, 2 * n), BF16),
                 jax.ShapeDtypeStruct((2 * n, GROUP_W), BF16),
                 jax.ShapeDtypeStruct((SUB, 2 * n), F32),
                 jax.ShapeDtypeStruct((SUB, 2 * n), F32),
                 jax.ShapeDtypeStruct((1, 2 * n), F32))
    return pl.pallas_call(
        _s5_prep_kernel,
        out_shape=out_shape,
        name="s5_prep",
        compiler_params=pltpu.CompilerParams(
            vmem_limit_bytes=_vmem_limit(4 * _nbytes((GROUP_W, n), F32), 10 * _nbytes((GROUP_W, n), F32),
                                         2 * _nbytes((SUB, GROUP_W, GROUP_W), BF16))),
    )(*args)


def _s5(u, ops, d_skip, w_glu, b_glu, batch, seq, r):
    t = u.shape[0]
    bd, bbd, cbd, w_end, a_in, a_chunk = ops
    n2 = 2 * S5_GROUPS * S5_P
    nck = seq // SUB

    def whole(a):
        return pl.BlockSpec(a.shape, lambda *_: (0,) * a.ndim)

    rows = pl.BlockSpec((r, GROUP_W), lambda i: (i, 0))
    crow = pl.BlockSpec((r // SUB, n2), lambda i: (i, 0))
    d2 = d_skip.astype(F32).reshape(1, -1)
    y1, e = pl.pallas_call(
        _s5_local_kernel,
        out_shape=(jax.ShapeDtypeStruct((t, GROUP_W), F32), jax.ShapeDtypeStruct((t // SUB, n2), F32)),
        grid=(t // r,),
        in_specs=[rows, whole(bd), whole(bbd), whole(w_end), whole(d2)],
        out_specs=(rows, crow),
        name="s5_local",
        compiler_params=pltpu.CompilerParams(
            dimension_semantics=("arbitrary",),
            vmem_limit_bytes=_vmem_limit(2 * _nbytes(bd.shape, BF16), 2 * _nbytes(bbd.shape, BF16),
                                         6 * _nbytes((r, n2), F32), 8 * _nbytes((r, GROUP_W), F32))),
    )(u, bd, bbd, w_end, d2)
    xin = pl.pallas_call(
        _s5_scan_kernel,
        out_shape=jax.ShapeDtypeStruct((t // SUB, n2), F32),
        grid=(batch,),
        in_specs=[pl.BlockSpec((nck, n2), lambda b: (b, 0)), whole(a_chunk)],
        out_specs=pl.BlockSpec((nck, n2), lambda b: (b, 0)),
        name="s5_scan",
        compiler_params=pltpu.CompilerParams(
            dimension_semantics=("arbitrary",), vmem_limit_bytes=_vmem_limit(10 * _nbytes((nck, n2), F32))),
    )(e, a_chunk)
    wg = w_glu.astype(BF16)
    bg = b_glu.astype(F32).reshape(1, -1)
    return pl.pallas_call(
        _s5_out_kernel,
        out_shape=jax.ShapeDtypeStruct((t, GROUP_W), F32),
        grid=(t // r,),
        in_specs=[rows, crow, whole(a_in), whole(cbd), whole(wg), whole(bg)],
        out_specs=rows,
        scratch_shapes=[pltpu.VMEM((r, n2), BF16)],
        name="s5_out",
        compiler_params=pltpu.CompilerParams(
            dimension_semantics=("arbitrary",),
            vmem_limit_bytes=_vmem_limit(2 * _nbytes(cbd.shape, BF16), 3 * _nbytes((r, n2), BF16),
                                         10 * _nbytes((r, GROUP_W), F32))),
    )(y1, xin, a_in, cbd, wg, bg)


HALO = 2 * V7X_SUBLANES


def _post_kernel(tiles_per_seq, final, ya_ref, yb_ref, yc_ref, gate_ref, yd_ref, h_ref, p_ref,
                 gn_ref, wo_ref, fg_ref, wu_ref, cw_ref, cb_ref, wd_ref, pg_ref, wpg_ref, wp_ref, ng_ref,
                 o_ref, xn_ref, up_ref, act_ref):
    tm = h_ref.shape[0]
    gw = GROUP_W

    @pl.when(pl.program_id(0) % tiles_per_seq == 0)
    def _():
        xn_ref[0:HALO, :] = jnp.zeros((HALO, D_MODEL), BF16)

    parts = (_rms(ya_ref[...], gn_ref[:, 0:gw]),
             _rms(yb_ref[...], gn_ref[:, gw:2 * gw]),
             _rms(yc_ref[...], gn_ref[:, 2 * gw:3 * gw]) * _sigmoid(gate_ref[...]),
             _rms(yd_ref[...], gn_ref[:, 3 * gw:4 * gw]))
    h1 = h_ref[...]
    for g, part in enumerate(parts):
        h1 = h1 + jnp.dot(part.astype(BF16), wo_ref[g * gw:(g + 1) * gw, :], preferred_element_type=F32)
    o_ref[...] = h1
    xn_ref[HALO:, :] = _rms(h1, fg_ref[...]).astype(BF16)

    nchunk = D_FF // MXU_TILE

    def conv(slot, c):
        cols = slice(c * MXU_TILE, (c + 1) * MXU_TILE)
        up_ref[slot] = jnp.dot(xn_ref[...], wu_ref[:, cols], preferred_element_type=F32)
        return (cb_ref[:, cols] + cw_ref[0:1, cols] * up_ref[slot, pl.ds(HALO - 2, tm), :]
                + cw_ref[1:2, cols] * up_ref[slot, pl.ds(HALO - 1, tm), :]
                + cw_ref[2:3, cols] * up_ref[slot, pl.ds(HALO, tm), :])

    for c in range(nchunk):
        slot = 2 * (c % 2)
        gate = conv(slot, c)
        val = conv(slot + 1, c + nchunk)
        act_ref[:, c * MXU_TILE:(c + 1) * MXU_TILE] = (gate * _sigmoid(gate) * val).astype(BF16)
    xn_ref[0:HALO, :] = xn_ref[tm:tm + HALO, :]
    h2 = o_ref[...] + jnp.dot(act_ref[...], wd_ref[...], preferred_element_type=F32)

    pgate = _sigmoid(jnp.dot(_rms(h2, pg_ref[...]).astype(BF16), wpg_ref[...], preferred_element_type=F32))
    out = h2 + pgate * jnp.dot(p_ref[...].astype(BF16), wp_ref[...], preferred_element_type=F32)
    o_ref[...] = _rms(out, ng_ref[...]) if final else out


def _post(ya, yb, yc, hg, yd, h, p, seq, gn, w_out, ffn_g, w_up, conv_w, conv_b, w_down,
          ple_g, w_pgate, w_ple, final_g, final, tm):
    t = h.shape[0]
    rows = pl.BlockSpec((tm, GROUP_W), lambda i: (i, 0))
    wide = pl.BlockSpec((tm, D_MODEL), lambda i: (i, 0))

    def whole(a):
        return pl.BlockSpec(a.shape, lambda i: (0,) * a.ndim, pipeline_mode=pl.Buffered(1))

    consts = (gn.reshape(1, -1), w_out.astype(BF16), ffn_g.reshape(1, -1), w_up.astype(BF16), conv_w,
              conv_b.reshape(1, -1), w_down.astype(BF16), ple_g.reshape(1, -1), w_pgate.astype(BF16),
              w_ple.astype(BF16), final_g.reshape(1, -1))
    vmem = _vmem_limit(sum(_nbytes(c.shape, c.dtype) for c in consts), 12 * _nbytes((tm, GROUP_W), F32),
                       4 * _nbytes((tm, D_MODEL), F32), _nbytes((tm + HALO, D_MODEL), BF16),
                       4 * _nbytes((tm + HALO, MXU_TILE), F32), _nbytes((tm, D_FF), BF16),
                       3 * _nbytes((tm, D_MODEL), F32))
    return pl.pallas_call(
        functools.partial(_post_kernel, seq // tm, final),
        out_shape=jax.ShapeDtypeStruct((t, D_MODEL), F32),
        grid=(t // tm,),
        in_specs=[rows, rows, rows, pl.BlockSpec((tm, GROUP_W), lambda i: (i, 3)), rows, wide, rows]
        + [whole(c) for c in consts],
        out_specs=wide,
        scratch_shapes=[pltpu.VMEM((tm + HALO, D_MODEL), BF16), pltpu.VMEM((4, tm + HALO, MXU_TILE), F32),
                        pltpu.VMEM((tm, D_FF), BF16)],
        name="post_mixer",
        compiler_params=pltpu.CompilerParams(dimension_semantics=("arbitrary",), vmem_limit_bytes=vmem),
    )(ya, yb, yc, hg, yd, h, p, *consts)


def _tiles(seq):
    tm = min(512, seq)
    return dict(tm=tm, tq=min(512, seq), tkv=min(512, seq), r=min(512, seq))


def kernel(x, p, positions, attn_norm_g, w_in, mla_q_norm_g, mla_w_uq, mla_kv_norm_g, mla_w_ukv, fox_b_f,
           hgrn_lb_param, s5_lam_re, s5_lam_im, s5_log_step, s5_b_re, s5_b_im, s5_c_re, s5_c_im, s5_d,
           s5_w_glu, s5_b_glu, group_norm_g, w_out, ffn_norm_g, w_up, conv_w, conv_b, w_down, ple_norm_g,
           w_ple_gate, w_ple, final_norm_g):
    batch, seq, _ = x.shape
    depth = w_in.shape[0]
    t = batch * seq
    ts = _tiles(seq)
    assert seq % ts["tm"] == 0 and seq % ts["tq"] == 0 and ts["tq"] % ts["tkv"] == 0 and seq % ts["r"] == 0
    assert (seq // SUB) & (seq // SUB - 1) == 0, "chunk scan assumes a power-of-two chunk count"

    lb_all = jnp.cumsum(jax.nn.softmax(hgrn_lb_param.astype(F32), axis=0), axis=0)
    lb_all = lb_all - lb_all[0:1]
    tables = _rope_tables(positions, ts["tm"])
    h = x.reshape(t, D_MODEL)
    for i in range(depth):
        mq, mk, mv, fq, fk, fv, hg, su = _inproj(
            h, seq, attn_norm_g[i], w_in[i], mla_q_norm_g[i], mla_w_uq[i], mla_kv_norm_g[i], mla_w_ukv[i],
            fox_b_f[i], tables, ts["tm"])
        y_a = _attention(mq, mk, mv, batch, seq, ts["tq"], ts["tkv"])
        y_b = _attention(fq, fk, fv, batch, seq, ts["tq"], ts["tkv"])
        y_c = _hgrn(hg, lb_all[i], batch, seq, ts["r"])
        ops = _s5_operators(s5_lam_re[i], s5_lam_im[i], s5_log_step[i], s5_b_re[i], s5_b_im[i],
                            s5_c_re[i], s5_c_im[i])
        y_d = _s5(su, ops, s5_d[i], s5_w_glu[i], s5_b_glu[i], batch, seq, ts["r"])
        h = _post(y_a, y_b, y_c, hg, y_d, h, p[i].reshape(t, PLE_DIM), seq, group_norm_g[i], w_out[i],
                  ffn_norm_g[i], w_up[i], conv_w[i], conv_b[i], w_down[i], ple_norm_g[i], w_ple_gate[i],
                  w_ple[i], final_norm_g, i == depth - 1, ts["tm"])
    return h.reshape(batch, seq, D_MODEL)
```

```python
import functools
import math

import numpy as np
import jax
import jax.numpy as jnp
from jax import lax
from jax.experimental import pallas as pl
from jax.experimental.pallas import tpu as pltpu

F32 = jnp.float32
BF16 = jnp.bfloat16

D_MODEL = 1024
N_HEADS = 4
HEAD_DIM = 64
GROUP_W = 256
MLA_Q_RANK = 256
MLA_KV_RANK = 128
MLA_NOPE = 64
MLA_ROPE = 32
ROPE_THETA = 10000.0
S5_GROUPS = 16
S5_CH = 16
S5_P = 64
D_FF = 2816
PLE_DIM = 256
EPS = 1e-6
N_IN = 2468

V7X_LANES = 128
V7X_SUBLANES = 8
V7X_VMEM_BYTES = 64 * 1024 * 1024
MXU_TILE = 256
VMEM_CAP_BYTES = 58 * 1024 * 1024

HEAD_PAD = V7X_LANES
SUB = 16
NEG_BIG = -1e30

SEG_CQ = 0
SEG_CKV = 256
SEG_KR = 384
SEG_FOX = 512
SEG_HG = 1280
SEG_S5 = 2304
SEG_FF = 2560
N_PERM = 2688
ROPE_LANE0 = 64
BIAS_LANE0 = 64
LOG2E = math.log2(math.e)


def _vmem_limit(*byte_counts):
    need = int(sum(byte_counts))
    return int(min(VMEM_CAP_BYTES, need + need // 4 + (4 << 20)))


def _nbytes(shape, dtype):
    return int(np.prod(shape)) * jnp.dtype(dtype).itemsize


def _rms(x, g):
    return x * lax.rsqrt(jnp.mean(x * x, axis=-1, keepdims=True) + EPS) * g


def _log_sigmoid(z):
    return jnp.minimum(z, 0.0) - jnp.log1p(jnp.exp(-jnp.abs(z)))


def _sigmoid(z):
    return 1.0 / (1.0 + jnp.exp(-z))


def _iota(shape, dim):
    return lax.broadcasted_iota(jnp.int32, shape, dim)


def _shift_rows(x, k):
    if k == 0:
        return x
    return pltpu.roll(x, k, 0)


def _rope_kernel(pos_ref, freq_ref, ct_ref, s1_ref, s2_ref):
    ang = pos_ref[...].astype(F32) * freq_ref[...]
    lane = _iota(ang.shape, 1)
    half = MLA_ROPE // 2
    sin = jnp.sin(ang)
    ct_ref[...] = jnp.cos(ang)
    s1_ref[...] = jnp.where((lane >= ROPE_LANE0) & (lane < ROPE_LANE0 + half), -sin, 0.0)
    s2_ref[...] = jnp.where((lane >= ROPE_LANE0 + half) & (lane < ROPE_LANE0 + 2 * half), sin, 0.0)


def _rope_tables(positions, tm):
    t = positions.size
    half = MLA_ROPE // 2
    inv_freq = ROPE_THETA ** (-jnp.arange(half, dtype=F32) / half)
    freq = jnp.zeros((1, HEAD_PAD), F32).at[0, ROPE_LANE0:ROPE_LANE0 + 2 * half].set(jnp.tile(inv_freq, 2))
    pos = positions.reshape(t, 1)
    out = jax.ShapeDtypeStruct((t, HEAD_PAD), F32)
    spec = pl.BlockSpec((tm, HEAD_PAD), lambda i: (i, 0))
    return pl.pallas_call(
        _rope_kernel,
        out_shape=(out, out, out),
        grid=(t // tm,),
        in_specs=[pl.BlockSpec((tm, 1), lambda i: (i, 0)), pl.BlockSpec((1, HEAD_PAD), lambda i: (0, 0))],
        out_specs=(spec, spec, spec),
        name="rope_tables",
        compiler_params=pltpu.CompilerParams(dimension_semantics=("arbitrary",)),
    )(pos, freq)


def _inproj_kernel(tiles_per_seq, h_ref, g_ref, w_ref, qg_ref, wuq_ref, kvg_ref, wukv_ref, bf_ref,
                   ct_ref, s1_ref, s2_ref, selq_ref, selk_ref,
                   mq_ref, mk_ref, mv_ref, fq_ref, fk_ref, fv_ref, hg_ref, su_ref, carry_ref):
    i = pl.program_id(0)

    @pl.when(i % tiles_per_seq == 0)
    def _():
        carry_ref[...] = jnp.zeros_like(carry_ref)

    tm = h_ref.shape[0]
    xn = _rms(h_ref[...], g_ref[...]).astype(BF16)

    proj = jnp.dot(xn, w_ref[...], preferred_element_type=F32)

    def seg(a, b):
        return proj[:, a:b]

    q = jnp.dot(_rms(seg(SEG_CQ, SEG_CKV), qg_ref[...]).astype(BF16), wuq_ref[...],
                preferred_element_type=F32)
    ckv_kr = seg(SEG_CKV, SEG_FOX)
    kv = jnp.dot(_rms(ckv_kr[:, :MLA_KV_RANK], kvg_ref[...]).astype(BF16), wukv_ref[...],
                 preferred_element_type=F32)
    ct, s1, s2 = ct_ref[...], s1_ref[...], s2_ref[...]
    half = MLA_ROPE // 2

    def rope(t):
        return t * ct + pltpu.roll(t, HEAD_PAD - half, 1) * s1 + pltpu.roll(t, half, 1) * s2

    k_pe = rope(ckv_kr[:, MLA_KV_RANK:])
    mla_scale = (MLA_NOPE + MLA_ROPE) ** -0.5 * LOG2E
    for hd in range(N_HEADS):
        sl = slice(hd * HEAD_PAD, (hd + 1) * HEAD_PAD)
        mq_ref[:, sl] = (rope(q[:, sl]) * mla_scale).astype(BF16)
        mk_ref[:, sl] = (kv[:, sl] + k_pe).astype(BF16)
    hp4 = N_HEADS * HEAD_PAD
    ones_pad = jnp.where((_iota((1, hp4), 1) & (HEAD_PAD - 1)) >= HEAD_DIM, 1.0, 0.0)
    mv_ref[...] = (kv[:, hp4:] + ones_pad).astype(BF16)

    lane = _iota((tm, HEAD_PAD), 1)
    keep = lane < N_HEADS

    def parts3(x):
        a = x.astype(BF16).astype(F32)
        r = x - a
        b = r.astype(BF16).astype(F32)
        c = (r - b).astype(BF16).astype(F32)
        return (jnp.where(keep, a, 0.0) + pltpu.roll(jnp.where(keep, b, 0.0), N_HEADS, 1)
                + pltpu.roll(jnp.where(keep, c, 0.0), 2 * N_HEADS, 1))

    lf = _log_sigmoid(seg(SEG_FF, N_PERM) + bf_ref[...])
    tril = (_iota((tm, tm), 0) >= _iota((tm, tm), 1)).astype(BF16)
    cum3 = jnp.dot(tril, parts3(lf).astype(BF16), preferred_element_type=F32)
    cum = (cum3 + pltpu.roll(cum3, HEAD_PAD - N_HEADS, 1) + pltpu.roll(cum3, HEAD_PAD - 2 * N_HEADS, 1)
           + carry_ref[...])
    carry_ref[...] = cum[tm - 1:tm, :]
    parts = (parts3(cum * LOG2E) + jnp.where(lane == 3 * N_HEADS, 1.0, 0.0)).astype(BF16)
    bias_q = jnp.dot(parts, selq_ref[...], preferred_element_type=F32)
    bias_k = jnp.dot(parts, selk_ref[...], preferred_element_type=F32)
    fox = seg(SEG_FOX, SEG_HG)
    fox_scale = HEAD_DIM ** -0.5 * LOG2E
    low = lane < HEAD_DIM
    for hd in range(N_HEADS):
        sl = slice(hd * HEAD_PAD, (hd + 1) * HEAD_PAD)
        src = slice((hd // 2) * HEAD_PAD, (hd // 2 + 1) * HEAD_PAD)

        def head_tile(x):
            tile = x[:, src]
            return pltpu.roll(tile, HEAD_DIM, 1) if hd % 2 else tile

        fq_ref[:, sl] = jnp.where(low, head_tile(fox[:, 0:GROUP_W]) * fox_scale, bias_q[:, sl]).astype(BF16)
        fk_ref[:, sl] = jnp.where(low, head_tile(fox[:, GROUP_W:2 * GROUP_W]), bias_k[:, sl]).astype(BF16)
        fv_ref[:, sl] = jnp.where(low, head_tile(fox[:, 2 * GROUP_W:]), 1.0).astype(BF16)

    hg_ref[...] = seg(SEG_HG, SEG_S5)
    su_ref[...] = seg(SEG_S5, SEG_FF)


def _permute_inproj(w):
    def zeros(n):
        return jnp.zeros(w.shape[:-1] + (n,), w.dtype)

    cols = [w[..., 0:384], zeros(ROPE_LANE0), w[..., 384:416], zeros(HEAD_PAD - ROPE_LANE0 - MLA_ROPE),
            w[..., 416:1184], w[..., 1188:N_IN], w[..., 1184:1188], zeros(HEAD_PAD - N_HEADS)]
    out = jnp.concatenate(cols, axis=-1)
    assert out.shape[-1] == N_PERM
    return out


def _bias_selectors():
    selq = np.zeros((HEAD_PAD, N_HEADS * HEAD_PAD), np.float32)
    selk = np.zeros((HEAD_PAD, N_HEADS * HEAD_PAD), np.float32)
    one = 3 * N_HEADS
    for hd in range(N_HEADS):
        for j in range(3):
            selq[N_HEADS * j + hd, hd * HEAD_PAD + BIAS_LANE0 + j] = 1.0
            selq[one, hd * HEAD_PAD + BIAS_LANE0 + 3 + j] = 1.0
            selk[one, hd * HEAD_PAD + BIAS_LANE0 + j] = 1.0
            selk[N_HEADS * j + hd, hd * HEAD_PAD + BIAS_LANE0 + 3 + j] = -1.0
    return jnp.asarray(selq, BF16), jnp.asarray(selk, BF16)


def _layer_block(a, layer):
    shape = a.shape[1:]
    return pl.BlockSpec((None,) + shape, lambda *_: (layer,) + (0,) * len(shape), pipeline_mode=pl.Buffered(1))


def _inproj(h, seq, layer, attn_g, w_perm, q_g, w_uq, kv_g, w_ukv, b_f, tables, tm):
    t = h.shape[0]
    wuq = jnp.pad(w_uq.reshape(MLA_Q_RANK, N_HEADS, MLA_NOPE + MLA_ROPE),
                  ((0, 0), (0, 0), (0, HEAD_PAD - MLA_NOPE - MLA_ROPE))).reshape(MLA_Q_RANK, -1).astype(BF16)
    wkv = w_ukv.reshape(MLA_KV_RANK, N_HEADS, 2 * HEAD_DIM)
    head_pad = ((0, 0), (0, 0), (0, HEAD_PAD - HEAD_DIM))
    wk = jnp.pad(wkv[:, :, :MLA_NOPE], head_pad).reshape(MLA_KV_RANK, -1)
    wv = jnp.pad(wkv[:, :, MLA_NOPE:], head_pad).reshape(MLA_KV_RANK, -1)
    wukv = jnp.concatenate([wk, wv], axis=1).astype(BF16)
    bf = jnp.zeros((1, HEAD_PAD), F32).at[0, :N_HEADS].set(b_f.astype(F32))
    selq, selk = _bias_selectors()
    ct, s1, s2 = tables
    hp4 = N_HEADS * HEAD_PAD

    def rows(width):
        return pl.BlockSpec((tm, width), lambda i: (i, 0))

    def whole(a):
        return pl.BlockSpec(a.shape, lambda i: (0,) * a.ndim)

    args = (h, attn_g.reshape(1, -1), w_perm, q_g.reshape(1, -1), wuq, kv_g.reshape(1, -1), wukv, bf,
            ct, s1, s2, selq, selk)
    in_specs = ([rows(D_MODEL), whole(args[1]), _layer_block(w_perm, layer)] + [whole(a) for a in args[3:8]]
                + [rows(HEAD_PAD)] * 3 + [whole(selq), whole(selk)])
    out_widths = (hp4,) * 6
    out_shape = tuple(jax.ShapeDtypeStruct((t, w), BF16) for w in out_widths) + (
        jax.ShapeDtypeStruct((t, 4 * GROUP_W), F32), jax.ShapeDtypeStruct((t, GROUP_W), F32))
    out_specs = tuple(rows(w) for w in out_widths) + (rows(4 * GROUP_W), rows(GROUP_W))
    vmem = _vmem_limit(_nbytes(w_perm.shape[1:], BF16), 2 * _nbytes((tm, D_MODEL), F32),
                       2 * sum(_nbytes((tm, w), BF16) for w in out_widths),
                       2 * _nbytes((tm, 5 * GROUP_W), F32), _nbytes((tm, N_PERM), F32),
                       _nbytes((tm, D_MODEL), F32))
    return pl.pallas_call(
        functools.partial(_inproj_kernel, seq // tm),
        out_shape=out_shape,
        grid=(t // tm,),
        in_specs=in_specs,
        out_specs=out_specs,
        scratch_shapes=[pltpu.VMEM((1, HEAD_PAD), F32)],
        name="inproj",
        compiler_params=pltpu.CompilerParams(dimension_semantics=("arbitrary",), vmem_limit_bytes=vmem),
    )(*args)


def _attn_kernel(tq, tkv, q_ref, k_ref, v_ref, o_ref, m_ref, acc_ref):
    qi = pl.program_id(1)
    m_ref[...] = jnp.full_like(m_ref, NEG_BIG)
    acc_ref[...] = jnp.zeros_like(acc_ref)
    per_q = tq // tkv

    def block(start, width, diag_offset):
        rows = pl.ds(pl.multiple_of(start, tkv), width)
        for hd in range(N_HEADS):
            sl = slice(hd * HEAD_PAD, (hd + 1) * HEAD_PAD)
            s = lax.dot_general(q_ref[:, sl], k_ref[rows, sl], (((1,), (1,)), ((), ())),
                                preferred_element_type=F32)
            if diag_offset is not None:
                visible = _iota(s.shape, 1) + diag_offset <= _iota(s.shape, 0)
                s = jnp.where(visible, s, NEG_BIG)
            m_old = m_ref[hd]
            m_new = jnp.maximum(m_old, jnp.max(s, axis=1, keepdims=True))
            p = jnp.exp2(s - jnp.concatenate([m_new] * (width // HEAD_PAD), axis=1))
            m_ref[hd] = m_new
            acc_ref[hd] = (jnp.exp2(m_old - m_new) * acc_ref[hd]
                           + jnp.dot(p.astype(BF16), v_ref[rows, sl], preferred_element_type=F32))

    n_full = qi * per_q

    def wide_block(jj, carry):
        block(jj * (2 * tkv), 2 * tkv, None)
        return carry

    lax.fori_loop(0, n_full // 2, wide_block, 0)

    @pl.when(n_full % 2 == 1)
    def _():
        block((n_full - 1) * tkv, tkv, None)

    for r in range(per_q):
        block((n_full + r) * tkv, tkv, r * tkv)

    low = _iota((tq, HEAD_PAD), 1) < HEAD_DIM
    for pair in range(N_HEADS // 2):
        a0, a1 = acc_ref[2 * pair], acc_ref[2 * pair + 1]
        n0 = a0 / pltpu.roll(a0, HEAD_DIM, 1)
        n1 = a1 / pltpu.roll(a1, HEAD_DIM, 1)
        o_ref[:, pair * HEAD_PAD:(pair + 1) * HEAD_PAD] = jnp.where(low, n0, pltpu.roll(n1, HEAD_DIM, 1))


def _attention(q, k, v, batch, seq, tq, tkv):
    t = q.shape[0]
    hp4 = N_HEADS * HEAD_PAD
    vmem = _vmem_limit(2 * _nbytes((seq, hp4), BF16), 2 * _nbytes((tq, hp4), BF16),
                       2 * _nbytes((tq, GROUP_W), F32), 2 * N_HEADS * _nbytes((tq, HEAD_PAD), F32),
                       6 * _nbytes((tq, 2 * tkv), F32))
    resident = pl.BlockSpec((seq, hp4), lambda b, i: (b, 0), pipeline_mode=pl.Buffered(1))
    return pl.pallas_call(
        functools.partial(_attn_kernel, tq, tkv),
        out_shape=jax.ShapeDtypeStruct((t, GROUP_W), F32),
        grid=(batch, seq // tq),
        in_specs=[pl.BlockSpec((tq, hp4), lambda b, i: (b * (seq // tq) + i, 0)), resident, resident],
        out_specs=pl.BlockSpec((tq, GROUP_W), lambda b, i: (b * (seq // tq) + i, 0)),
        scratch_shapes=[pltpu.VMEM((N_HEADS, tq, HEAD_PAD), F32), pltpu.VMEM((N_HEADS, tq, HEAD_PAD), F32)],
        name="causal_attention",
        compiler_params=pltpu.CompilerParams(dimension_semantics=("arbitrary", "arbitrary"),
                                             vmem_limit_bytes=vmem),
    )(q, k, v)


def _hgrn_kernel(win, q0_ref, q1_ref, f0_ref, f1_ref, v0_ref, v1_ref, lb_ref, ee_ref, o_ref,
                 st_ref, bc_ref, kk_ref, vs_ref, qe_ref, ke_ref, od_ref, dec_ref, vt_ref):
    @pl.when(pl.program_id(1) == 0)
    def _():
        st_ref[...] = jnp.zeros_like(st_ref)

    q_ref, f_ref, v_ref = (q0_ref, q1_ref), (f0_ref, f1_ref), (v0_ref, v1_ref)
    r = q0_ref.shape[0]
    g = r // SUB
    lb = lb_ref[...]
    log_lb = jnp.log(lb)
    log_1m = jnp.log1p(-lb)

    def slab(halves, s):
        return jnp.concatenate([h[pl.ds(s, g, stride=SUB), :] for h in halves], axis=1)

    def put_rows(ref, s, x):
        for half in range(2):
            ref[half, pl.ds(s, g, stride=SUB), :] = x[:, half * V7X_LANES:(half + 1) * V7X_LANES]

    def get_rows(ref, row0):
        return jnp.concatenate([ref[0, pl.ds(row0, SUB), :], ref[1, pl.ds(row0, SUB), :]], axis=1)

    run = None
    for s in range(SUB):
        z = slab(f_ref, s)
        b = log_1m + _log_sigmoid(z)
        log_f = jnp.maximum(log_lb, b) + jnp.log1p(jnp.exp(-jnp.abs(log_lb - b)))
        run = log_f if s == 0 else run + log_f
        bc_ref[s] = run
        kk_ref[s] = (1.0 - lb) * _sigmoid(-z)
        vs_ref[s] = slab(v_ref, s)
    total = bc_ref[SUB - 1]
    dec_ref[...] = jnp.exp(total)

    for s in range(SUB):
        qs = slab(q_ref, s)
        bcs = bc_ref[s]
        prods = [(qs * kk_ref[j] * jnp.exp(bcs - bc_ref[j])).astype(BF16) for j in range(s)]
        prods.append((qs * kk_ref[s]).astype(BF16))
        red = jnp.dot(jnp.concatenate(prods, axis=0), ee_ref[...], preferred_element_type=F32)
        od = red[0:g] * vs_ref[0]
        for j in range(1, s + 1):
            od = od + red[j * g:(j + 1) * g] * vs_ref[j]
        put_rows(od_ref, s, od)
        put_rows(qe_ref, s, qs * jnp.exp(bcs))
        put_rows(ke_ref, s, kk_ref[s] * jnp.exp(total - bcs))
    for w in range(r // win):
        for half in range(2):
            vt_ref[w, half * V7X_LANES:(half + 1) * V7X_LANES, :] = (
                v_ref[half][w * win:(w + 1) * win, :].T.astype(BF16))

    lane_head = _iota((GROUP_W, GROUP_W), 1) // HEAD_DIM
    row_head = _iota((GROUP_W, GROUP_W), 0) // HEAD_DIM
    same_head = lane_head == row_head
    per_win = win // SUB

    def window(w, carry):
        vt = vt_ref[w]
        st = st_ref[...]
        for c in range(per_win):
            row0 = pl.multiple_of(w * win + c * SUB, SUB)
            qe = get_rows(qe_ref, row0).astype(BF16)
            o_state = lax.dot_general(qe, st.astype(BF16), (((1,), (1,)), ((), ())),
                                      preferred_element_type=F32)
            o_ref[pl.ds(row0, SUB), :] = get_rows(od_ref, row0) + o_state
            pieces = [get_rows(ke_ref, row0).astype(BF16)]
            if c:
                pieces.insert(0, jnp.zeros((c * SUB, GROUP_W), BF16))
            if c < per_win - 1:
                pieces.append(jnp.zeros(((per_win - 1 - c) * SUB, GROUP_W), BF16))
            upd = jnp.dot(vt, jnp.concatenate(pieces, axis=0), preferred_element_type=F32)
            st = st * dec_ref[pl.ds(w * per_win + c, 1), :] + jnp.where(same_head, upd, 0.0)
        st_ref[...] = st
        return carry

    lax.fori_loop(0, r // win, window, 0)


def _hgrn(hg, lb, batch, seq, r):
    t = hg.shape[0]
    win = min(r, V7X_LANES)
    ee = jnp.asarray(np.kron(np.eye(N_HEADS, dtype=np.float32), np.ones((HEAD_DIM, HEAD_DIM), np.float32)), BF16)
    nblk = seq // r

    def half(c):
        return pl.BlockSpec((r, V7X_LANES), lambda b, i: (b * nblk + i, c))

    tile = _nbytes((r, GROUP_W), F32)
    vmem = _vmem_limit(8 * tile, 7 * tile, 8 * tile)
    slabs = pltpu.VMEM((SUB, r // SUB, GROUP_W), F32)
    rows = pltpu.VMEM((2, r, V7X_LANES), F32)
    return pl.pallas_call(
        functools.partial(_hgrn_kernel, win),
        out_shape=jax.ShapeDtypeStruct((t, GROUP_W), F32),
        grid=(batch, nblk),
        in_specs=[half(0), half(1), half(2), half(3), half(4), half(5),
                  pl.BlockSpec((1, GROUP_W), lambda b, i: (0, 0)),
                  pl.BlockSpec((GROUP_W, GROUP_W), lambda b, i: (0, 0))],
        out_specs=pl.BlockSpec((r, GROUP_W), lambda b, i: (b * nblk + i, 0)),
        scratch_shapes=[pltpu.VMEM((GROUP_W, GROUP_W), F32), slabs, slabs, slabs, rows, rows, rows,
                        pltpu.VMEM((r // SUB, GROUP_W), F32), pltpu.VMEM((r // win, GROUP_W, win), BF16)],
        name="hgrn2",
        compiler_params=pltpu.CompilerParams(dimension_semantics=("arbitrary", "arbitrary"),
                                             vmem_limit_bytes=vmem),
    )(hg, hg, hg, hg, hg, hg, lb.reshape(1, -1), ee)


def _s5_local_kernel(u_ref, bd_ref, bbd_ref, we_ref, d_ref, y_ref, e_ref):
    r = u_ref.shape[0]
    u = u_ref[...]
    rsub = _iota((r, GROUP_W), 0) & (SUB - 1)
    y = d_ref[...] * u
    for j in range(SUB):
        uj = jnp.where(rsub >= j, _shift_rows(u, j), 0.0).astype(BF16)
        y = y + jnp.dot(uj, bd_ref[j], preferred_element_type=F32)
    y_ref[...] = y
    n = S5_GROUPS * S5_P
    bu = jnp.dot(u.astype(BF16), bbd_ref[...], preferred_element_type=F32).reshape(r // SUB, SUB, 2 * n)
    bur, bui = bu[:, :, :n], bu[:, :, n:]
    wr, wi = we_ref[:, :n], we_ref[:, n:]
    e_ref[:, :n] = jnp.sum(wr * bur - wi * bui, axis=1)
    e_ref[:, n:] = jnp.sum(wr * bui + wi * bur, axis=1)


def _s5_scan_kernel(e_ref, a_ref, x_ref):
    nchunk = e_ref.shape[0]
    n = S5_GROUPS * S5_P
    xr, xi = e_ref[:, :n], e_ref[:, n:]
    pr, pi = a_ref[:, :n], a_ref[:, n:]
    row = _iota((nchunk, n), 0)
    k = 1
    while k < nchunk:
        sr = jnp.where(row >= k, pltpu.roll(xr, k, 0), 0.0)
        si = jnp.where(row >= k, pltpu.roll(xi, k, 0), 0.0)
        xr, xi = xr + pr * sr - pi * si, xi + pr * si + pi * sr
        pr, pi = pr * pr - pi * pi, 2.0 * pr * pi
        k *= 2
    x_ref[:, :n] = jnp.where(row >= 1, pltpu.roll(xr, 1, 0), 0.0)
    x_ref[:, n:] = jnp.where(row >= 1, pltpu.roll(xi, 1, 0), 0.0)


def _s5_out_kernel(y1_ref, x_ref, a1_ref, cbd_ref, wg_ref, bg_ref, o_ref, xs_ref):
    r = y1_ref.shape[0]
    n = S5_GROUPS * S5_P
    ar, ai = a1_ref[:, :n], a1_ref[:, n:]

    def chunk(c, carry):
        base = pl.multiple_of(c * SUB, SUB)
        xr = x_ref[pl.ds(c, 1), :n]
        xi = x_ref[pl.ds(c, 1), n:]
        xs_ref[pl.ds(base, SUB), :n] = (ar * xr - ai * xi).astype(BF16)
        xs_ref[pl.ds(base, SUB), n:] = (ar * xi + ai * xr).astype(BF16)
        return carry

    lax.fori_loop(0, r // SUB, chunk, 0)
    y = y1_ref[...] + jnp.dot(xs_ref[...], cbd_ref[...], preferred_element_type=F32)
    zact = 0.5 * y * (1.0 + jnp.tanh(math.sqrt(2.0 / math.pi) * (y + 0.044715 * (y * y * y))))
    gate = jnp.dot(zact.astype(BF16), wg_ref[...], preferred_element_type=F32) + bg_ref[...]
    o_ref[...] = zact * _sigmoid(gate)


def _s5_prep_kernel(ar_ref, ai_ref, bre_ref, bim_ref, cre_ref, cim_ref,
                    bd_ref, bbd_ref, cbd_ref, wend_ref, ain_ref, achunk_ref):
    n = S5_GROUPS * S5_P
    hi = lax.Precision.HIGHEST
    ar, ai = ar_ref[...], ai_ref[...]
    bre, bim, cre, cim = bre_ref[...], bim_ref[...], cre_ref[...], cim_ref[...]
    bbd_ref[:, :n] = bre.astype(BF16)
    bbd_ref[:, n:] = bim.astype(BF16)
    cbd_ref[:n, :] = cre.astype(BF16)
    cbd_ref[n:, :] = (-cim).astype(BF16)
    pr, pi = jnp.ones_like(ar), jnp.zeros_like(ar)
    for j in range(SUB):
        k_lag = (jnp.dot(bre * pr - bim * pi, cre, precision=hi, preferred_element_type=F32)
                 - jnp.dot(bre * pi + bim * pr, cim, precision=hi, preferred_element_type=F32))
        bd_ref[j] = k_lag.astype(BF16)
        wend_ref[SUB - 1 - j:SUB - j, :n] = pr
        wend_ref[SUB - 1 - j:SUB - j, n:] = pi
        pr, pi = pr * ar - pi * ai, pr * ai + pi * ar
        ain_ref[j:j + 1, :n] = pr
        ain_ref[j:j + 1, n:] = pi
    achunk_ref[:, :n] = pr
    achunk_ref[:, n:] = pi


def _s5_operators(lam_re, lam_im, log_step, b_re, b_im, c_re, c_im):
    step = jnp.exp(log_step.astype(F32))[:, None]
    lre = jnp.minimum(lam_re.astype(F32), -1e-4)
    lim = lam_im.astype(F32)
    mag = jnp.exp(lre * step)
    a_re, a_im = mag * jnp.cos(lim * step), mag * jnp.sin(lim * step)
    den = lre * lre + lim * lim
    coef_re = ((a_re - 1.0) * lre + a_im * lim) / den
    coef_im = (a_im * lre - (a_re - 1.0) * lim) / den
    br, bi = b_re.astype(F32), b_im.astype(F32)
    bb_re = coef_re[..., None] * br - coef_im[..., None] * bi
    bb_im = coef_re[..., None] * bi + coef_im[..., None] * br
    cr, ci = c_re.astype(F32), c_im.astype(F32)
    n = S5_GROUPS * S5_P
    same_group = (np.arange(GROUP_W)[:, None] // S5_CH) == (np.arange(n)[None, :] // S5_P)
    in_mask = jnp.asarray(same_group, F32)
    out_mask = jnp.asarray(same_group.T, F32)

    def in_side(b):
        return jnp.tile(b.transpose(0, 2, 1).reshape(GROUP_W, S5_P), (1, S5_GROUPS)) * in_mask

    def out_side(c):
        return jnp.tile(c.transpose(0, 2, 1).reshape(n, S5_CH), (1, S5_GROUPS)) * out_mask

    args = (a_re.reshape(1, n), a_im.reshape(1, n), in_side(bb_re), in_side(bb_im), out_side(cr), out_side(ci))
    out_shape = (jax.ShapeDtypeStruct((SUB, GROUP_W, GROUP_W), BF16),
                 jax.ShapeDtypeStruct((GROUP_W, 2 * n), BF16),
                 jax.ShapeDtypeStruct((2 * n, GROUP_W), BF16),
                 jax.ShapeDtypeStruct((SUB, 2 * n), F32),
                 jax.ShapeDtypeStruct((SUB, 2 * n), F32),
                 jax.ShapeDtypeStruct((1, 2 * n), F32))
    return pl.pallas_call(
        _s5_prep_kernel,
        out_shape=out_shape,
        name="s5_prep",
        compiler_params=pltpu.CompilerParams(
            vmem_limit_bytes=_vmem_limit(4 * _nbytes((GROUP_W, n), F32), 10 * _nbytes((GROUP_W, n), F32),
                                         2 * _nbytes((SUB, GROUP_W, GROUP_W), BF16))),
    )(*args)


def _s5(u, ops, d_skip, w_glu, b_glu, batch, seq, r):
    t = u.shape[0]
    bd, bbd, cbd, w_end, a_in, a_chunk = ops
    n2 = 2 * S5_GROUPS * S5_P
    nck = seq // SUB

    def whole(a):
        return pl.BlockSpec(a.shape, lambda *_: (0,) * a.ndim)

    rows = pl.BlockSpec((r, GROUP_W), lambda i: (i, 0))
    crow = pl.BlockSpec((r // SUB, n2), lambda i: (i, 0))
    d2 = d_skip.astype(F32).reshape(1, -1)
    y1, e = pl.pallas_call(
        _s5_local_kernel,
        out_shape=(jax.ShapeDtypeStruct((t, GROUP_W), F32), jax.ShapeDtypeStruct((t // SUB, n2), F32)),
        grid=(t // r,),
        in_specs=[rows, whole(bd), whole(bbd), whole(w_end), whole(d2)],
        out_specs=(rows, crow),
        name="s5_local",
        compiler_params=pltpu.CompilerParams(
            dimension_semantics=("arbitrary",),
            vmem_limit_bytes=_vmem_limit(2 * _nbytes(bd.shape, BF16), 2 * _nbytes(bbd.shape, BF16),
                                         6 * _nbytes((r, n2), F32), 8 * _nbytes((r, GROUP_W), F32))),
    )(u, bd, bbd, w_end, d2)
    xin = pl.pallas_call(
        _s5_scan_kernel,
        out_shape=jax.ShapeDtypeStruct((t // SUB, n2), F32),
        grid=(batch,),
        in_specs=[pl.BlockSpec((nck, n2), lambda b: (b, 0)), whole(a_chunk)],
        out_specs=pl.BlockSpec((nck, n2), lambda b: (b, 0)),
        name="s5_scan",
        compiler_params=pltpu.CompilerParams(
            dimension_semantics=("arbitrary",), vmem_limit_bytes=_vmem_limit(10 * _nbytes((nck, n2), F32))),
    )(e, a_chunk)
    wg = w_glu.astype(BF16)
    bg = b_glu.astype(F32).reshape(1, -1)
    return pl.pallas_call(
        _s5_out_kernel,
        out_shape=jax.ShapeDtypeStruct((t, GROUP_W), F32),
        grid=(t // r,),
        in_specs=[rows, crow, whole(a_in), whole(cbd), whole(wg), whole(bg)],
        out_specs=rows,
        scratch_shapes=[pltpu.VMEM((r, n2), BF16)],
        name="s5_out",
        compiler_params=pltpu.CompilerParams(
            dimension_semantics=("arbitrary",),
            vmem_limit_bytes=_vmem_limit(2 * _nbytes(cbd.shape, BF16), 3 * _nbytes((r, n2), BF16),
                                         10 * _nbytes((r, GROUP_W), F32))),
    )(y1, xin, a_in, cbd, wg, bg)


HALO = 2 * V7X_SUBLANES


def _post_kernel(tiles_per_seq, final, ya_ref, yb_ref, yc_ref, gate_ref, yd_ref, h_ref, p_ref,
                 gn_ref, wo_ref, fg_ref, wu_ref, cw_ref, cb_ref, wd_ref, pg_ref, wpg_ref, wp_ref, ng_ref,
                 o_ref, xn_ref, up_ref, act_ref):
    tm = h_ref.shape[0]
    gw = GROUP_W

    @pl.when(pl.program_id(0) % tiles_per_seq == 0)
    def _():
        xn_ref[0:HALO, :] = jnp.zeros((HALO, D_MODEL), BF16)

    parts = (_rms(ya_ref[...], gn_ref[:, 0:gw]),
             _rms(yb_ref[...], gn_ref[:, gw:2 * gw]),
             _rms(yc_ref[...], gn_ref[:, 2 * gw:3 * gw]) * _sigmoid(gate_ref[...]),
             _rms(yd_ref[...], gn_ref[:, 3 * gw:4 * gw]))
    h1 = h_ref[...]
    for g, part in enumerate(parts):
        h1 = h1 + jnp.dot(part.astype(BF16), wo_ref[g * gw:(g + 1) * gw, :], preferred_element_type=F32)
    o_ref[...] = h1
    xn_ref[HALO:, :] = _rms(h1, fg_ref[...]).astype(BF16)

    nchunk = D_FF // MXU_TILE

    def conv(slot, c):
        cols = slice(c * MXU_TILE, (c + 1) * MXU_TILE)
        up_ref[slot] = jnp.dot(xn_ref[...], wu_ref[:, cols], preferred_element_type=F32)
        return (cb_ref[:, cols] + cw_ref[0:1, cols] * up_ref[slot, pl.ds(HALO - 2, tm), :]
                + cw_ref[1:2, cols] * up_ref[slot, pl.ds(HALO - 1, tm), :]
                + cw_ref[2:3, cols] * up_ref[slot, pl.ds(HALO, tm), :])

    for c in range(nchunk):
        slot = 2 * (c % 2)
        gate = conv(slot, c)
        val = conv(slot + 1, c + nchunk)
        act_ref[:, c * MXU_TILE:(c + 1) * MXU_TILE] = (gate * _sigmoid(gate) * val).astype(BF16)
    xn_ref[0:HALO, :] = xn_ref[tm:tm + HALO, :]
    h2 = o_ref[...] + jnp.dot(act_ref[...], wd_ref[...], preferred_element_type=F32)

    pgate = _sigmoid(jnp.dot(_rms(h2, pg_ref[...]).astype(BF16), wpg_ref[...], preferred_element_type=F32))
    out = h2 + pgate * jnp.dot(p_ref[...].astype(BF16), wp_ref[...], preferred_element_type=F32)
    o_ref[...] = _rms(out, ng_ref[...]) if final else out


def _post(ya, yb, yc, hg, yd, h, p, seq, layer, stacked, final, tm):
    t = h.shape[0]
    rows = pl.BlockSpec((tm, GROUP_W), lambda i: (i, 0))
    wide = pl.BlockSpec((tm, D_MODEL), lambda i: (i, 0))
    consts = tuple(stacked)
    vmem = _vmem_limit(sum(_nbytes(c.shape[1:], c.dtype) for c in consts), 12 * _nbytes((tm, GROUP_W), F32),
                       4 * _nbytes((tm, D_MODEL), F32), _nbytes((tm + HALO, D_MODEL), BF16),
                       4 * _nbytes((tm + HALO, MXU_TILE), F32), _nbytes((tm, D_FF), BF16),
                       3 * _nbytes((tm, D_MODEL), F32))
    return pl.pallas_call(
        functools.partial(_post_kernel, seq // tm, final),
        out_shape=jax.ShapeDtypeStruct((t, D_MODEL), F32),
        grid=(t // tm,),
        in_specs=[rows, rows, rows, pl.BlockSpec((tm, GROUP_W), lambda i: (i, 3)), rows, wide,
                  pl.BlockSpec((tm, PLE_DIM), lambda i: (layer * (t // tm) + i, 0))]
        + [_layer_block(c, layer) for c in consts],
        out_specs=wide,
        scratch_shapes=[pltpu.VMEM((tm + HALO, D_MODEL), BF16), pltpu.VMEM((4, tm + HALO, MXU_TILE), F32),
                        pltpu.VMEM((tm, D_FF), BF16)],
        name="post_mixer",
        compiler_params=pltpu.CompilerParams(dimension_semantics=("arbitrary",), vmem_limit_bytes=vmem),
    )(ya, yb, yc, hg, yd, h, p, *consts)


def _tiles(seq):
    tm = min(512, seq)
    return dict(tm=tm, tq=min(512, seq), tkv=min(512, seq), r=min(512, seq))


def kernel(x, p, positions, attn_norm_g, w_in, mla_q_norm_g, mla_w_uq, mla_kv_norm_g, mla_w_ukv, fox_b_f,
           hgrn_lb_param, s5_lam_re, s5_lam_im, s5_log_step, s5_b_re, s5_b_im, s5_c_re, s5_c_im, s5_d,
           s5_w_glu, s5_b_glu, group_norm_g, w_out, ffn_norm_g, w_up, conv_w, conv_b, w_down, ple_norm_g,
           w_ple_gate, w_ple, final_norm_g):
    batch, seq, _ = x.shape
    depth = w_in.shape[0]
    t = batch * seq
    ts = _tiles(seq)
    assert seq % ts["tm"] == 0 and seq % ts["tq"] == 0 and ts["tq"] % ts["tkv"] == 0 and seq % ts["r"] == 0
    assert (seq // SUB) & (seq // SUB - 1) == 0, "chunk scan assumes a power-of-two chunk count"

    lb_all = jnp.cumsum(jax.nn.softmax(hgrn_lb_param.astype(F32), axis=0), axis=0)
    lb_all = lb_all - lb_all[0:1]
    tables = _rope_tables(positions, ts["tm"])

    def row(v):
        return v.reshape(v.shape[0], 1, -1)

    w_perm = _permute_inproj(w_in.astype(BF16))
    post_params = (row(group_norm_g), w_out.astype(BF16), row(ffn_norm_g), w_up.astype(BF16), conv_w,
                   row(conv_b), w_down.astype(BF16), row(ple_norm_g), w_ple_gate.astype(BF16),
                   w_ple.astype(BF16), jnp.broadcast_to(final_norm_g.reshape(1, 1, -1), (depth, 1, D_MODEL)))
    h = x.reshape(t, D_MODEL)
    for i in range(depth):
        mq, mk, mv, fq, fk, fv, hg, su = _inproj(
            h, seq, i, attn_norm_g[i], w_perm, mla_q_norm_g[i], mla_w_uq[i], mla_kv_norm_g[i], mla_w_ukv[i],
            fox_b_f[i], tables, ts["tm"])
        y_a = _attention(mq, mk, mv, batch, seq, ts["tq"], ts["tkv"])
        y_b = _attention(fq, fk, fv, batch, seq, ts["tq"], ts["tkv"])
        y_c = _hgrn(hg, lb_all[i], batch, seq, ts["r"])
        ops = _s5_operators(s5_lam_re[i], s5_lam_im[i], s5_log_step[i], s5_b_re[i], s5_b_im[i],
                            s5_c_re[i], s5_c_im[i])
        y_d = _s5(su, ops, s5_d[i], s5_w_glu[i], s5_b_glu[i], batch, seq, ts["r"])
        h = _post(y_a, y_b, y_c, hg, y_d, h, p.reshape(depth * t, PLE_DIM), seq, i, post_params,
                  i == depth - 1, ts["tm"])
    return h.reshape(batch, seq, D_MODEL)
```

```python
import functools
import math

import numpy as np
import jax
import jax.numpy as jnp
from jax import lax
from jax.experimental import pallas as pl
from jax.experimental.pallas import tpu as pltpu

F32 = jnp.float32
BF16 = jnp.bfloat16

D_MODEL = 1024
N_HEADS = 4
HEAD_DIM = 64
GROUP_W = 256
MLA_Q_RANK = 256
MLA_KV_RANK = 128
MLA_NOPE = 64
MLA_ROPE = 32
ROPE_THETA = 10000.0
S5_GROUPS = 16
S5_CH = 16
S5_P = 64
D_FF = 2816
PLE_DIM = 256
EPS = 1e-6
N_IN = 2468

V7X_LANES = 128
V7X_SUBLANES = 8
V7X_VMEM_BYTES = 64 * 1024 * 1024
MXU_TILE = 256
VMEM_CAP_BYTES = 58 * 1024 * 1024

HEAD_PAD = V7X_LANES
SUB = 16
NEG_BIG = -1e30

SEG_CQ = 0
SEG_CKV = 256
SEG_KR = 384
SEG_FOX = 512
SEG_HG = 1280
SEG_S5 = 2304
SEG_FF = 2560
N_PERM = 2688
ROPE_LANE0 = 64
BIAS_LANE0 = 64
LOG2E = math.log2(math.e)


def _vmem_limit(*byte_counts):
    need = int(sum(byte_counts))
    return int(min(VMEM_CAP_BYTES, need + need // 4 + (4 << 20)))


def _nbytes(shape, dtype):
    return int(np.prod(shape)) * jnp.dtype(dtype).itemsize


def _rms(x, g):
    return x * lax.rsqrt(jnp.mean(x * x, axis=-1, keepdims=True) + EPS) * g


def _log_sigmoid(z):
    return jnp.minimum(z, 0.0) - jnp.log1p(jnp.exp(-jnp.abs(z)))


def _sigmoid(z):
    return 1.0 / (1.0 + jnp.exp(-z))


def _iota(shape, dim):
    return lax.broadcasted_iota(jnp.int32, shape, dim)


def _shift_rows(x, k):
    if k == 0:
        return x
    return pltpu.roll(x, k, 0)


def _rope_kernel(pos_ref, freq_ref, ct_ref, s1_ref, s2_ref):
    ang = pos_ref[...].astype(F32) * freq_ref[...]
    lane = _iota(ang.shape, 1)
    half = MLA_ROPE // 2
    sin = jnp.sin(ang)
    ct_ref[...] = jnp.cos(ang)
    s1_ref[...] = jnp.where((lane >= ROPE_LANE0) & (lane < ROPE_LANE0 + half), -sin, 0.0)
    s2_ref[...] = jnp.where((lane >= ROPE_LANE0 + half) & (lane < ROPE_LANE0 + 2 * half), sin, 0.0)


def _rope_tables(positions, tm):
    t = positions.size
    half = MLA_ROPE // 2
    inv_freq = ROPE_THETA ** (-jnp.arange(half, dtype=F32) / half)
    freq = jnp.zeros((1, HEAD_PAD), F32).at[0, ROPE_LANE0:ROPE_LANE0 + 2 * half].set(jnp.tile(inv_freq, 2))
    pos = positions.reshape(t, 1)
    out = jax.ShapeDtypeStruct((t, HEAD_PAD), F32)
    spec = pl.BlockSpec((tm, HEAD_PAD), lambda i: (i, 0))
    return pl.pallas_call(
        _rope_kernel,
        out_shape=(out, out, out),
        grid=(t // tm,),
        in_specs=[pl.BlockSpec((tm, 1), lambda i: (i, 0)), pl.BlockSpec((1, HEAD_PAD), lambda i: (0, 0))],
        out_specs=(spec, spec, spec),
        name="rope_tables",
        compiler_params=pltpu.CompilerParams(dimension_semantics=("arbitrary",)),
    )(pos, freq)


def _inproj_kernel(tiles_per_seq, h_ref, g_ref, w_ref, qg_ref, wuq_ref, kvg_ref, wukv_ref, bf_ref,
                   ct_ref, s1_ref, s2_ref, selq_ref, selk_ref,
                   mq_ref, mk_ref, mv_ref, fq_ref, fk_ref, fv_ref, hg_ref, su_ref, carry_ref):
    i = pl.program_id(0)

    @pl.when(i % tiles_per_seq == 0)
    def _():
        carry_ref[...] = jnp.zeros_like(carry_ref)

    tm = h_ref.shape[0]
    xn = _rms(h_ref[...], g_ref[...]).astype(BF16)

    proj = jnp.dot(xn, w_ref[...], preferred_element_type=F32)

    def seg(a, b):
        return proj[:, a:b]

    q = jnp.dot(_rms(seg(SEG_CQ, SEG_CKV), qg_ref[...]).astype(BF16), wuq_ref[...],
                preferred_element_type=F32)
    ckv_kr = seg(SEG_CKV, SEG_FOX)
    kv = jnp.dot(_rms(ckv_kr[:, :MLA_KV_RANK], kvg_ref[...]).astype(BF16), wukv_ref[...],
                 preferred_element_type=F32)
    ct, s1, s2 = ct_ref[...], s1_ref[...], s2_ref[...]
    half = MLA_ROPE // 2

    def rope(t):
        return t * ct + pltpu.roll(t, HEAD_PAD - half, 1) * s1 + pltpu.roll(t, half, 1) * s2

    k_pe = rope(ckv_kr[:, MLA_KV_RANK:])
    mla_scale = (MLA_NOPE + MLA_ROPE) ** -0.5 * LOG2E
    for hd in range(N_HEADS):
        sl = slice(hd * HEAD_PAD, (hd + 1) * HEAD_PAD)
        mq_ref[:, sl] = (rope(q[:, sl]) * mla_scale).astype(BF16)
        mk_ref[:, sl] = (kv[:, sl] + k_pe).astype(BF16)
    hp4 = N_HEADS * HEAD_PAD
    ones_pad = jnp.where((_iota((1, hp4), 1) & (HEAD_PAD - 1)) >= HEAD_DIM, 1.0, 0.0)
    mv_ref[...] = (kv[:, hp4:] + ones_pad).astype(BF16)

    lane = _iota((tm, HEAD_PAD), 1)
    keep = lane < N_HEADS

    def parts3(x):
        a = x.astype(BF16).astype(F32)
        r = x - a
        b = r.astype(BF16).astype(F32)
        c = (r - b).astype(BF16).astype(F32)
        return (jnp.where(keep, a, 0.0) + pltpu.roll(jnp.where(keep, b, 0.0), N_HEADS, 1)
                + pltpu.roll(jnp.where(keep, c, 0.0), 2 * N_HEADS, 1))

    lf = _log_sigmoid(seg(SEG_FF, N_PERM) + bf_ref[...])
    tril = (_iota((tm, tm), 0) >= _iota((tm, tm), 1)).astype(BF16)
    cum3 = jnp.dot(tril, parts3(lf).astype(BF16), preferred_element_type=F32)
    cum = (cum3 + pltpu.roll(cum3, HEAD_PAD - N_HEADS, 1) + pltpu.roll(cum3, HEAD_PAD - 2 * N_HEADS, 1)
           + carry_ref[...])
    carry_ref[...] = cum[tm - 1:tm, :]
    parts = (parts3(cum * LOG2E) + jnp.where(lane == 3 * N_HEADS, 1.0, 0.0)).astype(BF16)
    bias_q = jnp.dot(parts, selq_ref[...], preferred_element_type=F32)
    bias_k = jnp.dot(parts, selk_ref[...], preferred_element_type=F32)
    fox = seg(SEG_FOX, SEG_HG)
    fox_scale = HEAD_DIM ** -0.5 * LOG2E
    low = lane < HEAD_DIM
    for hd in range(N_HEADS):
        sl = slice(hd * HEAD_PAD, (hd + 1) * HEAD_PAD)
        src = slice((hd // 2) * HEAD_PAD, (hd // 2 + 1) * HEAD_PAD)

        def head_tile(x):
            tile = x[:, src]
            return pltpu.roll(tile, HEAD_DIM, 1) if hd % 2 else tile

        fq_ref[:, sl] = jnp.where(low, head_tile(fox[:, 0:GROUP_W]) * fox_scale, bias_q[:, sl]).astype(BF16)
        fk_ref[:, sl] = jnp.where(low, head_tile(fox[:, GROUP_W:2 * GROUP_W]), bias_k[:, sl]).astype(BF16)
        fv_ref[:, sl] = jnp.where(low, head_tile(fox[:, 2 * GROUP_W:]), 1.0).astype(BF16)

    hg_ref[...] = seg(SEG_HG, SEG_S5)
    su_ref[...] = seg(SEG_S5, SEG_FF)


def _permute_inproj(w):
    def zeros(n):
        return jnp.zeros(w.shape[:-1] + (n,), w.dtype)

    cols = [w[..., 0:384], zeros(ROPE_LANE0), w[..., 384:416], zeros(HEAD_PAD - ROPE_LANE0 - MLA_ROPE),
            w[..., 416:1184], w[..., 1188:N_IN], w[..., 1184:1188], zeros(HEAD_PAD - N_HEADS)]
    out = jnp.concatenate(cols, axis=-1)
    assert out.shape[-1] == N_PERM
    return out


def _bias_selectors():
    selq = np.zeros((HEAD_PAD, N_HEADS * HEAD_PAD), np.float32)
    selk = np.zeros((HEAD_PAD, N_HEADS * HEAD_PAD), np.float32)
    one = 3 * N_HEADS
    for hd in range(N_HEADS):
        for j in range(3):
            selq[N_HEADS * j + hd, hd * HEAD_PAD + BIAS_LANE0 + j] = 1.0
            selq[one, hd * HEAD_PAD + BIAS_LANE0 + 3 + j] = 1.0
            selk[one, hd * HEAD_PAD + BIAS_LANE0 + j] = 1.0
            selk[N_HEADS * j + hd, hd * HEAD_PAD + BIAS_LANE0 + 3 + j] = -1.0
    return jnp.asarray(selq, BF16), jnp.asarray(selk, BF16)


def _layer_block(a, layer):
    shape = a.shape[1:]
    return pl.BlockSpec((None,) + shape, lambda *_: (layer,) + (0,) * len(shape), pipeline_mode=pl.Buffered(1))


def _inproj_params(attn_g, w_in, q_g, w_uq, kv_g, w_ukv, b_f):
    depth = w_in.shape[0]
    wuq = jnp.pad(w_uq.reshape(depth, MLA_Q_RANK, N_HEADS, MLA_NOPE + MLA_ROPE),
                  ((0, 0), (0, 0), (0, 0), (0, HEAD_PAD - MLA_NOPE - MLA_ROPE)))
    wkv = w_ukv.reshape(depth, MLA_KV_RANK, N_HEADS, 2 * HEAD_DIM)
    head_pad = ((0, 0), (0, 0), (0, 0), (0, HEAD_PAD - HEAD_DIM))
    wk = jnp.pad(wkv[..., :MLA_NOPE], head_pad).reshape(depth, MLA_KV_RANK, -1)
    wv = jnp.pad(wkv[..., MLA_NOPE:], head_pad).reshape(depth, MLA_KV_RANK, -1)
    bf = jnp.pad(b_f.astype(F32), ((0, 0), (0, HEAD_PAD - N_HEADS)))

    def row(v):
        return v.reshape(depth, 1, -1)

    return (row(attn_g), _permute_inproj(w_in.astype(BF16)), row(q_g),
            wuq.reshape(depth, MLA_Q_RANK, -1).astype(BF16), row(kv_g),
            jnp.concatenate([wk, wv], axis=2).astype(BF16), row(bf))


def _inproj(h, seq, layer, stacked, tables, tm):
    t = h.shape[0]
    selq, selk = _bias_selectors()
    ct, s1, s2 = tables
    hp4 = N_HEADS * HEAD_PAD
    w_perm = stacked[1]

    def rows(width):
        return pl.BlockSpec((tm, width), lambda i: (i, 0))

    def whole(a):
        return pl.BlockSpec(a.shape, lambda i: (0,) * a.ndim)

    args = (h,) + tuple(stacked) + (ct, s1, s2, selq, selk)
    in_specs = ([rows(D_MODEL)] + [_layer_block(a, layer) for a in stacked]
                + [rows(HEAD_PAD)] * 3 + [whole(selq), whole(selk)])
    out_widths = (hp4,) * 6
    out_shape = tuple(jax.ShapeDtypeStruct((t, w), BF16) for w in out_widths) + (
        jax.ShapeDtypeStruct((t, 4 * GROUP_W), F32), jax.ShapeDtypeStruct((t, GROUP_W), F32))
    out_specs = tuple(rows(w) for w in out_widths) + (rows(4 * GROUP_W), rows(GROUP_W))
    vmem = _vmem_limit(_nbytes(w_perm.shape[1:], BF16), 2 * _nbytes((tm, D_MODEL), F32),
                       2 * sum(_nbytes((tm, w), BF16) for w in out_widths),
                       2 * _nbytes((tm, 5 * GROUP_W), F32), _nbytes((tm, N_PERM), F32),
                       _nbytes((tm, D_MODEL), F32))
    return pl.pallas_call(
        functools.partial(_inproj_kernel, seq // tm),
        out_shape=out_shape,
        grid=(t // tm,),
        in_specs=in_specs,
        out_specs=out_specs,
        scratch_shapes=[pltpu.VMEM((1, HEAD_PAD), F32)],
        name="inproj",
        compiler_params=pltpu.CompilerParams(dimension_semantics=("arbitrary",), vmem_limit_bytes=vmem),
    )(*args)


def _attn_kernel(tq, tkv, q_ref, k_ref, v_ref, o_ref, m_ref, acc_ref):
    qi = pl.program_id(1)
    m_ref[...] = jnp.full_like(m_ref, NEG_BIG)
    acc_ref[...] = jnp.zeros_like(acc_ref)
    per_q = tq // tkv

    def block(start, width, diag_offset):
        rows = pl.ds(pl.multiple_of(start, tkv), width)
        heads = [slice(hd * HEAD_PAD, (hd + 1) * HEAD_PAD) for hd in range(N_HEADS)]
        logits = [lax.dot_general(q_ref[:, sl], k_ref[rows, sl], (((1,), (1,)), ((), ())),
                                  preferred_element_type=F32) for sl in heads]
        probs, rescale = [], []
        for hd in range(N_HEADS):
            s = logits[hd]
            if diag_offset is not None:
                visible = _iota(s.shape, 1) + diag_offset <= _iota(s.shape, 0)
                s = jnp.where(visible, s, NEG_BIG)
            m_old = m_ref[hd]
            m_new = jnp.maximum(m_old, jnp.max(s, axis=1, keepdims=True))
            probs.append(jnp.exp2(s - jnp.concatenate([m_new] * (width // HEAD_PAD), axis=1)).astype(BF16))
            rescale.append(jnp.exp2(m_old - m_new))
            m_ref[hd] = m_new
        for hd, sl in enumerate(heads):
            acc_ref[hd] = (rescale[hd] * acc_ref[hd]
                           + jnp.dot(probs[hd], v_ref[rows, sl], preferred_element_type=F32))

    n_full = qi * per_q

    def wide_block(jj, carry):
        block(jj * (2 * tkv), 2 * tkv, None)
        return carry

    lax.fori_loop(0, n_full // 2, wide_block, 0)

    @pl.when(n_full % 2 == 1)
    def _():
        block((n_full - 1) * tkv, tkv, None)

    for r in range(per_q):
        block((n_full + r) * tkv, tkv, r * tkv)

    low = _iota((tq, HEAD_PAD), 1) < HEAD_DIM
    for pair in range(N_HEADS // 2):
        a0, a1 = acc_ref[2 * pair], acc_ref[2 * pair + 1]
        n0 = a0 / pltpu.roll(a0, HEAD_DIM, 1)
        n1 = a1 / pltpu.roll(a1, HEAD_DIM, 1)
        o_ref[:, pair * HEAD_PAD:(pair + 1) * HEAD_PAD] = jnp.where(low, n0, pltpu.roll(n1, HEAD_DIM, 1))


def _attention(q, k, v, batch, seq, tq, tkv):
    t = q.shape[0]
    hp4 = N_HEADS * HEAD_PAD
    vmem = _vmem_limit(2 * _nbytes((seq, hp4), BF16), 2 * _nbytes((tq, hp4), BF16),
                       2 * _nbytes((tq, GROUP_W), F32), 2 * N_HEADS * _nbytes((tq, HEAD_PAD), F32),
                       6 * _nbytes((tq, 2 * tkv), F32))
    resident = pl.BlockSpec((seq, hp4), lambda b, i: (b, 0), pipeline_mode=pl.Buffered(1))
    return pl.pallas_call(
        functools.partial(_attn_kernel, tq, tkv),
        out_shape=jax.ShapeDtypeStruct((t, GROUP_W), F32),
        grid=(batch, seq // tq),
        in_specs=[pl.BlockSpec((tq, hp4), lambda b, i: (b * (seq // tq) + i, 0)), resident, resident],
        out_specs=pl.BlockSpec((tq, GROUP_W), lambda b, i: (b * (seq // tq) + i, 0)),
        scratch_shapes=[pltpu.VMEM((N_HEADS, tq, HEAD_PAD), F32), pltpu.VMEM((N_HEADS, tq, HEAD_PAD), F32)],
        name="causal_attention",
        compiler_params=pltpu.CompilerParams(dimension_semantics=("arbitrary", "arbitrary"),
                                             vmem_limit_bytes=vmem),
    )(q, k, v)


def _hgrn_kernel(win, q0_ref, q1_ref, f0_ref, f1_ref, v0_ref, v1_ref, lb_ref, ee_ref, o_ref,
                 st_ref, bc_ref, kk_ref, vs_ref, qe_ref, ke_ref, od_ref, dec_ref, vt_ref):
    @pl.when(pl.program_id(1) == 0)
    def _():
        st_ref[...] = jnp.zeros_like(st_ref)

    q_ref, f_ref, v_ref = (q0_ref, q1_ref), (f0_ref, f1_ref), (v0_ref, v1_ref)
    r = q0_ref.shape[0]
    g = r // SUB
    lb = lb_ref[...]
    log_lb = jnp.log(lb)
    log_1m = jnp.log1p(-lb)

    def slab(halves, s):
        return jnp.concatenate([h[pl.ds(s, g, stride=SUB), :] for h in halves], axis=1)

    def put_rows(ref, s, x):
        for half in range(2):
            ref[half, pl.ds(s, g, stride=SUB), :] = x[:, half * V7X_LANES:(half + 1) * V7X_LANES]

    def get_rows(ref, row0):
        return jnp.concatenate([ref[0, pl.ds(row0, SUB), :], ref[1, pl.ds(row0, SUB), :]], axis=1)

    run = None
    for s in range(SUB):
        z = slab(f_ref, s)
        b = log_1m + _log_sigmoid(z)
        log_f = jnp.maximum(log_lb, b) + jnp.log1p(jnp.exp(-jnp.abs(log_lb - b)))
        run = log_f if s == 0 else run + log_f
        bc_ref[s] = run
        kk_ref[s] = (1.0 - lb) * _sigmoid(-z)
        vs_ref[s] = slab(v_ref, s)
    total = bc_ref[SUB - 1]
    dec_ref[...] = jnp.exp(total)

    for s in range(SUB):
        qs = slab(q_ref, s)
        bcs = bc_ref[s]
        prods = [(qs * kk_ref[j] * jnp.exp(bcs - bc_ref[j])).astype(BF16) for j in range(s)]
        prods.append((qs * kk_ref[s]).astype(BF16))
        red = jnp.dot(jnp.concatenate(prods, axis=0), ee_ref[...], preferred_element_type=F32)
        od = red[0:g] * vs_ref[0]
        for j in range(1, s + 1):
            od = od + red[j * g:(j + 1) * g] * vs_ref[j]
        put_rows(od_ref, s, od)
        put_rows(qe_ref, s, qs * jnp.exp(bcs))
        put_rows(ke_ref, s, kk_ref[s] * jnp.exp(total - bcs))
    for w in range(r // win):
        for half in range(2):
            vt_ref[w, half * V7X_LANES:(half + 1) * V7X_LANES, :] = (
                v_ref[half][w * win:(w + 1) * win, :].T.astype(BF16))

    lane_head = _iota((GROUP_W, GROUP_W), 1) // HEAD_DIM
    row_head = _iota((GROUP_W, GROUP_W), 0) // HEAD_DIM
    same_head = lane_head == row_head
    per_win = win // SUB

    def window(w, carry):
        vt = vt_ref[w]
        st = st_ref[...]
        for c in range(per_win):
            row0 = pl.multiple_of(w * win + c * SUB, SUB)
            qe = get_rows(qe_ref, row0).astype(BF16)
            o_state = lax.dot_general(qe, st.astype(BF16), (((1,), (1,)), ((), ())),
                                      preferred_element_type=F32)
            o_ref[pl.ds(row0, SUB), :] = get_rows(od_ref, row0) + o_state
            pieces = [get_rows(ke_ref, row0).astype(BF16)]
            if c:
                pieces.insert(0, jnp.zeros((c * SUB, GROUP_W), BF16))
            if c < per_win - 1:
                pieces.append(jnp.zeros(((per_win - 1 - c) * SUB, GROUP_W), BF16))
            upd = jnp.dot(vt, jnp.concatenate(pieces, axis=0), preferred_element_type=F32)
            st = st * dec_ref[pl.ds(w * per_win + c, 1), :] + jnp.where(same_head, upd, 0.0)
        st_ref[...] = st
        return carry

    lax.fori_loop(0, r // win, window, 0)


def _hgrn(hg, layer, lb, batch, seq, r):
    t = hg.shape[0]
    win = min(r, V7X_LANES)
    ee = jnp.asarray(np.kron(np.eye(N_HEADS, dtype=np.float32), np.ones((HEAD_DIM, HEAD_DIM), np.float32)), BF16)
    nblk = seq // r

    def half(c):
        return pl.BlockSpec((r, V7X_LANES), lambda b, i: (b * nblk + i, c))

    tile = _nbytes((r, GROUP_W), F32)
    vmem = _vmem_limit(8 * tile, 7 * tile, 8 * tile)
    slabs = pltpu.VMEM((SUB, r // SUB, GROUP_W), F32)
    rows = pltpu.VMEM((2, r, V7X_LANES), F32)
    return pl.pallas_call(
        functools.partial(_hgrn_kernel, win),
        out_shape=jax.ShapeDtypeStruct((t, GROUP_W), F32),
        grid=(batch, nblk),
        in_specs=[half(0), half(1), half(2), half(3), half(4), half(5),
                  _layer_block(lb, layer),
                  pl.BlockSpec((GROUP_W, GROUP_W), lambda b, i: (0, 0))],
        out_specs=pl.BlockSpec((r, GROUP_W), lambda b, i: (b * nblk + i, 0)),
        scratch_shapes=[pltpu.VMEM((GROUP_W, GROUP_W), F32), slabs, slabs, slabs, rows, rows, rows,
                        pltpu.VMEM((r // SUB, GROUP_W), F32), pltpu.VMEM((r // win, GROUP_W, win), BF16)],
        name="hgrn2",
        compiler_params=pltpu.CompilerParams(dimension_semantics=("arbitrary", "arbitrary"),
                                             vmem_limit_bytes=vmem),
    )(hg, hg, hg, hg, hg, hg, lb, ee)


def _s5_local_kernel(u_ref, bd_ref, bbd_ref, we_ref, d_ref, y_ref, e_ref):
    r = u_ref.shape[0]
    u = u_ref[...]
    rsub = _iota((r, GROUP_W), 0) & (SUB - 1)
    y = d_ref[...] * u
    for j in range(SUB):
        uj = jnp.where(rsub >= j, _shift_rows(u, j), 0.0).astype(BF16)
        y = y + jnp.dot(uj, bd_ref[j], preferred_element_type=F32)
    y_ref[...] = y
    n = S5_GROUPS * S5_P
    bu = jnp.dot(u.astype(BF16), bbd_ref[...], preferred_element_type=F32).reshape(r // SUB, SUB, 2 * n)
    bur, bui = bu[:, :, :n], bu[:, :, n:]
    wr, wi = we_ref[:, :n], we_ref[:, n:]
    e_ref[:, :n] = jnp.sum(wr * bur - wi * bui, axis=1)
    e_ref[:, n:] = jnp.sum(wr * bui + wi * bur, axis=1)


def _s5_scan_kernel(e_ref, a_ref, x_ref):
    nchunk = e_ref.shape[0]
    n = S5_GROUPS * S5_P
    xr, xi = e_ref[:, :n], e_ref[:, n:]
    pr, pi = a_ref[:, :n], a_ref[:, n:]
    row = _iota((nchunk, n), 0)
    k = 1
    while k < nchunk:
        sr = jnp.where(row >= k, pltpu.roll(xr, k, 0), 0.0)
        si = jnp.where(row >= k, pltpu.roll(xi, k, 0), 0.0)
        xr, xi = xr + pr * sr - pi * si, xi + pr * si + pi * sr
        pr, pi = pr * pr - pi * pi, 2.0 * pr * pi
        k *= 2
    x_ref[:, :n] = jnp.where(row >= 1, pltpu.roll(xr, 1, 0), 0.0)
    x_ref[:, n:] = jnp.where(row >= 1, pltpu.roll(xi, 1, 0), 0.0)


def _s5_out_kernel(y1_ref, x_ref, a1_ref, cbd_ref, wg_ref, bg_ref, o_ref, xs_ref):
    r = y1_ref.shape[0]
    n = S5_GROUPS * S5_P
    ar, ai = a1_ref[:, :n], a1_ref[:, n:]

    def chunk(c, carry):
        base = pl.multiple_of(c * SUB, SUB)
        xr = x_ref[pl.ds(c, 1), :n]
        xi = x_ref[pl.ds(c, 1), n:]
        xs_ref[pl.ds(base, SUB), :n] = (ar * xr - ai * xi).astype(BF16)
        xs_ref[pl.ds(base, SUB), n:] = (ar * xi + ai * xr).astype(BF16)
        return carry

    lax.fori_loop(0, r // SUB, chunk, 0)
    y = y1_ref[...] + jnp.dot(xs_ref[...], cbd_ref[...], preferred_element_type=F32)
    zact = 0.5 * y * (1.0 + jnp.tanh(math.sqrt(2.0 / math.pi) * (y + 0.044715 * (y * y * y))))
    gate = jnp.dot(zact.astype(BF16), wg_ref[...], preferred_element_type=F32) + bg_ref[...]
    o_ref[...] = zact * _sigmoid(gate)


def _s5_prep_kernel(ar_ref, ai_ref, bre_ref, bim_ref, cre_ref, cim_ref,
                    bd_ref, bbd_ref, cbd_ref, wend_ref, ain_ref, achunk_ref):
    n = S5_GROUPS * S5_P
    hi = lax.Precision.HIGHEST
    ar, ai = ar_ref[...], ai_ref[...]
    bre, bim, cre, cim = bre_ref[...], bim_ref[...], cre_ref[...], cim_ref[...]
    bbd_ref[:, :n] = bre.astype(BF16)
    bbd_ref[:, n:] = bim.astype(BF16)
    cbd_ref[:n, :] = cre.astype(BF16)
    cbd_ref[n:, :] = (-cim).astype(BF16)
    pr, pi = jnp.ones_like(ar), jnp.zeros_like(ar)
    for j in range(SUB):
        k_lag = (jnp.dot(bre * pr - bim * pi, cre, precision=hi, preferred_element_type=F32)
                 - jnp.dot(bre * pi + bim * pr, cim, precision=hi, preferred_element_type=F32))
        bd_ref[j] = k_lag.astype(BF16)
        wend_ref[SUB - 1 - j:SUB - j, :n] = pr
        wend_ref[SUB - 1 - j:SUB - j, n:] = pi
        pr, pi = pr * ar - pi * ai, pr * ai + pi * ar
        ain_ref[j:j + 1, :n] = pr
        ain_ref[j:j + 1, n:] = pi
    achunk_ref[:, :n] = pr
    achunk_ref[:, n:] = pi


def _s5_operators(lam_re, lam_im, log_step, b_re, b_im, c_re, c_im):
    depth = lam_re.shape[0]
    step = jnp.exp(log_step.astype(F32))[..., None]
    lre = jnp.minimum(lam_re.astype(F32), -1e-4)
    lim = lam_im.astype(F32)
    mag = jnp.exp(lre * step)
    a_re, a_im = mag * jnp.cos(lim * step), mag * jnp.sin(lim * step)
    den = lre * lre + lim * lim
    coef_re = ((a_re - 1.0) * lre + a_im * lim) / den
    coef_im = (a_im * lre - (a_re - 1.0) * lim) / den
    br, bi = b_re.astype(F32), b_im.astype(F32)
    bb_re = coef_re[..., None] * br - coef_im[..., None] * bi
    bb_im = coef_re[..., None] * bi + coef_im[..., None] * br
    cr, ci = c_re.astype(F32), c_im.astype(F32)
    n = S5_GROUPS * S5_P
    same_group = (np.arange(GROUP_W)[:, None] // S5_CH) == (np.arange(n)[None, :] // S5_P)
    in_mask = jnp.asarray(same_group, F32)
    out_mask = jnp.asarray(same_group.T, F32)

    def in_side(b):
        return jnp.tile(b.transpose(0, 1, 3, 2).reshape(depth, GROUP_W, S5_P), (1, 1, S5_GROUPS)) * in_mask

    def out_side(c):
        return jnp.tile(c.transpose(0, 1, 3, 2).reshape(depth, n, S5_CH), (1, 1, S5_GROUPS)) * out_mask

    args = (a_re.reshape(depth, 1, n), a_im.reshape(depth, 1, n), in_side(bb_re), in_side(bb_im),
            out_side(cr), out_side(ci))
    out_dims = (((SUB, GROUP_W, GROUP_W), BF16),
                ((GROUP_W, 2 * n), BF16),
                ((2 * n, GROUP_W), BF16),
                ((SUB, 2 * n), F32),
                ((SUB, 2 * n), F32),
                ((1, 2 * n), F32))

    def per_layer(shape):
        return pl.BlockSpec((None,) + shape, lambda l: (l,) + (0,) * len(shape))

    return pl.pallas_call(
        _s5_prep_kernel,
        out_shape=tuple(jax.ShapeDtypeStruct((depth,) + s, d) for s, d in out_dims),
        grid=(depth,),
        in_specs=[per_layer(a.shape[1:]) for a in args],
        out_specs=tuple(per_layer(s) for s, _ in out_dims),
        name="s5_prep",
        compiler_params=pltpu.CompilerParams(
            dimension_semantics=("arbitrary",),
            vmem_limit_bytes=_vmem_limit(8 * _nbytes((GROUP_W, n), F32), 10 * _nbytes((GROUP_W, n), F32),
                                         4 * _nbytes((SUB, GROUP_W, GROUP_W), BF16))),
    )(*args)


def _s5(u, layer, ops, d_skip, w_glu, b_glu, batch, seq, r):
    t = u.shape[0]
    bd, bbd, cbd, w_end, a_in, a_chunk = ops
    n2 = 2 * S5_GROUPS * S5_P
    nck = seq // SUB

    def whole(a):
        return _layer_block(a, layer)

    rows = pl.BlockSpec((r, GROUP_W), lambda i: (i, 0))
    crow = pl.BlockSpec((r // SUB, n2), lambda i: (i, 0))
    y1, e = pl.pallas_call(
        _s5_local_kernel,
        out_shape=(jax.ShapeDtypeStruct((t, GROUP_W), F32), jax.ShapeDtypeStruct((t // SUB, n2), F32)),
        grid=(t // r,),
        in_specs=[rows, whole(bd), whole(bbd), whole(w_end), whole(d_skip)],
        out_specs=(rows, crow),
        name="s5_local",
        compiler_params=pltpu.CompilerParams(
            dimension_semantics=("arbitrary",),
            vmem_limit_bytes=_vmem_limit(_nbytes(bd.shape[1:], BF16), _nbytes(bbd.shape[1:], BF16),
                                         6 * _nbytes((r, n2), F32), 8 * _nbytes((r, GROUP_W), F32))),
    )(u, bd, bbd, w_end, d_skip)
    xin = pl.pallas_call(
        _s5_scan_kernel,
        out_shape=jax.ShapeDtypeStruct((t // SUB, n2), F32),
        grid=(batch,),
        in_specs=[pl.BlockSpec((nck, n2), lambda b: (b, 0)), whole(a_chunk)],
        out_specs=pl.BlockSpec((nck, n2), lambda b: (b, 0)),
        name="s5_scan",
        compiler_params=pltpu.CompilerParams(
            dimension_semantics=("arbitrary",), vmem_limit_bytes=_vmem_limit(10 * _nbytes((nck, n2), F32))),
    )(e, a_chunk)
    return pl.pallas_call(
        _s5_out_kernel,
        out_shape=jax.ShapeDtypeStruct((t, GROUP_W), F32),
        grid=(t // r,),
        in_specs=[rows, crow, whole(a_in), whole(cbd), whole(w_glu), whole(b_glu)],
        out_specs=rows,
        scratch_shapes=[pltpu.VMEM((r, n2), BF16)],
        name="s5_out",
        compiler_params=pltpu.CompilerParams(
            dimension_semantics=("arbitrary",),
            vmem_limit_bytes=_vmem_limit(_nbytes(cbd.shape[1:], BF16), 3 * _nbytes((r, n2), BF16),
                                         10 * _nbytes((r, GROUP_W), F32))),
    )(y1, xin, a_in, cbd, w_glu, b_glu)


HALO = 2 * V7X_SUBLANES


def _post_kernel(tiles_per_seq, final, ya_ref, yb_ref, yc_ref, gate_ref, yd_ref, h_ref, p_ref,
                 gn_ref, wo_ref, fg_ref, wu_ref, cw_ref, cb_ref, wd_ref, pg_ref, wpg_ref, wp_ref, ng_ref,
                 o_ref, xn_ref, up_ref, act_ref):
    tm = h_ref.shape[0]
    gw = GROUP_W

    @pl.when(pl.program_id(0) % tiles_per_seq == 0)
    def _():
        xn_ref[0:HALO, :] = jnp.zeros((HALO, D_MODEL), BF16)

    parts = (_rms(ya_ref[...], gn_ref[:, 0:gw]),
             _rms(yb_ref[...], gn_ref[:, gw:2 * gw]),
             _rms(yc_ref[...], gn_ref[:, 2 * gw:3 * gw]) * _sigmoid(gate_ref[...]),
             _rms(yd_ref[...], gn_ref[:, 3 * gw:4 * gw]))
    h1 = h_ref[...]
    for g, part in enumerate(parts):
        h1 = h1 + jnp.dot(part.astype(BF16), wo_ref[g * gw:(g + 1) * gw, :], preferred_element_type=F32)
    o_ref[...] = h1
    xn_ref[HALO:, :] = _rms(h1, fg_ref[...]).astype(BF16)

    nchunk = D_FF // MXU_TILE

    def conv(slot, c):
        cols = slice(c * MXU_TILE, (c + 1) * MXU_TILE)
        up_ref[slot] = jnp.dot(xn_ref[...], wu_ref[:, cols], preferred_element_type=F32)
        return (cb_ref[:, cols] + cw_ref[0:1, cols] * up_ref[slot, pl.ds(HALO - 2, tm), :]
                + cw_ref[1:2, cols] * up_ref[slot, pl.ds(HALO - 1, tm), :]
                + cw_ref[2:3, cols] * up_ref[slot, pl.ds(HALO, tm), :])

    for c in range(nchunk):
        slot = 2 * (c % 2)
        gate = conv(slot, c)
        val = conv(slot + 1, c + nchunk)
        act_ref[:, c * MXU_TILE:(c + 1) * MXU_TILE] = (gate * _sigmoid(gate) * val).astype(BF16)
    xn_ref[0:HALO, :] = xn_ref[tm:tm + HALO, :]
    h2 = o_ref[...] + jnp.dot(act_ref[...], wd_ref[...], preferred_element_type=F32)

    pgate = _sigmoid(jnp.dot(_rms(h2, pg_ref[...]).astype(BF16), wpg_ref[...], preferred_element_type=F32))
    out = h2 + pgate * jnp.dot(p_ref[...].astype(BF16), wp_ref[...], preferred_element_type=F32)
    o_ref[...] = _rms(out, ng_ref[...]) if final else out


def _post(ya, yb, yc, hg, yd, h, p, seq, layer, stacked, final, tm):
    t = h.shape[0]
    rows = pl.BlockSpec((tm, GROUP_W), lambda i: (i, 0))
    wide = pl.BlockSpec((tm, D_MODEL), lambda i: (i, 0))
    consts = tuple(stacked)
    vmem = _vmem_limit(sum(_nbytes(c.shape[1:], c.dtype) for c in consts), 12 * _nbytes((tm, GROUP_W), F32),
                       4 * _nbytes((tm, D_MODEL), F32), _nbytes((tm + HALO, D_MODEL), BF16),
                       4 * _nbytes((tm + HALO, MXU_TILE), F32), _nbytes((tm, D_FF), BF16),
                       3 * _nbytes((tm, D_MODEL), F32))
    return pl.pallas_call(
        functools.partial(_post_kernel, seq // tm, final),
        out_shape=jax.ShapeDtypeStruct((t, D_MODEL), F32),
        grid=(t // tm,),
        in_specs=[rows, rows, rows, pl.BlockSpec((tm, GROUP_W), lambda i: (i, 3)), rows, wide,
                  pl.BlockSpec((tm, PLE_DIM), lambda i: (layer * (t // tm) + i, 0))]
        + [_layer_block(c, layer) for c in consts],
        out_specs=wide,
        scratch_shapes=[pltpu.VMEM((tm + HALO, D_MODEL), BF16), pltpu.VMEM((4, tm + HALO, MXU_TILE), F32),
                        pltpu.VMEM((tm, D_FF), BF16)],
        name="post_mixer",
        compiler_params=pltpu.CompilerParams(dimension_semantics=("arbitrary",), vmem_limit_bytes=vmem),
    )(ya, yb, yc, hg, yd, h, p, *consts)


def _tiles(seq):
    tm = min(512, seq)
    return dict(tm=tm, tq=min(512, seq), tkv=min(512, seq), r=min(512, seq))


def kernel(x, p, positions, attn_norm_g, w_in, mla_q_norm_g, mla_w_uq, mla_kv_norm_g, mla_w_ukv, fox_b_f,
           hgrn_lb_param, s5_lam_re, s5_lam_im, s5_log_step, s5_b_re, s5_b_im, s5_c_re, s5_c_im, s5_d,
           s5_w_glu, s5_b_glu, group_norm_g, w_out, ffn_norm_g, w_up, conv_w, conv_b, w_down, ple_norm_g,
           w_ple_gate, w_ple, final_norm_g):
    batch, seq, _ = x.shape
    depth = w_in.shape[0]
    t = batch * seq
    ts = _tiles(seq)
    assert seq % ts["tm"] == 0 and seq % ts["tq"] == 0 and ts["tq"] % ts["tkv"] == 0 and seq % ts["r"] == 0
    assert (seq // SUB) & (seq // SUB - 1) == 0, "chunk scan assumes a power-of-two chunk count"

    lb_all = jnp.cumsum(jax.nn.softmax(hgrn_lb_param.astype(F32), axis=0), axis=0)
    lb_all = lb_all - lb_all[0:1]
    tables = _rope_tables(positions, ts["tm"])

    def row(v):
        return v.reshape(v.shape[0], 1, -1)

    in_params = _inproj_params(attn_norm_g, w_in, mla_q_norm_g, mla_w_uq, mla_kv_norm_g, mla_w_ukv, fox_b_f)
    s5_ops = _s5_operators(s5_lam_re, s5_lam_im, s5_log_step, s5_b_re, s5_b_im, s5_c_re, s5_c_im)
    s5_tail = (row(s5_d.astype(F32)), s5_w_glu.astype(BF16), row(s5_b_glu.astype(F32)))
    post_params = (row(group_norm_g), w_out.astype(BF16), row(ffn_norm_g), w_up.astype(BF16), conv_w,
                   row(conv_b), w_down.astype(BF16), row(ple_norm_g), w_ple_gate.astype(BF16),
                   w_ple.astype(BF16), jnp.broadcast_to(final_norm_g.reshape(1, 1, -1), (depth, 1, D_MODEL)))
    h = x.reshape(t, D_MODEL)
    for i in range(depth):
        mq, mk, mv, fq, fk, fv, hg, su = _inproj(h, seq, i, in_params, tables, ts["tm"])
        y_a = _attention(mq, mk, mv, batch, seq, ts["tq"], ts["tkv"])
        y_b = _attention(fq, fk, fv, batch, seq, ts["tq"], ts["tkv"])
        y_c = _hgrn(hg, i, row(lb_all), batch, seq, ts["r"])
        y_d = _s5(su, i, s5_ops, *s5_tail, batch, seq, ts["r"])
        h = _post(y_a, y_b, y_c, hg, y_d, h, p.reshape(depth * t, PLE_DIM), seq, i, post_params,
                  i == depth - 1, ts["tm"])
    return h.reshape(batch, seq, D_MODEL)
```

```python
import functools
import math

import numpy as np
import jax
import jax.numpy as jnp
from jax import lax
from jax.experimental import pallas as pl
from jax.experimental.pallas import tpu as pltpu

F32 = jnp.float32
BF16 = jnp.bfloat16

D_MODEL = 1024
N_HEADS = 4
HEAD_DIM = 64
GROUP_W = 256
MLA_Q_RANK = 256
MLA_KV_RANK = 128
MLA_NOPE = 64
MLA_ROPE = 32
ROPE_THETA = 10000.0
S5_GROUPS = 16
S5_CH = 16
S5_P = 64
D_FF = 2816
PLE_DIM = 256
EPS = 1e-6
N_IN = 2468

V7X_LANES = 128
V7X_SUBLANES = 8
V7X_VMEM_BYTES = 64 * 1024 * 1024
MXU_TILE = 256
VMEM_CAP_BYTES = 58 * 1024 * 1024

HEAD_PAD = V7X_LANES
SUB = 16
NEG_BIG = -1e30

SEG_CQ = 0
SEG_CKV = 256
SEG_KR = 384
SEG_FOX = 512
SEG_HG = 1280
SEG_S5 = 2304
SEG_FF = 2560
N_PERM = 2688
ROPE_LANE0 = 64
BIAS_LANE0 = 64
LOG2E = math.log2(math.e)


def _vmem_limit(*byte_counts):
    need = int(sum(byte_counts))
    return int(min(VMEM_CAP_BYTES, need + need // 4 + (4 << 20)))


def _nbytes(shape, dtype):
    return int(np.prod(shape)) * jnp.dtype(dtype).itemsize


def _rms(x, g):
    return x * lax.rsqrt(jnp.mean(x * x, axis=-1, keepdims=True) + EPS) * g


def _log_sigmoid(z):
    return jnp.minimum(z, 0.0) - jnp.log1p(jnp.exp(-jnp.abs(z)))


def _sigmoid(z):
    return 1.0 / (1.0 + jnp.exp(-z))


def _iota(shape, dim):
    return lax.broadcasted_iota(jnp.int32, shape, dim)


def _shift_rows(x, k):
    if k == 0:
        return x
    return pltpu.roll(x, k, 0)


def _rope_kernel(pos_ref, freq_ref, ct_ref, s1_ref, s2_ref):
    ang = pos_ref[...].astype(F32) * freq_ref[...]
    lane = _iota(ang.shape, 1)
    half = MLA_ROPE // 2
    sin = jnp.sin(ang)
    ct_ref[...] = jnp.cos(ang)
    s1_ref[...] = jnp.where((lane >= ROPE_LANE0) & (lane < ROPE_LANE0 + half), -sin, 0.0)
    s2_ref[...] = jnp.where((lane >= ROPE_LANE0 + half) & (lane < ROPE_LANE0 + 2 * half), sin, 0.0)


def _rope_tables(positions, tm):
    t = positions.size
    half = MLA_ROPE // 2
    inv_freq = ROPE_THETA ** (-jnp.arange(half, dtype=F32) / half)
    freq = jnp.zeros((1, HEAD_PAD), F32).at[0, ROPE_LANE0:ROPE_LANE0 + 2 * half].set(jnp.tile(inv_freq, 2))
    pos = positions.reshape(t, 1)
    out = jax.ShapeDtypeStruct((t, HEAD_PAD), F32)
    spec = pl.BlockSpec((tm, HEAD_PAD), lambda i: (i, 0))
    return pl.pallas_call(
        _rope_kernel,
        out_shape=(out, out, out),
        grid=(t // tm,),
        in_specs=[pl.BlockSpec((tm, 1), lambda i: (i, 0)), pl.BlockSpec((1, HEAD_PAD), lambda i: (0, 0))],
        out_specs=(spec, spec, spec),
        name="rope_tables",
        compiler_params=pltpu.CompilerParams(dimension_semantics=("arbitrary",)),
    )(pos, freq)


def _inproj_kernel(tiles_per_seq, h_ref, g_ref, w_ref, qg_ref, wuq_ref, kvg_ref, wukv_ref, bf_ref,
                   ct_ref, s1_ref, s2_ref, selq_ref, selk_ref,
                   mq_ref, mk_ref, mv_ref, fq_ref, fk_ref, fv_ref, hg_ref, su_ref, carry_ref):
    i = pl.program_id(0)

    @pl.when(i % tiles_per_seq == 0)
    def _():
        carry_ref[...] = jnp.zeros_like(carry_ref)

    tm = h_ref.shape[0]
    xn = _rms(h_ref[...], g_ref[...]).astype(BF16)

    proj = jnp.dot(xn, w_ref[...], preferred_element_type=F32)

    def seg(a, b):
        return proj[:, a:b]

    q = jnp.dot(_rms(seg(SEG_CQ, SEG_CKV), qg_ref[...]).astype(BF16), wuq_ref[...],
                preferred_element_type=F32)
    ckv_kr = seg(SEG_CKV, SEG_FOX)
    kv = jnp.dot(_rms(ckv_kr[:, :MLA_KV_RANK], kvg_ref[...]).astype(BF16), wukv_ref[...],
                 preferred_element_type=F32)
    ct, s1, s2 = ct_ref[...], s1_ref[...], s2_ref[...]
    half = MLA_ROPE // 2

    def rope(t):
        return t * ct + pltpu.roll(t, HEAD_PAD - half, 1) * s1 + pltpu.roll(t, half, 1) * s2

    k_pe = rope(ckv_kr[:, MLA_KV_RANK:])
    mla_scale = (MLA_NOPE + MLA_ROPE) ** -0.5 * LOG2E
    for hd in range(N_HEADS):
        sl = slice(hd * HEAD_PAD, (hd + 1) * HEAD_PAD)
        mq_ref[:, sl] = (rope(q[:, sl]) * mla_scale).astype(BF16)
        mk_ref[:, sl] = (kv[:, sl] + k_pe).astype(BF16)
    hp4 = N_HEADS * HEAD_PAD
    ones_pad = jnp.where((_iota((1, hp4), 1) & (HEAD_PAD - 1)) >= HEAD_DIM, 1.0, 0.0)
    mv_ref[...] = (kv[:, hp4:] + ones_pad).astype(BF16)

    lane = _iota((tm, HEAD_PAD), 1)
    keep = lane < N_HEADS

    def parts3(x):
        a = x.astype(BF16).astype(F32)
        r = x - a
        b = r.astype(BF16).astype(F32)
        c = (r - b).astype(BF16).astype(F32)
        return (jnp.where(keep, a, 0.0) + pltpu.roll(jnp.where(keep, b, 0.0), N_HEADS, 1)
                + pltpu.roll(jnp.where(keep, c, 0.0), 2 * N_HEADS, 1))

    lf = _log_sigmoid(seg(SEG_FF, N_PERM) + bf_ref[...])
    tril = (_iota((tm, tm), 0) >= _iota((tm, tm), 1)).astype(BF16)
    cum3 = jnp.dot(tril, parts3(lf).astype(BF16), preferred_element_type=F32)
    cum = (cum3 + pltpu.roll(cum3, HEAD_PAD - N_HEADS, 1) + pltpu.roll(cum3, HEAD_PAD - 2 * N_HEADS, 1)
           + carry_ref[...])
    carry_ref[...] = cum[tm - 1:tm, :]
    parts = (parts3(cum * LOG2E) + jnp.where(lane == 3 * N_HEADS, 1.0, 0.0)).astype(BF16)
    bias_q = jnp.dot(parts, selq_ref[...], preferred_element_type=F32)
    bias_k = jnp.dot(parts, selk_ref[...], preferred_element_type=F32)
    fox = seg(SEG_FOX, SEG_HG)
    fox_scale = HEAD_DIM ** -0.5 * LOG2E
    low = lane < HEAD_DIM
    for hd in range(N_HEADS):
        sl = slice(hd * HEAD_PAD, (hd + 1) * HEAD_PAD)
        src = slice((hd // 2) * HEAD_PAD, (hd // 2 + 1) * HEAD_PAD)

        def head_tile(x):
            tile = x[:, src]
            return pltpu.roll(tile, HEAD_DIM, 1) if hd % 2 else tile

        fq_ref[:, sl] = jnp.where(low, head_tile(fox[:, 0:GROUP_W]) * fox_scale, bias_q[:, sl]).astype(BF16)
        fk_ref[:, sl] = jnp.where(low, head_tile(fox[:, GROUP_W:2 * GROUP_W]), bias_k[:, sl]).astype(BF16)
        fv_ref[:, sl] = jnp.where(low, head_tile(fox[:, 2 * GROUP_W:]), 1.0).astype(BF16)

    hg_ref[...] = seg(SEG_HG, SEG_S5)
    su_ref[...] = seg(SEG_S5, SEG_FF)


def _permute_inproj(w):
    def zeros(n):
        return jnp.zeros(w.shape[:-1] + (n,), w.dtype)

    cols = [w[..., 0:384], zeros(ROPE_LANE0), w[..., 384:416], zeros(HEAD_PAD - ROPE_LANE0 - MLA_ROPE),
            w[..., 416:1184], w[..., 1188:N_IN], w[..., 1184:1188], zeros(HEAD_PAD - N_HEADS)]
    out = jnp.concatenate(cols, axis=-1)
    assert out.shape[-1] == N_PERM
    return out


def _bias_selectors():
    selq = np.zeros((HEAD_PAD, N_HEADS * HEAD_PAD), np.float32)
    selk = np.zeros((HEAD_PAD, N_HEADS * HEAD_PAD), np.float32)
    one = 3 * N_HEADS
    for hd in range(N_HEADS):
        for j in range(3):
            selq[N_HEADS * j + hd, hd * HEAD_PAD + BIAS_LANE0 + j] = 1.0
            selq[one, hd * HEAD_PAD + BIAS_LANE0 + 3 + j] = 1.0
            selk[one, hd * HEAD_PAD + BIAS_LANE0 + j] = 1.0
            selk[N_HEADS * j + hd, hd * HEAD_PAD + BIAS_LANE0 + 3 + j] = -1.0
    return jnp.asarray(selq, BF16), jnp.asarray(selk, BF16)


def _layer_block(a, layer):
    shape = a.shape[1:]
    return pl.BlockSpec((None,) + shape, lambda *_: (layer,) + (0,) * len(shape), pipeline_mode=pl.Buffered(1))


def _inproj_params(attn_g, w_in, q_g, w_uq, kv_g, w_ukv, b_f):
    depth = w_in.shape[0]
    wuq = jnp.pad(w_uq.reshape(depth, MLA_Q_RANK, N_HEADS, MLA_NOPE + MLA_ROPE),
                  ((0, 0), (0, 0), (0, 0), (0, HEAD_PAD - MLA_NOPE - MLA_ROPE)))
    wkv = w_ukv.reshape(depth, MLA_KV_RANK, N_HEADS, 2 * HEAD_DIM)
    head_pad = ((0, 0), (0, 0), (0, 0), (0, HEAD_PAD - HEAD_DIM))
    wk = jnp.pad(wkv[..., :MLA_NOPE], head_pad).reshape(depth, MLA_KV_RANK, -1)
    wv = jnp.pad(wkv[..., MLA_NOPE:], head_pad).reshape(depth, MLA_KV_RANK, -1)
    bf = jnp.pad(b_f.astype(F32), ((0, 0), (0, HEAD_PAD - N_HEADS)))

    def row(v):
        return v.reshape(depth, 1, -1)

    return (row(attn_g), _permute_inproj(w_in.astype(BF16)), row(q_g),
            wuq.reshape(depth, MLA_Q_RANK, -1).astype(BF16), row(kv_g),
            jnp.concatenate([wk, wv], axis=2).astype(BF16), row(bf))


def _inproj(h, seq, layer, stacked, tables, tm):
    t = h.shape[0]
    selq, selk = _bias_selectors()
    ct, s1, s2 = tables
    hp4 = N_HEADS * HEAD_PAD
    w_perm = stacked[1]

    def rows(width):
        return pl.BlockSpec((tm, width), lambda i: (i, 0))

    def whole(a):
        return pl.BlockSpec(a.shape, lambda i: (0,) * a.ndim)

    args = (h,) + tuple(stacked) + (ct, s1, s2, selq, selk)
    in_specs = ([rows(D_MODEL)] + [_layer_block(a, layer) for a in stacked]
                + [rows(HEAD_PAD)] * 3 + [whole(selq), whole(selk)])
    out_widths = (hp4,) * 6
    out_shape = tuple(jax.ShapeDtypeStruct((t, w), BF16) for w in out_widths) + (
        jax.ShapeDtypeStruct((t, 4 * GROUP_W), F32), jax.ShapeDtypeStruct((t, GROUP_W), F32))
    out_specs = tuple(rows(w) for w in out_widths) + (rows(4 * GROUP_W), rows(GROUP_W))
    vmem = _vmem_limit(_nbytes(w_perm.shape[1:], BF16), 2 * _nbytes((tm, D_MODEL), F32),
                       2 * sum(_nbytes((tm, w), BF16) for w in out_widths),
                       2 * _nbytes((tm, 5 * GROUP_W), F32), _nbytes((tm, N_PERM), F32),
                       _nbytes((tm, D_MODEL), F32))
    return pl.pallas_call(
        functools.partial(_inproj_kernel, seq // tm),
        out_shape=out_shape,
        grid=(t // tm,),
        in_specs=in_specs,
        out_specs=out_specs,
        scratch_shapes=[pltpu.VMEM((1, HEAD_PAD), F32)],
        name="inproj",
        compiler_params=pltpu.CompilerParams(dimension_semantics=("arbitrary",), vmem_limit_bytes=vmem),
    )(*args)


def _attn_kernel(tq, tkv, q_ref, k_ref, v_ref, o_ref, m_ref, acc_ref):
    qi = pl.program_id(1)
    m_ref[...] = jnp.full_like(m_ref, NEG_BIG)
    acc_ref[...] = jnp.zeros_like(acc_ref)
    per_q = tq // tkv

    def block(start, width, diag_offset):
        rows = pl.ds(pl.multiple_of(start, tkv), width)
        heads = [slice(hd * HEAD_PAD, (hd + 1) * HEAD_PAD) for hd in range(N_HEADS)]
        logits = [lax.dot_general(q_ref[:, sl], k_ref[rows, sl], (((1,), (1,)), ((), ())),
                                  preferred_element_type=F32) for sl in heads]
        probs, rescale = [], []
        for hd in range(N_HEADS):
            s = logits[hd]
            if diag_offset is not None:
                visible = _iota(s.shape, 1) + diag_offset <= _iota(s.shape, 0)
                s = jnp.where(visible, s, NEG_BIG)
            m_old = m_ref[hd]
            m_new = jnp.maximum(m_old, jnp.max(s, axis=1, keepdims=True))
            probs.append(jnp.exp2(s - jnp.concatenate([m_new] * (width // HEAD_PAD), axis=1)).astype(BF16))
            rescale.append(jnp.exp2(m_old - m_new))
            m_ref[hd] = m_new
        for hd, sl in enumerate(heads):
            acc_ref[hd] = (rescale[hd] * acc_ref[hd]
                           + jnp.dot(probs[hd], v_ref[rows, sl], preferred_element_type=F32))

    n_full = qi * per_q

    def wide_block(jj, carry):
        block(jj * (2 * tkv), 2 * tkv, None)
        return carry

    lax.fori_loop(0, n_full // 2, wide_block, 0)

    @pl.when(n_full % 2 == 1)
    def _():
        block((n_full - 1) * tkv, tkv, None)

    for r in range(per_q):
        block((n_full + r) * tkv, tkv, r * tkv)

    low = _iota((tq, HEAD_PAD), 1) < HEAD_DIM
    for pair in range(N_HEADS // 2):
        a0, a1 = acc_ref[2 * pair], acc_ref[2 * pair + 1]
        n0 = a0 / pltpu.roll(a0, HEAD_DIM, 1)
        n1 = a1 / pltpu.roll(a1, HEAD_DIM, 1)
        o_ref[:, pair * HEAD_PAD:(pair + 1) * HEAD_PAD] = jnp.where(low, n0, pltpu.roll(n1, HEAD_DIM, 1))


def _attention(q, k, v, batch, seq, tq, tkv):
    t = q.shape[0]
    hp4 = N_HEADS * HEAD_PAD
    vmem = _vmem_limit(2 * _nbytes((seq, hp4), BF16), 2 * _nbytes((tq, hp4), BF16),
                       2 * _nbytes((tq, GROUP_W), F32), 2 * N_HEADS * _nbytes((tq, HEAD_PAD), F32),
                       6 * _nbytes((tq, 2 * tkv), F32))
    resident = pl.BlockSpec((seq, hp4), lambda b, i: (b, 0), pipeline_mode=pl.Buffered(1))
    return pl.pallas_call(
        functools.partial(_attn_kernel, tq, tkv),
        out_shape=jax.ShapeDtypeStruct((t, GROUP_W), F32),
        grid=(batch, seq // tq),
        in_specs=[pl.BlockSpec((tq, hp4), lambda b, i: (b * (seq // tq) + i, 0)), resident, resident],
        out_specs=pl.BlockSpec((tq, GROUP_W), lambda b, i: (b * (seq // tq) + i, 0)),
        scratch_shapes=[pltpu.VMEM((N_HEADS, tq, HEAD_PAD), F32), pltpu.VMEM((N_HEADS, tq, HEAD_PAD), F32)],
        name="causal_attention",
        compiler_params=pltpu.CompilerParams(dimension_semantics=("arbitrary", "arbitrary"),
                                             vmem_limit_bytes=vmem),
    )(q, k, v)


def _hgrn_kernel(win, q0_ref, q1_ref, f0_ref, f1_ref, v0_ref, v1_ref, lb_ref, ee_ref, o_ref,
                 st_ref, bc_ref, kk_ref, vs_ref, qe_ref, ke_ref, od_ref, dec_ref, vt_ref):
    @pl.when(pl.program_id(1) == 0)
    def _():
        st_ref[...] = jnp.zeros_like(st_ref)

    q_ref, f_ref, v_ref = (q0_ref, q1_ref), (f0_ref, f1_ref), (v0_ref, v1_ref)
    r = q0_ref.shape[0]
    g = r // SUB
    lb = lb_ref[...]
    log_lb = jnp.log(lb)
    log_1m = jnp.log1p(-lb)

    def slab(halves, s):
        return jnp.concatenate([h[pl.ds(s, g, stride=SUB), :] for h in halves], axis=1)

    def put_rows(ref, s, x):
        for half in range(2):
            ref[half, pl.ds(s, g, stride=SUB), :] = x[:, half * V7X_LANES:(half + 1) * V7X_LANES]

    run = None
    for s in range(SUB):
        z = slab(f_ref, s)
        b = log_1m + _log_sigmoid(z)
        log_f = jnp.maximum(log_lb, b) + jnp.log1p(jnp.exp(-jnp.abs(log_lb - b)))
        run = log_f if s == 0 else run + log_f
        bc_ref[s] = run
        kk_ref[s] = (1.0 - lb) * _sigmoid(-z)
        vs_ref[s] = slab(v_ref, s)
    total = bc_ref[SUB - 1]
    dec_ref[...] = jnp.exp(total)

    for s in range(SUB):
        qs = slab(q_ref, s)
        bcs = bc_ref[s]
        prods = [(qs * kk_ref[j] * jnp.exp(bcs - bc_ref[j])).astype(BF16) for j in range(s)]
        prods.append((qs * kk_ref[s]).astype(BF16))
        red = jnp.dot(jnp.concatenate(prods, axis=0), ee_ref[...], preferred_element_type=F32)
        od = red[0:g] * vs_ref[0]
        for j in range(1, s + 1):
            od = od + red[j * g:(j + 1) * g] * vs_ref[j]
        put_rows(od_ref, s, od)
        put_rows(qe_ref, s, qs * jnp.exp(bcs))
        put_rows(ke_ref, s, kk_ref[s] * jnp.exp(total - bcs))
    for w in range(r // win):
        for half in range(2):
            vt_ref[w, half * V7X_LANES:(half + 1) * V7X_LANES, :] = (
                v_ref[half][w * win:(w + 1) * win, :].T.astype(BF16))

    hl = V7X_LANES
    same_head = (_iota((hl, hl), 0) // HEAD_DIM) == (_iota((hl, hl), 1) // HEAD_DIM)
    per_win = win // SUB

    def window(w, carry):
        halves = (0, 1)
        lanes = [slice(half * hl, (half + 1) * hl) for half in halves]

        def rows_of(c):
            return pl.ds(pl.multiple_of(w * win + c * SUB, SUB), SUB)

        upds = [[], []]
        for half in halves:
            vt = vt_ref[w, lanes[half], :]
            for c in range(per_win):
                pieces = [ke_ref[half, rows_of(c), :].astype(BF16)]
                if c:
                    pieces.insert(0, jnp.zeros((c * SUB, hl), BF16))
                if c < per_win - 1:
                    pieces.append(jnp.zeros(((per_win - 1 - c) * SUB, hl), BF16))
                upd = jnp.dot(vt, jnp.concatenate(pieces, axis=0), preferred_element_type=F32)
                upds[half].append(jnp.where(same_head, upd, 0.0))
        st = [st_ref[half] for half in halves]
        for c in range(per_win):
            rows = rows_of(c)
            dec = dec_ref[pl.ds(w * per_win + c, 1), :]
            for half in halves:
                o_state = lax.dot_general(qe_ref[half, rows, :].astype(BF16), st[half].astype(BF16),
                                          (((1,), (1,)), ((), ())), preferred_element_type=F32)
                o_ref[rows, lanes[half]] = od_ref[half, rows, :] + o_state
                st[half] = st[half] * dec[:, lanes[half]] + upds[half][c]
        for half in halves:
            st_ref[half] = st[half]
        return carry

    lax.fori_loop(0, r // win, window, 0)


def _hgrn(hg, layer, lb, batch, seq, r):
    t = hg.shape[0]
    win = min(r, V7X_LANES)
    ee = jnp.asarray(np.kron(np.eye(N_HEADS, dtype=np.float32), np.ones((HEAD_DIM, HEAD_DIM), np.float32)), BF16)
    nblk = seq // r

    def half(c):
        return pl.BlockSpec((r, V7X_LANES), lambda b, i: (b * nblk + i, c))

    tile = _nbytes((r, GROUP_W), F32)
    vmem = _vmem_limit(8 * tile, 7 * tile, 8 * tile)
    slabs = pltpu.VMEM((SUB, r // SUB, GROUP_W), F32)
    rows = pltpu.VMEM((2, r, V7X_LANES), F32)
    return pl.pallas_call(
        functools.partial(_hgrn_kernel, win),
        out_shape=jax.ShapeDtypeStruct((t, GROUP_W), F32),
        grid=(batch, nblk),
        in_specs=[half(0), half(1), half(2), half(3), half(4), half(5),
                  _layer_block(lb, layer),
                  pl.BlockSpec((GROUP_W, GROUP_W), lambda b, i: (0, 0))],
        out_specs=pl.BlockSpec((r, GROUP_W), lambda b, i: (b * nblk + i, 0)),
        scratch_shapes=[pltpu.VMEM((2, V7X_LANES, V7X_LANES), F32), slabs, slabs, slabs, rows, rows, rows,
                        pltpu.VMEM((r // SUB, GROUP_W), F32), pltpu.VMEM((r // win, GROUP_W, win), BF16)],
        name="hgrn2",
        compiler_params=pltpu.CompilerParams(dimension_semantics=("arbitrary", "arbitrary"),
                                             vmem_limit_bytes=vmem),
    )(hg, hg, hg, hg, hg, hg, lb, ee)


def _s5_local_kernel(u_ref, bd_ref, bbd_ref, we_ref, d_ref, y_ref, e_ref):
    r = u_ref.shape[0]
    u = u_ref[...]
    rsub = _iota((r, GROUP_W), 0) & (SUB - 1)
    y = d_ref[...] * u
    for j in range(SUB):
        uj = jnp.where(rsub >= j, _shift_rows(u, j), 0.0).astype(BF16)
        y = y + jnp.dot(uj, bd_ref[j], preferred_element_type=F32)
    y_ref[...] = y
    n = S5_GROUPS * S5_P
    bu = jnp.dot(u.astype(BF16), bbd_ref[...], preferred_element_type=F32).reshape(r // SUB, SUB, 2 * n)
    bur, bui = bu[:, :, :n], bu[:, :, n:]
    wr, wi = we_ref[:, :n], we_ref[:, n:]
    e_ref[:, :n] = jnp.sum(wr * bur - wi * bui, axis=1)
    e_ref[:, n:] = jnp.sum(wr * bui + wi * bur, axis=1)


def _s5_scan_kernel(e_ref, a_ref, x_ref):
    nchunk = e_ref.shape[0]
    n = S5_GROUPS * S5_P
    xr, xi = e_ref[:, :n], e_ref[:, n:]
    pr, pi = a_ref[:, :n], a_ref[:, n:]
    row = _iota((nchunk, n), 0)
    k = 1
    while k < nchunk:
        sr = jnp.where(row >= k, pltpu.roll(xr, k, 0), 0.0)
        si = jnp.where(row >= k, pltpu.roll(xi, k, 0), 0.0)
        xr, xi = xr + pr * sr - pi * si, xi + pr * si + pi * sr
        pr, pi = pr * pr - pi * pi, 2.0 * pr * pi
        k *= 2
    x_ref[:, :n] = jnp.where(row >= 1, pltpu.roll(xr, 1, 0), 0.0)
    x_ref[:, n:] = jnp.where(row >= 1, pltpu.roll(xi, 1, 0), 0.0)


def _s5_out_kernel(y1_ref, x_ref, a1_ref, cbd_ref, wg_ref, bg_ref, o_ref, xs_ref):
    r = y1_ref.shape[0]
    n = S5_GROUPS * S5_P
    ar, ai = a1_ref[:, :n], a1_ref[:, n:]

    def chunk(c, carry):
        base = pl.multiple_of(c * SUB, SUB)
        xr = x_ref[pl.ds(c, 1), :n]
        xi = x_ref[pl.ds(c, 1), n:]
        xs_ref[pl.ds(base, SUB), :n] = (ar * xr - ai * xi).astype(BF16)
        xs_ref[pl.ds(base, SUB), n:] = (ar * xi + ai * xr).astype(BF16)
        return carry

    lax.fori_loop(0, r // SUB, chunk, 0)
    y = y1_ref[...] + jnp.dot(xs_ref[...], cbd_ref[...], preferred_element_type=F32)
    zact = 0.5 * y * (1.0 + jnp.tanh(math.sqrt(2.0 / math.pi) * (y + 0.044715 * (y * y * y))))
    gate = jnp.dot(zact.astype(BF16), wg_ref[...], preferred_element_type=F32) + bg_ref[...]
    o_ref[...] = zact * _sigmoid(gate)


def _s5_prep_kernel(ar_ref, ai_ref, bre_ref, bim_ref, cre_ref, cim_ref,
                    bd_ref, bbd_ref, cbd_ref, wend_ref, ain_ref, achunk_ref):
    n = S5_GROUPS * S5_P
    ar, ai = ar_ref[...], ai_ref[...]
    bre, bim, cre, cim = bre_ref[...], bim_ref[...], cre_ref[...], cim_ref[...]
    bbd_ref[:, :n] = bre.astype(BF16)
    bbd_ref[:, n:] = bim.astype(BF16)
    cbd_ref[:n, :] = cre.astype(BF16)
    cbd_ref[n:, :] = (-cim).astype(BF16)

    def split2(x):
        hi = x.astype(BF16)
        return hi, (x - hi.astype(F32)).astype(BF16)

    def dot3(x, w_hi, w_lo):
        x_hi, x_lo = split2(x)
        return (jnp.dot(x_hi, w_hi, preferred_element_type=F32) + jnp.dot(x_hi, w_lo, preferred_element_type=F32)
                + jnp.dot(x_lo, w_hi, preferred_element_type=F32))

    cre_parts, cim_parts = split2(cre), split2(cim)
    pr, pi = jnp.ones_like(ar), jnp.zeros_like(ar)
    for j in range(SUB):
        k_lag = dot3(bre * pr - bim * pi, *cre_parts) - dot3(bre * pi + bim * pr, *cim_parts)
        bd_ref[j] = k_lag.astype(BF16)
        wend_ref[SUB - 1 - j:SUB - j, :n] = pr
        wend_ref[SUB - 1 - j:SUB - j, n:] = pi
        pr, pi = pr * ar - pi * ai, pr * ai + pi * ar
        ain_ref[j:j + 1, :n] = pr
        ain_ref[j:j + 1, n:] = pi
    achunk_ref[:, :n] = pr
    achunk_ref[:, n:] = pi


def _s5_operators(lam_re, lam_im, log_step, b_re, b_im, c_re, c_im):
    depth = lam_re.shape[0]
    step = jnp.exp(log_step.astype(F32))[..., None]
    lre = jnp.minimum(lam_re.astype(F32), -1e-4)
    lim = lam_im.astype(F32)
    mag = jnp.exp(lre * step)
    a_re, a_im = mag * jnp.cos(lim * step), mag * jnp.sin(lim * step)
    den = lre * lre + lim * lim
    coef_re = ((a_re - 1.0) * lre + a_im * lim) / den
    coef_im = (a_im * lre - (a_re - 1.0) * lim) / den
    br, bi = b_re.astype(F32), b_im.astype(F32)
    bb_re = coef_re[..., None] * br - coef_im[..., None] * bi
    bb_im = coef_re[..., None] * bi + coef_im[..., None] * br
    cr, ci = c_re.astype(F32), c_im.astype(F32)
    n = S5_GROUPS * S5_P
    same_group = (np.arange(GROUP_W)[:, None] // S5_CH) == (np.arange(n)[None, :] // S5_P)
    in_mask = jnp.asarray(same_group, F32)
    out_mask = jnp.asarray(same_group.T, F32)

    def in_side(b):
        return jnp.tile(b.transpose(0, 1, 3, 2).reshape(depth, GROUP_W, S5_P), (1, 1, S5_GROUPS)) * in_mask

    def out_side(c):
        return jnp.tile(c.transpose(0, 1, 3, 2).reshape(depth, n, S5_CH), (1, 1, S5_GROUPS)) * out_mask

    args = (a_re.reshape(depth, 1, n), a_im.reshape(depth, 1, n), in_side(bb_re), in_side(bb_im),
            out_side(cr), out_side(ci))
    out_dims = (((SUB, GROUP_W, GROUP_W), BF16),
                ((GROUP_W, 2 * n), BF16),
                ((2 * n, GROUP_W), BF16),
                ((SUB, 2 * n), F32),
                ((SUB, 2 * n), F32),
                ((1, 2 * n), F32))

    def per_layer(shape):
        return pl.BlockSpec((None,) + shape, lambda l: (l,) + (0,) * len(shape))

    return pl.pallas_call(
        _s5_prep_kernel,
        out_shape=tuple(jax.ShapeDtypeStruct((depth,) + s, d) for s, d in out_dims),
        grid=(depth,),
        in_specs=[per_layer(a.shape[1:]) for a in args],
        out_specs=tuple(per_layer(s) for s, _ in out_dims),
        name="s5_prep",
        compiler_params=pltpu.CompilerParams(
            dimension_semantics=("arbitrary",),
            vmem_limit_bytes=_vmem_limit(8 * _nbytes((GROUP_W, n), F32), 10 * _nbytes((GROUP_W, n), F32),
                                         4 * _nbytes((SUB, GROUP_W, GROUP_W), BF16))),
    )(*args)


def _s5(u, layer, ops, d_skip, w_glu, b_glu, batch, seq, r):
    t = u.shape[0]
    bd, bbd, cbd, w_end, a_in, a_chunk = ops
    n2 = 2 * S5_GROUPS * S5_P
    nck = seq // SUB

    def whole(a):
        return _layer_block(a, layer)

    rows = pl.BlockSpec((r, GROUP_W), lambda i: (i, 0))
    crow = pl.BlockSpec((r // SUB, n2), lambda i: (i, 0))
    y1, e = pl.pallas_call(
        _s5_local_kernel,
        out_shape=(jax.ShapeDtypeStruct((t, GROUP_W), F32), jax.ShapeDtypeStruct((t // SUB, n2), F32)),
        grid=(t // r,),
        in_specs=[rows, whole(bd), whole(bbd), whole(w_end), whole(d_skip)],
        out_specs=(rows, crow),
        name="s5_local",
        compiler_params=pltpu.CompilerParams(
            dimension_semantics=("arbitrary",),
            vmem_limit_bytes=_vmem_limit(_nbytes(bd.shape[1:], BF16), _nbytes(bbd.shape[1:], BF16),
                                         6 * _nbytes((r, n2), F32), 8 * _nbytes((r, GROUP_W), F32))),
    )(u, bd, bbd, w_end, d_skip)
    xin = pl.pallas_call(
        _s5_scan_kernel,
        out_shape=jax.ShapeDtypeStruct((t // SUB, n2), F32),
        grid=(batch,),
        in_specs=[pl.BlockSpec((nck, n2), lambda b: (b, 0)), whole(a_chunk)],
        out_specs=pl.BlockSpec((nck, n2), lambda b: (b, 0)),
        name="s5_scan",
        compiler_params=pltpu.CompilerParams(
            dimension_semantics=("arbitrary",), vmem_limit_bytes=_vmem_limit(10 * _nbytes((nck, n2), F32))),
    )(e, a_chunk)
    return pl.pallas_call(
        _s5_out_kernel,
        out_shape=jax.ShapeDtypeStruct((t, GROUP_W), F32),
        grid=(t // r,),
        in_specs=[rows, crow, whole(a_in), whole(cbd), whole(w_glu), whole(b_glu)],
        out_specs=rows,
        scratch_shapes=[pltpu.VMEM((r, n2), BF16)],
        name="s5_out",
        compiler_params=pltpu.CompilerParams(
            dimension_semantics=("arbitrary",),
            vmem_limit_bytes=_vmem_limit(_nbytes(cbd.shape[1:], BF16), 3 * _nbytes((r, n2), BF16),
                                         10 * _nbytes((r, GROUP_W), F32))),
    )(y1, xin, a_in, cbd, w_glu, b_glu)


HALO = 2 * V7X_SUBLANES


def _post_kernel(tiles_per_seq, final, ya_ref, yb_ref, yc_ref, gate_ref, yd_ref, h_ref, p_ref,
                 gn_ref, wo_ref, fg_ref, wu_ref, cw_ref, cb_ref, wd_ref, pg_ref, wpg_ref, wp_ref, ng_ref,
                 o_ref, xn_ref, up_ref, act_ref):
    tm = h_ref.shape[0]
    gw = GROUP_W

    @pl.when(pl.program_id(0) % tiles_per_seq == 0)
    def _():
        xn_ref[0:HALO, :] = jnp.zeros((HALO, D_MODEL), BF16)

    parts = (_rms(ya_ref[...], gn_ref[:, 0:gw]),
             _rms(yb_ref[...], gn_ref[:, gw:2 * gw]),
             _rms(yc_ref[...], gn_ref[:, 2 * gw:3 * gw]) * _sigmoid(gate_ref[...]),
             _rms(yd_ref[...], gn_ref[:, 3 * gw:4 * gw]))
    h1 = h_ref[...]
    for g, part in enumerate(parts):
        h1 = h1 + jnp.dot(part.astype(BF16), wo_ref[g * gw:(g + 1) * gw, :], preferred_element_type=F32)
    o_ref[...] = h1
    xn_ref[HALO:, :] = _rms(h1, fg_ref[...]).astype(BF16)

    nchunk = D_FF // MXU_TILE

    def conv(slot, c):
        cols = slice(c * MXU_TILE, (c + 1) * MXU_TILE)
        up_ref[slot] = jnp.dot(xn_ref[...], wu_ref[:, cols], preferred_element_type=F32)
        return (cb_ref[:, cols] + cw_ref[0:1, cols] * up_ref[slot, pl.ds(HALO - 2, tm), :]
                + cw_ref[1:2, cols] * up_ref[slot, pl.ds(HALO - 1, tm), :]
                + cw_ref[2:3, cols] * up_ref[slot, pl.ds(HALO, tm), :])

    for c in range(nchunk):
        slot = 2 * (c % 2)
        gate = conv(slot, c)
        val = conv(slot + 1, c + nchunk)
        act_ref[:, c * MXU_TILE:(c + 1) * MXU_TILE] = (gate * _sigmoid(gate) * val).astype(BF16)
    xn_ref[0:HALO, :] = xn_ref[tm:tm + HALO, :]
    h2 = o_ref[...] + jnp.dot(act_ref[...], wd_ref[...], preferred_element_type=F32)

    pgate = _sigmoid(jnp.dot(_rms(h2, pg_ref[...]).astype(BF16), wpg_ref[...], preferred_element_type=F32))
    out = h2 + pgate * jnp.dot(p_ref[...].astype(BF16), wp_ref[...], preferred_element_type=F32)
    o_ref[...] = _rms(out, ng_ref[...]) if final else out


def _post(ya, yb, yc, hg, yd, h, p, seq, layer, stacked, final, tm):
    t = h.shape[0]
    rows = pl.BlockSpec((tm, GROUP_W), lambda i: (i, 0))
    wide = pl.BlockSpec((tm, D_MODEL), lambda i: (i, 0))
    consts = tuple(stacked)
    vmem = _vmem_limit(sum(_nbytes(c.shape[1:], c.dtype) for c in consts), 12 * _nbytes((tm, GROUP_W), F32),
                       4 * _nbytes((tm, D_MODEL), F32), _nbytes((tm + HALO, D_MODEL), BF16),
                       4 * _nbytes((tm + HALO, MXU_TILE), F32), _nbytes((tm, D_FF), BF16),
                       3 * _nbytes((tm, D_MODEL), F32))
    return pl.pallas_call(
        functools.partial(_post_kernel, seq // tm, final),
        out_shape=jax.ShapeDtypeStruct((t, D_MODEL), F32),
        grid=(t // tm,),
        in_specs=[rows, rows, rows, pl.BlockSpec((tm, GROUP_W), lambda i: (i, 3)), rows, wide,
                  pl.BlockSpec((tm, PLE_DIM), lambda i: (layer * (t // tm) + i, 0))]
        + [_layer_block(c, layer) for c in consts],
        out_specs=wide,
        scratch_shapes=[pltpu.VMEM((tm + HALO, D_MODEL), BF16), pltpu.VMEM((4, tm + HALO, MXU_TILE), F32),
                        pltpu.VMEM((tm, D_FF), BF16)],
        name="post_mixer",
        compiler_params=pltpu.CompilerParams(dimension_semantics=("arbitrary",), vmem_limit_bytes=vmem),
    )(ya, yb, yc, hg, yd, h, p, *consts)


def _tiles(seq):
    tm = min(512, seq)
    return dict(tm=tm, tq=min(512, seq), tkv=min(512, seq), r=min(512, seq))


def kernel(x, p, positions, attn_norm_g, w_in, mla_q_norm_g, mla_w_uq, mla_kv_norm_g, mla_w_ukv, fox_b_f,
           hgrn_lb_param, s5_lam_re, s5_lam_im, s5_log_step, s5_b_re, s5_b_im, s5_c_re, s5_c_im, s5_d,
           s5_w_glu, s5_b_glu, group_norm_g, w_out, ffn_norm_g, w_up, conv_w, conv_b, w_down, ple_norm_g,
           w_ple_gate, w_ple, final_norm_g):
    batch, seq, _ = x.shape
    depth = w_in.shape[0]
    t = batch * seq
    ts = _tiles(seq)
    assert seq % ts["tm"] == 0 and seq % ts["tq"] == 0 and ts["tq"] % ts["tkv"] == 0 and seq % ts["r"] == 0
    assert (seq // SUB) & (seq // SUB - 1) == 0, "chunk scan assumes a power-of-two chunk count"

    lb_all = jnp.cumsum(jax.nn.softmax(hgrn_lb_param.astype(F32), axis=0), axis=0)
    lb_all = lb_all - lb_all[0:1]
    tables = _rope_tables(positions, ts["tm"])

    def row(v):
        return v.reshape(v.shape[0], 1, -1)

    in_params = _inproj_params(attn_norm_g, w_in, mla_q_norm_g, mla_w_uq, mla_kv_norm_g, mla_w_ukv, fox_b_f)
    s5_ops = _s5_operators(s5_lam_re, s5_lam_im, s5_log_step, s5_b_re, s5_b_im, s5_c_re, s5_c_im)
    s5_tail = (row(s5_d.astype(F32)), s5_w_glu.astype(BF16), row(s5_b_glu.astype(F32)))
    post_params = (row(group_norm_g), w_out.astype(BF16), row(ffn_norm_g), w_up.astype(BF16), conv_w,
                   row(conv_b), w_down.astype(BF16), row(ple_norm_g), w_ple_gate.astype(BF16),
                   w_ple.astype(BF16), jnp.broadcast_to(final_norm_g.reshape(1, 1, -1), (depth, 1, D_MODEL)))
    h = x.reshape(t, D_MODEL)
    for i in range(depth):
        mq, mk, mv, fq, fk, fv, hg, su = _inproj(h, seq, i, in_params, tables, ts["tm"])
        y_a = _attention(mq, mk, mv, batch, seq, ts["tq"], ts["tkv"])
        y_b = _attention(fq, fk, fv, batch, seq, ts["tq"], ts["tkv"])
        y_c = _hgrn(hg, i, row(lb_all), batch, seq, ts["r"])
        y_d = _s5(su, i, s5_ops, *s5_tail, batch, seq, ts["r"])
        h = _post(y_a, y_b, y_c, hg, y_d, h, p.reshape(depth * t, PLE_DIM), seq, i, post_params,
                  i == depth - 1, ts["tm"])
    return h.reshape(batch, seq, D_MODEL)
```

```python
import functools
import math

import numpy as np
import jax
import jax.numpy as jnp
from jax import lax
from jax.experimental import pallas as pl
from jax.experimental.pallas import tpu as pltpu

F32 = jnp.float32
BF16 = jnp.bfloat16

D_MODEL = 1024
N_HEADS = 4
HEAD_DIM = 64
GROUP_W = 256
MLA_Q_RANK = 256
MLA_KV_RANK = 128
MLA_NOPE = 64
MLA_ROPE = 32
ROPE_THETA = 10000.0
S5_GROUPS = 16
S5_CH = 16
S5_P = 64
D_FF = 2816
PLE_DIM = 256
EPS = 1e-6
N_IN = 2468

V7X_LANES = 128
V7X_SUBLANES = 8
V7X_VMEM_BYTES = 64 * 1024 * 1024
MXU_TILE = 256
VMEM_CAP_BYTES = 58 * 1024 * 1024

HEAD_PAD = V7X_LANES
SUB = 16
NEG_BIG = -1e30

SEG_CQ = 0
SEG_CKV = 256
SEG_KR = 384
SEG_FOX = 512
SEG_HG = 1280
SEG_S5 = 2304
SEG_FF = 2560
N_PERM = 2688
ROPE_LANE0 = 64
BIAS_LANE0 = 64
LOG2E = math.log2(math.e)


def _vmem_limit(*byte_counts):
    need = int(sum(byte_counts))
    return int(min(VMEM_CAP_BYTES, need + need // 4 + (4 << 20)))


def _nbytes(shape, dtype):
    return int(np.prod(shape)) * jnp.dtype(dtype).itemsize


def _rms(x, g):
    return x * lax.rsqrt(jnp.mean(x * x, axis=-1, keepdims=True) + EPS) * g


def _log_sigmoid(z):
    return jnp.minimum(z, 0.0) - jnp.log1p(jnp.exp(-jnp.abs(z)))


def _sigmoid(z):
    return 1.0 / (1.0 + jnp.exp(-z))


def _iota(shape, dim):
    return lax.broadcasted_iota(jnp.int32, shape, dim)


def _rope_kernel(pos_ref, freq_ref, ct_ref, s1_ref, s2_ref):
    ang = pos_ref[...].astype(F32) * freq_ref[...]
    lane = _iota(ang.shape, 1)
    half = MLA_ROPE // 2
    sin = jnp.sin(ang)
    ct_ref[...] = jnp.cos(ang)
    s1_ref[...] = jnp.where((lane >= ROPE_LANE0) & (lane < ROPE_LANE0 + half), -sin, 0.0)
    s2_ref[...] = jnp.where((lane >= ROPE_LANE0 + half) & (lane < ROPE_LANE0 + 2 * half), sin, 0.0)


def _rope_tables(positions, tm):
    t = positions.size
    half = MLA_ROPE // 2
    inv_freq = ROPE_THETA ** (-jnp.arange(half, dtype=F32) / half)
    freq = jnp.zeros((1, HEAD_PAD), F32).at[0, ROPE_LANE0:ROPE_LANE0 + 2 * half].set(jnp.tile(inv_freq, 2))
    pos = positions.reshape(t, 1)
    out = jax.ShapeDtypeStruct((t, HEAD_PAD), F32)
    spec = pl.BlockSpec((tm, HEAD_PAD), lambda i: (i, 0))
    return pl.pallas_call(
        _rope_kernel,
        out_shape=(out, out, out),
        grid=(t // tm,),
        in_specs=[pl.BlockSpec((tm, 1), lambda i: (i, 0)), pl.BlockSpec((1, HEAD_PAD), lambda i: (0, 0))],
        out_specs=(spec, spec, spec),
        name="rope_tables",
        compiler_params=pltpu.CompilerParams(dimension_semantics=("arbitrary",)),
    )(pos, freq)


def _inproj_kernel(tiles_per_seq, h_ref, g_ref, w_ref, qg_ref, wuq_ref, kvg_ref, wukv_ref, bf_ref,
                   ct_ref, s1_ref, s2_ref, selq_ref, selk_ref,
                   mq_ref, mk_ref, mv_ref, fq_ref, fk_ref, fv_ref, hg_ref, su_ref, carry_ref):
    i = pl.program_id(0)

    @pl.when(i % tiles_per_seq == 0)
    def _():
        carry_ref[...] = jnp.zeros_like(carry_ref)

    tm = h_ref.shape[0]
    xn = _rms(h_ref[...], g_ref[...]).astype(BF16)

    proj = jnp.dot(xn, w_ref[...], preferred_element_type=F32)

    def seg(a, b):
        return proj[:, a:b]

    q = jnp.dot(_rms(seg(SEG_CQ, SEG_CKV), qg_ref[...]).astype(BF16), wuq_ref[...],
                preferred_element_type=F32)
    ckv_kr = seg(SEG_CKV, SEG_FOX)
    kv = jnp.dot(_rms(ckv_kr[:, :MLA_KV_RANK], kvg_ref[...]).astype(BF16), wukv_ref[...],
                 preferred_element_type=F32)
    ct, s1, s2 = ct_ref[...], s1_ref[...], s2_ref[...]
    half = MLA_ROPE // 2

    def rope(t):
        return t * ct + pltpu.roll(t, HEAD_PAD - half, 1) * s1 + pltpu.roll(t, half, 1) * s2

    k_pe = rope(ckv_kr[:, MLA_KV_RANK:])
    mla_scale = (MLA_NOPE + MLA_ROPE) ** -0.5 * LOG2E
    for hd in range(N_HEADS):
        sl = slice(hd * HEAD_PAD, (hd + 1) * HEAD_PAD)
        mq_ref[:, sl] = (rope(q[:, sl]) * mla_scale).astype(BF16)
        mk_ref[:, sl] = (kv[:, sl] + k_pe).astype(BF16)
    hp4 = N_HEADS * HEAD_PAD
    ones_pad = jnp.where((_iota((1, hp4), 1) & (HEAD_PAD - 1)) >= HEAD_DIM, 1.0, 0.0)
    mv_ref[...] = (kv[:, hp4:] + ones_pad).astype(BF16)

    lane = _iota((tm, HEAD_PAD), 1)
    keep = lane < N_HEADS

    def parts3(x):
        a = x.astype(BF16).astype(F32)
        r = x - a
        b = r.astype(BF16).astype(F32)
        c = (r - b).astype(BF16).astype(F32)
        return (jnp.where(keep, a, 0.0) + pltpu.roll(jnp.where(keep, b, 0.0), N_HEADS, 1)
                + pltpu.roll(jnp.where(keep, c, 0.0), 2 * N_HEADS, 1))

    lf = _log_sigmoid(seg(SEG_FF, N_PERM) + bf_ref[...])
    tril = (_iota((tm, tm), 0) >= _iota((tm, tm), 1)).astype(BF16)
    cum3 = jnp.dot(tril, parts3(lf).astype(BF16), preferred_element_type=F32)
    cum = (cum3 + pltpu.roll(cum3, HEAD_PAD - N_HEADS, 1) + pltpu.roll(cum3, HEAD_PAD - 2 * N_HEADS, 1)
           + carry_ref[...])
    carry_ref[...] = cum[tm - 1:tm, :]
    parts = (parts3(cum * LOG2E) + jnp.where(lane == 3 * N_HEADS, 1.0, 0.0)).astype(BF16)
    bias_q = jnp.dot(parts, selq_ref[...], preferred_element_type=F32)
    bias_k = jnp.dot(parts, selk_ref[...], preferred_element_type=F32)
    fox = seg(SEG_FOX, SEG_HG)
    fox_scale = HEAD_DIM ** -0.5 * LOG2E
    low = lane < HEAD_DIM
    for hd in range(N_HEADS):
        sl = slice(hd * HEAD_PAD, (hd + 1) * HEAD_PAD)
        src = slice((hd // 2) * HEAD_PAD, (hd // 2 + 1) * HEAD_PAD)

        def head_tile(x):
            tile = x[:, src]
            return pltpu.roll(tile, HEAD_DIM, 1) if hd % 2 else tile

        fq_ref[:, sl] = jnp.where(low, head_tile(fox[:, 0:GROUP_W]) * fox_scale, bias_q[:, sl]).astype(BF16)
        fk_ref[:, sl] = jnp.where(low, head_tile(fox[:, GROUP_W:2 * GROUP_W]), bias_k[:, sl]).astype(BF16)
        fv_ref[:, sl] = jnp.where(low, head_tile(fox[:, 2 * GROUP_W:]), 1.0).astype(BF16)

    hg_ref[...] = seg(SEG_HG, SEG_S5)
    su_ref[...] = seg(SEG_S5, SEG_FF)


def _permute_inproj(w):
    def zeros(n):
        return jnp.zeros(w.shape[:-1] + (n,), w.dtype)

    cols = [w[..., 0:384], zeros(ROPE_LANE0), w[..., 384:416], zeros(HEAD_PAD - ROPE_LANE0 - MLA_ROPE),
            w[..., 416:1184], w[..., 1188:N_IN], w[..., 1184:1188], zeros(HEAD_PAD - N_HEADS)]
    out = jnp.concatenate(cols, axis=-1)
    assert out.shape[-1] == N_PERM
    return out


def _bias_selectors():
    selq = np.zeros((HEAD_PAD, N_HEADS * HEAD_PAD), np.float32)
    selk = np.zeros((HEAD_PAD, N_HEADS * HEAD_PAD), np.float32)
    one = 3 * N_HEADS
    for hd in range(N_HEADS):
        for j in range(3):
            selq[N_HEADS * j + hd, hd * HEAD_PAD + BIAS_LANE0 + j] = 1.0
            selq[one, hd * HEAD_PAD + BIAS_LANE0 + 3 + j] = 1.0
            selk[one, hd * HEAD_PAD + BIAS_LANE0 + j] = 1.0
            selk[N_HEADS * j + hd, hd * HEAD_PAD + BIAS_LANE0 + 3 + j] = -1.0
    return jnp.asarray(selq, BF16), jnp.asarray(selk, BF16)


def _layer_block(a, layer):
    shape = a.shape[1:]
    return pl.BlockSpec((None,) + shape, lambda *_: (layer,) + (0,) * len(shape), pipeline_mode=pl.Buffered(1))


def _inproj_params(attn_g, w_in, q_g, w_uq, kv_g, w_ukv, b_f):
    depth = w_in.shape[0]
    wuq = jnp.pad(w_uq.reshape(depth, MLA_Q_RANK, N_HEADS, MLA_NOPE + MLA_ROPE),
                  ((0, 0), (0, 0), (0, 0), (0, HEAD_PAD - MLA_NOPE - MLA_ROPE)))
    wkv = w_ukv.reshape(depth, MLA_KV_RANK, N_HEADS, 2 * HEAD_DIM)
    head_pad = ((0, 0), (0, 0), (0, 0), (0, HEAD_PAD - HEAD_DIM))
    wk = jnp.pad(wkv[..., :MLA_NOPE], head_pad).reshape(depth, MLA_KV_RANK, -1)
    wv = jnp.pad(wkv[..., MLA_NOPE:], head_pad).reshape(depth, MLA_KV_RANK, -1)
    bf = jnp.pad(b_f.astype(F32), ((0, 0), (0, HEAD_PAD - N_HEADS)))

    def row(v):
        return v.reshape(depth, 1, -1)

    return (row(attn_g), _permute_inproj(w_in.astype(BF16)), row(q_g),
            wuq.reshape(depth, MLA_Q_RANK, -1).astype(BF16), row(kv_g),
            jnp.concatenate([wk, wv], axis=2).astype(BF16), row(bf))


def _inproj(h, seq, layer, stacked, tables, tm):
    t = h.shape[0]
    selq, selk = _bias_selectors()
    ct, s1, s2 = tables
    hp4 = N_HEADS * HEAD_PAD
    w_perm = stacked[1]

    def rows(width):
        return pl.BlockSpec((tm, width), lambda i: (i, 0))

    def whole(a):
        return pl.BlockSpec(a.shape, lambda i: (0,) * a.ndim)

    args = (h,) + tuple(stacked) + (ct, s1, s2, selq, selk)
    in_specs = ([rows(D_MODEL)] + [_layer_block(a, layer) for a in stacked]
                + [rows(HEAD_PAD)] * 3 + [whole(selq), whole(selk)])
    out_widths = (hp4,) * 6
    out_shape = tuple(jax.ShapeDtypeStruct((t, w), BF16) for w in out_widths) + (
        jax.ShapeDtypeStruct((t, 4 * GROUP_W), F32), jax.ShapeDtypeStruct((t, GROUP_W), F32))
    out_specs = tuple(rows(w) for w in out_widths) + (rows(4 * GROUP_W), rows(GROUP_W))
    vmem = _vmem_limit(_nbytes(w_perm.shape[1:], BF16), 2 * _nbytes((tm, D_MODEL), F32),
                       2 * sum(_nbytes((tm, w), BF16) for w in out_widths),
                       2 * _nbytes((tm, 5 * GROUP_W), F32), _nbytes((tm, N_PERM), F32),
                       _nbytes((tm, D_MODEL), F32))
    return pl.pallas_call(
        functools.partial(_inproj_kernel, seq // tm),
        out_shape=out_shape,
        grid=(t // tm,),
        in_specs=in_specs,
        out_specs=out_specs,
        scratch_shapes=[pltpu.VMEM((1, HEAD_PAD), F32)],
        name="inproj",
        compiler_params=pltpu.CompilerParams(dimension_semantics=("arbitrary",), vmem_limit_bytes=vmem),
    )(*args)


def _attn_kernel(tq, tkv, q_ref, k_ref, v_ref, o_ref, m_ref, acc_ref):
    qi = pl.program_id(1)
    m_ref[...] = jnp.full_like(m_ref, NEG_BIG)
    acc_ref[...] = jnp.zeros_like(acc_ref)
    per_q = tq // tkv

    def block(start, width, diag_offset):
        rows = pl.ds(pl.multiple_of(start, tkv), width)
        heads = [slice(hd * HEAD_PAD, (hd + 1) * HEAD_PAD) for hd in range(N_HEADS)]
        logits = [lax.dot_general(q_ref[:, sl], k_ref[rows, sl], (((1,), (1,)), ((), ())),
                                  preferred_element_type=F32) for sl in heads]
        probs, rescale = [], []
        for hd in range(N_HEADS):
            s = logits[hd]
            if diag_offset is not None:
                visible = _iota(s.shape, 1) + diag_offset <= _iota(s.shape, 0)
                s = jnp.where(visible, s, NEG_BIG)
            m_old = m_ref[hd]
            m_new = jnp.maximum(m_old, jnp.max(s, axis=1, keepdims=True))
            probs.append(jnp.exp2(s - jnp.concatenate([m_new] * (width // HEAD_PAD), axis=1)).astype(BF16))
            rescale.append(jnp.exp2(m_old - m_new))
            m_ref[hd] = m_new
        for hd, sl in enumerate(heads):
            acc_ref[hd] = (rescale[hd] * acc_ref[hd]
                           + jnp.dot(probs[hd], v_ref[rows, sl], preferred_element_type=F32))

    n_full = qi * per_q

    def wide_block(jj, carry):
        block(jj * (2 * tkv), 2 * tkv, None)
        return carry

    lax.fori_loop(0, n_full // 2, wide_block, 0)

    @pl.when(n_full % 2 == 1)
    def _():
        block((n_full - 1) * tkv, tkv, None)

    for r in range(per_q):
        block((n_full + r) * tkv, tkv, r * tkv)

    low = _iota((tq, HEAD_PAD), 1) < HEAD_DIM
    for pair in range(N_HEADS // 2):
        a0, a1 = acc_ref[2 * pair], acc_ref[2 * pair + 1]
        n0 = a0 / pltpu.roll(a0, HEAD_DIM, 1)
        n1 = a1 / pltpu.roll(a1, HEAD_DIM, 1)
        o_ref[:, pair * HEAD_PAD:(pair + 1) * HEAD_PAD] = jnp.where(low, n0, pltpu.roll(n1, HEAD_DIM, 1))


def _attention(q, k, v, batch, seq, tq, tkv):
    t = q.shape[0]
    hp4 = N_HEADS * HEAD_PAD
    vmem = _vmem_limit(2 * _nbytes((seq, hp4), BF16), 2 * _nbytes((tq, hp4), BF16),
                       2 * _nbytes((tq, GROUP_W), F32), 2 * N_HEADS * _nbytes((tq, HEAD_PAD), F32),
                       6 * _nbytes((tq, 2 * tkv), F32))
    resident = pl.BlockSpec((seq, hp4), lambda b, i: (b, 0), pipeline_mode=pl.Buffered(1))
    return pl.pallas_call(
        functools.partial(_attn_kernel, tq, tkv),
        out_shape=jax.ShapeDtypeStruct((t, GROUP_W), F32),
        grid=(batch, seq // tq),
        in_specs=[pl.BlockSpec((tq, hp4), lambda b, i: (b * (seq // tq) + i, 0)), resident, resident],
        out_specs=pl.BlockSpec((tq, GROUP_W), lambda b, i: (b * (seq // tq) + i, 0)),
        scratch_shapes=[pltpu.VMEM((N_HEADS, tq, HEAD_PAD), F32), pltpu.VMEM((N_HEADS, tq, HEAD_PAD), F32)],
        name="causal_attention",
        compiler_params=pltpu.CompilerParams(dimension_semantics=("arbitrary", "arbitrary"),
                                             vmem_limit_bytes=vmem),
    )(q, k, v)


def _hgrn_kernel(win, q0_ref, q1_ref, f0_ref, f1_ref, v0_ref, v1_ref, lb_ref, ee_ref, o_ref,
                 st_ref, bc_ref, kk_ref, vs_ref, qe_ref, ke_ref, od_ref, dec_ref, vt_ref):
    @pl.when(pl.program_id(1) == 0)
    def _():
        st_ref[...] = jnp.zeros_like(st_ref)

    q_ref, f_ref, v_ref = (q0_ref, q1_ref), (f0_ref, f1_ref), (v0_ref, v1_ref)
    r = q0_ref.shape[0]
    g = r // SUB
    lb = lb_ref[...]
    log_lb = jnp.log(lb)
    log_1m = jnp.log1p(-lb)

    def slab(halves, s):
        return jnp.concatenate([h[pl.ds(s, g, stride=SUB), :] for h in halves], axis=1)

    def put_rows(ref, s, x):
        for half in range(2):
            ref[half, pl.ds(s, g, stride=SUB), :] = x[:, half * V7X_LANES:(half + 1) * V7X_LANES]

    run = None
    for s in range(SUB):
        z = slab(f_ref, s)
        b = log_1m + _log_sigmoid(z)
        log_f = jnp.maximum(log_lb, b) + jnp.log1p(jnp.exp(-jnp.abs(log_lb - b)))
        run = log_f if s == 0 else run + log_f
        bc_ref[s] = run
        kk_ref[s] = (1.0 - lb) * _sigmoid(-z)
        vs_ref[s] = slab(v_ref, s)
    total = bc_ref[SUB - 1]
    dec_ref[...] = jnp.exp(total)

    for s in range(SUB):
        qs = slab(q_ref, s)
        bcs = bc_ref[s]
        prods = [(qs * kk_ref[j] * jnp.exp(bcs - bc_ref[j])).astype(BF16) for j in range(s)]
        prods.append((qs * kk_ref[s]).astype(BF16))
        red = jnp.dot(jnp.concatenate(prods, axis=0), ee_ref[...], preferred_element_type=F32)
        od = red[0:g] * vs_ref[0]
        for j in range(1, s + 1):
            od = od + red[j * g:(j + 1) * g] * vs_ref[j]
        put_rows(od_ref, s, od)
        put_rows(qe_ref, s, qs * jnp.exp(bcs))
        put_rows(ke_ref, s, kk_ref[s] * jnp.exp(total - bcs))
    for w in range(r // win):
        for half in range(2):
            vt_ref[w, half * V7X_LANES:(half + 1) * V7X_LANES, :] = (
                v_ref[half][w * win:(w + 1) * win, :].T.astype(BF16))

    hl = V7X_LANES
    same_head = (_iota((hl, hl), 0) // HEAD_DIM) == (_iota((hl, hl), 1) // HEAD_DIM)
    per_win = win // SUB

    def window(w, carry):
        halves = (0, 1)
        lanes = [slice(half * hl, (half + 1) * hl) for half in halves]

        def rows_of(c):
            return pl.ds(pl.multiple_of(w * win + c * SUB, SUB), SUB)

        upds = [[], []]
        for half in halves:
            vt = vt_ref[w, lanes[half], :]
            for c in range(per_win):
                pieces = [ke_ref[half, rows_of(c), :].astype(BF16)]
                if c:
                    pieces.insert(0, jnp.zeros((c * SUB, hl), BF16))
                if c < per_win - 1:
                    pieces.append(jnp.zeros(((per_win - 1 - c) * SUB, hl), BF16))
                upd = jnp.dot(vt, jnp.concatenate(pieces, axis=0), preferred_element_type=F32)
                upds[half].append(jnp.where(same_head, upd, 0.0))
        st = [st_ref[half] for half in halves]
        for c in range(per_win):
            rows = rows_of(c)
            dec = dec_ref[pl.ds(w * per_win + c, 1), :]
            for half in halves:
                o_state = lax.dot_general(qe_ref[half, rows, :].astype(BF16), st[half].astype(BF16),
                                          (((1,), (1,)), ((), ())), preferred_element_type=F32)
                o_ref[rows, lanes[half]] = od_ref[half, rows, :] + o_state
                st[half] = st[half] * dec[:, lanes[half]] + upds[half][c]
        for half in halves:
            st_ref[half] = st[half]
        return carry

    lax.fori_loop(0, r // win, window, 0)


def _hgrn(hg, layer, lb, batch, seq, r):
    t = hg.shape[0]
    win = min(r, V7X_LANES)
    ee = jnp.asarray(np.kron(np.eye(N_HEADS, dtype=np.float32), np.ones((HEAD_DIM, HEAD_DIM), np.float32)), BF16)
    nblk = seq // r

    def half(c):
        return pl.BlockSpec((r, V7X_LANES), lambda b, i: (b * nblk + i, c))

    tile = _nbytes((r, GROUP_W), F32)
    vmem = _vmem_limit(8 * tile, 7 * tile, 8 * tile)
    slabs = pltpu.VMEM((SUB, r // SUB, GROUP_W), F32)
    rows = pltpu.VMEM((2, r, V7X_LANES), F32)
    return pl.pallas_call(
        functools.partial(_hgrn_kernel, win),
        out_shape=jax.ShapeDtypeStruct((t, GROUP_W), F32),
        grid=(batch, nblk),
        in_specs=[half(0), half(1), half(2), half(3), half(4), half(5),
                  _layer_block(lb, layer),
                  pl.BlockSpec((GROUP_W, GROUP_W), lambda b, i: (0, 0))],
        out_specs=pl.BlockSpec((r, GROUP_W), lambda b, i: (b * nblk + i, 0)),
        scratch_shapes=[pltpu.VMEM((2, V7X_LANES, V7X_LANES), F32), slabs, slabs, slabs, rows, rows, rows,
                        pltpu.VMEM((r // SUB, GROUP_W), F32), pltpu.VMEM((r // win, GROUP_W, win), BF16)],
        name="hgrn2",
        compiler_params=pltpu.CompilerParams(dimension_semantics=("arbitrary", "arbitrary"),
                                             vmem_limit_bytes=vmem),
    )(hg, hg, hg, hg, hg, hg, lb, ee)


def _s5_local_kernel(u0_ref, u1_ref, bd_ref, bbd_ref, we_ref, d_ref, y_ref, e_ref, ua_ref):
    r = u0_ref.shape[0]
    g = r // SUB
    n = S5_GROUPS * S5_P
    for s in range(SUB):
        ua_ref[s * g:(s + 1) * g, :] = jnp.concatenate(
            [u0_ref[pl.ds(s, g, stride=SUB), :], u1_ref[pl.ds(s, g, stride=SUB), :]], axis=1)
    ua = ua_ref[...]
    ub = ua.astype(BF16)
    y_ref[...] = d_ref[...] * ua
    for j in range(SUB):
        y_ref[j * g:, :] += jnp.dot(ub[:(SUB - j) * g], bd_ref[j], preferred_element_type=F32)
    bu = jnp.dot(ub, bbd_ref[...], preferred_element_type=F32)
    er = jnp.zeros((g, n), F32)
    ei = jnp.zeros((g, n), F32)
    for s in range(SUB):
        bur, bui = bu[s * g:(s + 1) * g, :n], bu[s * g:(s + 1) * g, n:]
        wr, wi = we_ref[s:s + 1, :n], we_ref[s:s + 1, n:]
        er = er + (wr * bur - wi * bui)
        ei = ei + (wr * bui + wi * bur)
    e_ref[:, :n] = er
    e_ref[:, n:] = ei


def _s5_scan_kernel(e_ref, a_ref, x_ref):
    nchunk = e_ref.shape[0]
    n = S5_GROUPS * S5_P
    xr, xi = e_ref[:, :n], e_ref[:, n:]
    pr, pi = a_ref[:, :n], a_ref[:, n:]
    row = _iota((nchunk, n), 0)
    k = 1
    while k < nchunk:
        sr = jnp.where(row >= k, pltpu.roll(xr, k, 0), 0.0)
        si = jnp.where(row >= k, pltpu.roll(xi, k, 0), 0.0)
        xr, xi = xr + pr * sr - pi * si, xi + pr * si + pi * sr
        pr, pi = pr * pr - pi * pi, 2.0 * pr * pi
        k *= 2
    x_ref[:, :n] = jnp.where(row >= 1, pltpu.roll(xr, 1, 0), 0.0)
    x_ref[:, n:] = jnp.where(row >= 1, pltpu.roll(xi, 1, 0), 0.0)


def _s5_out_kernel(y1_ref, x_ref, a1_ref, cbd_ref, wg_ref, bg_ref, o_ref, xs_ref, nat_ref):
    r = y1_ref.shape[0]
    g = r // SUB
    n = S5_GROUPS * S5_P
    xr, xi = x_ref[:, :n], x_ref[:, n:]
    for s in range(SUB):
        ar, ai = a1_ref[s:s + 1, :n], a1_ref[s:s + 1, n:]
        xs_ref[s * g:(s + 1) * g, :n] = (ar * xr - ai * xi).astype(BF16)
        xs_ref[s * g:(s + 1) * g, n:] = (ar * xi + ai * xr).astype(BF16)
    y = y1_ref[...] + jnp.dot(xs_ref[...], cbd_ref[...], preferred_element_type=F32)
    zact = 0.5 * y * (1.0 + jnp.tanh(math.sqrt(2.0 / math.pi) * (y + 0.044715 * (y * y * y))))
    gate = jnp.dot(zact.astype(BF16), wg_ref[...], preferred_element_type=F32) + bg_ref[...]
    out = zact * _sigmoid(gate)
    for s in range(SUB):
        for half in range(2):
            nat_ref[half, pl.ds(s, g, stride=SUB), :] = out[s * g:(s + 1) * g,
                                                            half * V7X_LANES:(half + 1) * V7X_LANES]
    for half in range(2):
        o_ref[:, half * V7X_LANES:(half + 1) * V7X_LANES] = nat_ref[half]


def _s5_prep_kernel(ar_ref, ai_ref, bre_ref, bim_ref, cre_ref, cim_ref,
                    bd_ref, bbd_ref, cbd_ref, wend_ref, ain_ref, achunk_ref):
    n = S5_GROUPS * S5_P
    ar, ai = ar_ref[...], ai_ref[...]
    bre, bim, cre, cim = bre_ref[...], bim_ref[...], cre_ref[...], cim_ref[...]
    bbd_ref[:, :n] = bre.astype(BF16)
    bbd_ref[:, n:] = bim.astype(BF16)
    cbd_ref[:n, :] = cre.astype(BF16)
    cbd_ref[n:, :] = (-cim).astype(BF16)

    def split2(x):
        hi = x.astype(BF16)
        return hi, (x - hi.astype(F32)).astype(BF16)

    def dot3(x, w_hi, w_lo):
        x_hi, x_lo = split2(x)
        return (jnp.dot(x_hi, w_hi, preferred_element_type=F32) + jnp.dot(x_hi, w_lo, preferred_element_type=F32)
                + jnp.dot(x_lo, w_hi, preferred_element_type=F32))

    cre_parts, cim_parts = split2(cre), split2(cim)
    pr, pi = jnp.ones_like(ar), jnp.zeros_like(ar)
    for j in range(SUB):
        k_lag = dot3(bre * pr - bim * pi, *cre_parts) - dot3(bre * pi + bim * pr, *cim_parts)
        bd_ref[j] = k_lag.astype(BF16)
        wend_ref[SUB - 1 - j:SUB - j, :n] = pr
        wend_ref[SUB - 1 - j:SUB - j, n:] = pi
        pr, pi = pr * ar - pi * ai, pr * ai + pi * ar
        ain_ref[j:j + 1, :n] = pr
        ain_ref[j:j + 1, n:] = pi
    achunk_ref[:, :n] = pr
    achunk_ref[:, n:] = pi


def _s5_operators(lam_re, lam_im, log_step, b_re, b_im, c_re, c_im):
    depth = lam_re.shape[0]
    step = jnp.exp(log_step.astype(F32))[..., None]
    lre = jnp.minimum(lam_re.astype(F32), -1e-4)
    lim = lam_im.astype(F32)
    mag = jnp.exp(lre * step)
    a_re, a_im = mag * jnp.cos(lim * step), mag * jnp.sin(lim * step)
    den = lre * lre + lim * lim
    coef_re = ((a_re - 1.0) * lre + a_im * lim) / den
    coef_im = (a_im * lre - (a_re - 1.0) * lim) / den
    br, bi = b_re.astype(F32), b_im.astype(F32)
    bb_re = coef_re[..., None] * br - coef_im[..., None] * bi
    bb_im = coef_re[..., None] * bi + coef_im[..., None] * br
    cr, ci = c_re.astype(F32), c_im.astype(F32)
    n = S5_GROUPS * S5_P
    same_group = (np.arange(GROUP_W)[:, None] // S5_CH) == (np.arange(n)[None, :] // S5_P)
    in_mask = jnp.asarray(same_group, F32)
    out_mask = jnp.asarray(same_group.T, F32)

    def in_side(b):
        return jnp.tile(b.transpose(0, 1, 3, 2).reshape(depth, GROUP_W, S5_P), (1, 1, S5_GROUPS)) * in_mask

    def out_side(c):
        return jnp.tile(c.transpose(0, 1, 3, 2).reshape(depth, n, S5_CH), (1, 1, S5_GROUPS)) * out_mask

    args = (a_re.reshape(depth, 1, n), a_im.reshape(depth, 1, n), in_side(bb_re), in_side(bb_im),
            out_side(cr), out_side(ci))
    out_dims = (((SUB, GROUP_W, GROUP_W), BF16),
                ((GROUP_W, 2 * n), BF16),
                ((2 * n, GROUP_W), BF16),
                ((SUB, 2 * n), F32),
                ((SUB, 2 * n), F32),
                ((1, 2 * n), F32))

    def per_layer(shape):
        return pl.BlockSpec((None,) + shape, lambda l: (l,) + (0,) * len(shape))

    return pl.pallas_call(
        _s5_prep_kernel,
        out_shape=tuple(jax.ShapeDtypeStruct((depth,) + s, d) for s, d in out_dims),
        grid=(depth,),
        in_specs=[per_layer(a.shape[1:]) for a in args],
        out_specs=tuple(per_layer(s) for s, _ in out_dims),
        name="s5_prep",
        compiler_params=pltpu.CompilerParams(
            dimension_semantics=("arbitrary",),
            vmem_limit_bytes=_vmem_limit(8 * _nbytes((GROUP_W, n), F32), 10 * _nbytes((GROUP_W, n), F32),
                                         4 * _nbytes((SUB, GROUP_W, GROUP_W), BF16))),
    )(*args)


def _s5(u, layer, ops, d_skip, w_glu, b_glu, batch, seq, r):
    t = u.shape[0]
    bd, bbd, cbd, w_end, a_in, a_chunk = ops
    n2 = 2 * S5_GROUPS * S5_P
    nck = seq // SUB

    def whole(a):
        return _layer_block(a, layer)

    rows = pl.BlockSpec((r, GROUP_W), lambda i: (i, 0))
    crow = pl.BlockSpec((r // SUB, n2), lambda i: (i, 0))
    y1, e = pl.pallas_call(
        _s5_local_kernel,
        out_shape=(jax.ShapeDtypeStruct((t, GROUP_W), F32), jax.ShapeDtypeStruct((t // SUB, n2), F32)),
        grid=(t // r,),
        in_specs=[pl.BlockSpec((r, V7X_LANES), lambda i: (i, 0)), pl.BlockSpec((r, V7X_LANES), lambda i: (i, 1)),
                  whole(bd), whole(bbd), whole(w_end), whole(d_skip)],
        out_specs=(rows, crow),
        scratch_shapes=[pltpu.VMEM((r, GROUP_W), F32)],
        name="s5_local",
        compiler_params=pltpu.CompilerParams(
            dimension_semantics=("arbitrary",),
            vmem_limit_bytes=_vmem_limit(_nbytes(bd.shape[1:], BF16), _nbytes(bbd.shape[1:], BF16),
                                         6 * _nbytes((r, n2), F32), 8 * _nbytes((r, GROUP_W), F32))),
    )(u, u, bd, bbd, w_end, d_skip)
    xin = pl.pallas_call(
        _s5_scan_kernel,
        out_shape=jax.ShapeDtypeStruct((t // SUB, n2), F32),
        grid=(batch,),
        in_specs=[pl.BlockSpec((nck, n2), lambda b: (b, 0)), whole(a_chunk)],
        out_specs=pl.BlockSpec((nck, n2), lambda b: (b, 0)),
        name="s5_scan",
        compiler_params=pltpu.CompilerParams(
            dimension_semantics=("arbitrary",), vmem_limit_bytes=_vmem_limit(10 * _nbytes((nck, n2), F32))),
    )(e, a_chunk)
    return pl.pallas_call(
        _s5_out_kernel,
        out_shape=jax.ShapeDtypeStruct((t, GROUP_W), F32),
        grid=(t // r,),
        in_specs=[rows, crow, whole(a_in), whole(cbd), whole(w_glu), whole(b_glu)],
        out_specs=rows,
        scratch_shapes=[pltpu.VMEM((r, n2), BF16), pltpu.VMEM((2, r, V7X_LANES), F32)],
        name="s5_out",
        compiler_params=pltpu.CompilerParams(
            dimension_semantics=("arbitrary",),
            vmem_limit_bytes=_vmem_limit(_nbytes(cbd.shape[1:], BF16), 3 * _nbytes((r, n2), BF16),
                                         10 * _nbytes((r, GROUP_W), F32))),
    )(y1, xin, a_in, cbd, w_glu, b_glu)


HALO = 2 * V7X_SUBLANES


def _post_kernel(tiles_per_seq, final, ya_ref, yb_ref, yc_ref, gate_ref, yd_ref, h_ref, p_ref,
                 gn_ref, wo_ref, fg_ref, wu_ref, cw_ref, cb_ref, wd_ref, pg_ref, wpg_ref, wp_ref, ng_ref,
                 o_ref, xn_ref, up_ref, act_ref):
    tm = h_ref.shape[0]
    gw = GROUP_W

    @pl.when(pl.program_id(0) % tiles_per_seq == 0)
    def _():
        xn_ref[0:HALO, :] = jnp.zeros((HALO, D_MODEL), BF16)

    parts = (_rms(ya_ref[...], gn_ref[:, 0:gw]),
             _rms(yb_ref[...], gn_ref[:, gw:2 * gw]),
             _rms(yc_ref[...], gn_ref[:, 2 * gw:3 * gw]) * _sigmoid(gate_ref[...]),
             _rms(yd_ref[...], gn_ref[:, 3 * gw:4 * gw]))
    h1 = h_ref[...]
    for g, part in enumerate(parts):
        h1 = h1 + jnp.dot(part.astype(BF16), wo_ref[g * gw:(g + 1) * gw, :], preferred_element_type=F32)
    o_ref[...] = h1
    xn_ref[HALO:, :] = _rms(h1, fg_ref[...]).astype(BF16)

    nchunk = D_FF // MXU_TILE

    def conv(slot, c):
        cols = slice(c * MXU_TILE, (c + 1) * MXU_TILE)
        up_ref[slot] = jnp.dot(xn_ref[...], wu_ref[:, cols], preferred_element_type=F32)
        return (cb_ref[:, cols] + cw_ref[0:1, cols] * up_ref[slot, pl.ds(HALO - 2, tm), :]
                + cw_ref[1:2, cols] * up_ref[slot, pl.ds(HALO - 1, tm), :]
                + cw_ref[2:3, cols] * up_ref[slot, pl.ds(HALO, tm), :])

    for c in range(nchunk):
        slot = 2 * (c % 2)
        gate = conv(slot, c)
        val = conv(slot + 1, c + nchunk)
        act_ref[:, c * MXU_TILE:(c + 1) * MXU_TILE] = (gate * _sigmoid(gate) * val).astype(BF16)
    xn_ref[0:HALO, :] = xn_ref[tm:tm + HALO, :]
    h2 = o_ref[...] + jnp.dot(act_ref[...], wd_ref[...], preferred_element_type=F32)

    pgate = _sigmoid(jnp.dot(_rms(h2, pg_ref[...]).astype(BF16), wpg_ref[...], preferred_element_type=F32))
    out = h2 + pgate * jnp.dot(p_ref[...].astype(BF16), wp_ref[...], preferred_element_type=F32)
    o_ref[...] = _rms(out, ng_ref[...]) if final else out


def _post(ya, yb, yc, hg, yd, h, p, seq, layer, stacked, final, tm):
    t = h.shape[0]
    rows = pl.BlockSpec((tm, GROUP_W), lambda i: (i, 0))
    wide = pl.BlockSpec((tm, D_MODEL), lambda i: (i, 0))
    consts = tuple(stacked)
    vmem = _vmem_limit(sum(_nbytes(c.shape[1:], c.dtype) for c in consts), 12 * _nbytes((tm, GROUP_W), F32),
                       4 * _nbytes((tm, D_MODEL), F32), _nbytes((tm + HALO, D_MODEL), BF16),
                       4 * _nbytes((tm + HALO, MXU_TILE), F32), _nbytes((tm, D_FF), BF16),
                       3 * _nbytes((tm, D_MODEL), F32))
    return pl.pallas_call(
        functools.partial(_post_kernel, seq // tm, final),
        out_shape=jax.ShapeDtypeStruct((t, D_MODEL), F32),
        grid=(t // tm,),
        in_specs=[rows, rows, rows, pl.BlockSpec((tm, GROUP_W), lambda i: (i, 3)), rows, wide,
                  pl.BlockSpec((tm, PLE_DIM), lambda i: (layer * (t // tm) + i, 0))]
        + [_layer_block(c, layer) for c in consts],
        out_specs=wide,
        scratch_shapes=[pltpu.VMEM((tm + HALO, D_MODEL), BF16), pltpu.VMEM((4, tm + HALO, MXU_TILE), F32),
                        pltpu.VMEM((tm, D_FF), BF16)],
        name="post_mixer",
        compiler_params=pltpu.CompilerParams(dimension_semantics=("arbitrary",), vmem_limit_bytes=vmem),
    )(ya, yb, yc, hg, yd, h, p, *consts)


def _tiles(seq):
    tm = min(512, seq)
    return dict(tm=tm, tq=min(512, seq), tkv=min(512, seq), r=min(512, seq))


def kernel(x, p, positions, attn_norm_g, w_in, mla_q_norm_g, mla_w_uq, mla_kv_norm_g, mla_w_ukv, fox_b_f,
           hgrn_lb_param, s5_lam_re, s5_lam_im, s5_log_step, s5_b_re, s5_b_im, s5_c_re, s5_c_im, s5_d,
           s5_w_glu, s5_b_glu, group_norm_g, w_out, ffn_norm_g, w_up, conv_w, conv_b, w_down, ple_norm_g,
           w_ple_gate, w_ple, final_norm_g):
    batch, seq, _ = x.shape
    depth = w_in.shape[0]
    t = batch * seq
    ts = _tiles(seq)
    assert seq % ts["tm"] == 0 and seq % ts["tq"] == 0 and ts["tq"] % ts["tkv"] == 0 and seq % ts["r"] == 0
    assert (seq // SUB) & (seq // SUB - 1) == 0, "chunk scan assumes a power-of-two chunk count"

    lb_all = jnp.cumsum(jax.nn.softmax(hgrn_lb_param.astype(F32), axis=0), axis=0)
    lb_all = lb_all - lb_all[0:1]
    tables = _rope_tables(positions, ts["tm"])

    def row(v):
        return v.reshape(v.shape[0], 1, -1)

    in_params = _inproj_params(attn_norm_g, w_in, mla_q_norm_g, mla_w_uq, mla_kv_norm_g, mla_w_ukv, fox_b_f)
    s5_ops = _s5_operators(s5_lam_re, s5_lam_im, s5_log_step, s5_b_re, s5_b_im, s5_c_re, s5_c_im)
    s5_tail = (row(s5_d.astype(F32)), s5_w_glu.astype(BF16), row(s5_b_glu.astype(F32)))
    post_params = (row(group_norm_g), w_out.astype(BF16), row(ffn_norm_g), w_up.astype(BF16), conv_w,
                   row(conv_b), w_down.astype(BF16), row(ple_norm_g), w_ple_gate.astype(BF16),
                   w_ple.astype(BF16), jnp.broadcast_to(final_norm_g.reshape(1, 1, -1), (depth, 1, D_MODEL)))
    h = x.reshape(t, D_MODEL)
    for i in range(depth):
        mq, mk, mv, fq, fk, fv, hg, su = _inproj(h, seq, i, in_params, tables, ts["tm"])
        y_a = _attention(mq, mk, mv, batch, seq, ts["tq"], ts["tkv"])
        y_b = _attention(fq, fk, fv, batch, seq, ts["tq"], ts["tkv"])
        y_c = _hgrn(hg, i, row(lb_all), batch, seq, ts["r"])
        y_d = _s5(su, i, s5_ops, *s5_tail, batch, seq, ts["r"])
        h = _post(y_a, y_b, y_c, hg, y_d, h, p.reshape(depth * t, PLE_DIM), seq, i, post_params,
                  i == depth - 1, ts["tm"])
    return h.reshape(batch, seq, D_MODEL)
```

```python
import functools
import math

import numpy as np
import jax
import jax.numpy as jnp
from jax import lax
from jax.experimental import pallas as pl
from jax.experimental.pallas import tpu as pltpu

F32 = jnp.float32
BF16 = jnp.bfloat16

D_MODEL = 1024
N_HEADS = 4
HEAD_DIM = 64
GROUP_W = 256
MLA_Q_RANK = 256
MLA_KV_RANK = 128
MLA_NOPE = 64
MLA_ROPE = 32
ROPE_THETA = 10000.0
S5_GROUPS = 16
S5_CH = 16
S5_P = 64
D_FF = 2816
PLE_DIM = 256
EPS = 1e-6
N_IN = 2468

V7X_LANES = 128
V7X_SUBLANES = 8
V7X_VMEM_BYTES = 64 * 1024 * 1024
MXU_TILE = 256
VMEM_CAP_BYTES = 58 * 1024 * 1024

HEAD_PAD = V7X_LANES
SUB = 16
NEG_BIG = -1e30

SEG_CQ = 0
SEG_CKV = 256
SEG_KR = 384
SEG_FOX = 512
SEG_HG = 1280
SEG_S5 = 2304
SEG_FF = 2560
N_PERM = 2688
ROPE_LANE0 = 64
BIAS_LANE0 = 64
LOG2E = math.log2(math.e)


def _vmem_limit(*byte_counts):
    need = int(sum(byte_counts))
    return int(min(VMEM_CAP_BYTES, need + need // 4 + (4 << 20)))


def _nbytes(shape, dtype):
    return int(np.prod(shape)) * jnp.dtype(dtype).itemsize


def _rms(x, g):
    return x * lax.rsqrt(jnp.mean(x * x, axis=-1, keepdims=True) + EPS) * g


def _log_sigmoid(z):
    return jnp.minimum(z, 0.0) - jnp.log1p(jnp.exp(-jnp.abs(z)))


def _sigmoid(z):
    return 1.0 / (1.0 + jnp.exp(-z))


def _iota(shape, dim):
    return lax.broadcasted_iota(jnp.int32, shape, dim)


def _rope_kernel(pos_ref, freq_ref, ct_ref, s1_ref, s2_ref):
    ang = pos_ref[...].astype(F32) * freq_ref[...]
    lane = _iota(ang.shape, 1)
    half = MLA_ROPE // 2
    sin = jnp.sin(ang)
    ct_ref[...] = jnp.cos(ang)
    s1_ref[...] = jnp.where((lane >= ROPE_LANE0) & (lane < ROPE_LANE0 + half), -sin, 0.0)
    s2_ref[...] = jnp.where((lane >= ROPE_LANE0 + half) & (lane < ROPE_LANE0 + 2 * half), sin, 0.0)


def _rope_tables(positions, tm):
    t = positions.size
    half = MLA_ROPE // 2
    inv_freq = ROPE_THETA ** (-jnp.arange(half, dtype=F32) / half)
    freq = jnp.zeros((1, HEAD_PAD), F32).at[0, ROPE_LANE0:ROPE_LANE0 + 2 * half].set(jnp.tile(inv_freq, 2))
    pos = positions.reshape(t, 1)
    out = jax.ShapeDtypeStruct((t, HEAD_PAD), F32)
    spec = pl.BlockSpec((tm, HEAD_PAD), lambda i: (i, 0))
    return pl.pallas_call(
        _rope_kernel,
        out_shape=(out, out, out),
        grid=(t // tm,),
        in_specs=[pl.BlockSpec((tm, 1), lambda i: (i, 0)), pl.BlockSpec((1, HEAD_PAD), lambda i: (0, 0))],
        out_specs=(spec, spec, spec),
        name="rope_tables",
        compiler_params=pltpu.CompilerParams(dimension_semantics=("arbitrary",)),
    )(pos, freq)


def _inproj_kernel(tiles_per_seq, h_ref, g_ref, w_ref, qg_ref, wuq_ref, kvg_ref, wukv_ref, bf_ref,
                   ct_ref, s1_ref, s2_ref, selq_ref, selk_ref,
                   mq_ref, mk_ref, mv_ref, fq_ref, fk_ref, fv_ref, hg_ref, su_ref, carry_ref):
    i = pl.program_id(0)

    @pl.when(i % tiles_per_seq == 0)
    def _():
        carry_ref[...] = jnp.zeros_like(carry_ref)

    tm = h_ref.shape[0]
    xn = _rms(h_ref[...], g_ref[...]).astype(BF16)

    proj = jnp.dot(xn, w_ref[...], preferred_element_type=F32)

    def seg(a, b):
        return proj[:, a:b]

    q = jnp.dot(_rms(seg(SEG_CQ, SEG_CKV), qg_ref[...]).astype(BF16), wuq_ref[...],
                preferred_element_type=F32)
    ckv_kr = seg(SEG_CKV, SEG_FOX)
    kv = jnp.dot(_rms(ckv_kr[:, :MLA_KV_RANK], kvg_ref[...]).astype(BF16), wukv_ref[...],
                 preferred_element_type=F32)
    ct, s1, s2 = ct_ref[...], s1_ref[...], s2_ref[...]
    half = MLA_ROPE // 2

    def rope(t):
        return t * ct + pltpu.roll(t, HEAD_PAD - half, 1) * s1 + pltpu.roll(t, half, 1) * s2

    k_pe = rope(ckv_kr[:, MLA_KV_RANK:])
    mla_scale = (MLA_NOPE + MLA_ROPE) ** -0.5 * LOG2E
    for hd in range(N_HEADS):
        sl = slice(hd * HEAD_PAD, (hd + 1) * HEAD_PAD)
        mq_ref[:, sl] = (rope(q[:, sl]) * mla_scale).astype(BF16)
        mk_ref[:, sl] = (kv[:, sl] + k_pe).astype(BF16)
    hp4 = N_HEADS * HEAD_PAD
    ones_pad = jnp.where((_iota((1, hp4), 1) & (HEAD_PAD - 1)) >= HEAD_DIM, 1.0, 0.0)
    mv_ref[...] = (kv[:, hp4:] + ones_pad).astype(BF16)

    lane = _iota((tm, HEAD_PAD), 1)
    keep = lane < N_HEADS

    def parts3(x):
        a = x.astype(BF16).astype(F32)
        r = x - a
        b = r.astype(BF16).astype(F32)
        c = (r - b).astype(BF16).astype(F32)
        return (jnp.where(keep, a, 0.0) + pltpu.roll(jnp.where(keep, b, 0.0), N_HEADS, 1)
                + pltpu.roll(jnp.where(keep, c, 0.0), 2 * N_HEADS, 1))

    lf = _log_sigmoid(seg(SEG_FF, N_PERM) + bf_ref[...])
    tril = (_iota((tm, tm), 0) >= _iota((tm, tm), 1)).astype(BF16)
    cum3 = jnp.dot(tril, parts3(lf).astype(BF16), preferred_element_type=F32)
    cum = (cum3 + pltpu.roll(cum3, HEAD_PAD - N_HEADS, 1) + pltpu.roll(cum3, HEAD_PAD - 2 * N_HEADS, 1)
           + carry_ref[...])
    carry_ref[...] = cum[tm - 1:tm, :]
    parts = (parts3(cum * LOG2E) + jnp.where(lane == 3 * N_HEADS, 1.0, 0.0)).astype(BF16)
    bias_q = jnp.dot(parts, selq_ref[...], preferred_element_type=F32)
    bias_k = jnp.dot(parts, selk_ref[...], preferred_element_type=F32)
    fox = seg(SEG_FOX, SEG_HG)
    fox_scale = HEAD_DIM ** -0.5 * LOG2E
    low = lane < HEAD_DIM
    for hd in range(N_HEADS):
        sl = slice(hd * HEAD_PAD, (hd + 1) * HEAD_PAD)
        src = slice((hd // 2) * HEAD_PAD, (hd // 2 + 1) * HEAD_PAD)

        def head_tile(x):
            tile = x[:, src]
            return pltpu.roll(tile, HEAD_DIM, 1) if hd % 2 else tile

        fq_ref[:, sl] = jnp.where(low, head_tile(fox[:, 0:GROUP_W]) * fox_scale, bias_q[:, sl]).astype(BF16)
        fk_ref[:, sl] = jnp.where(low, head_tile(fox[:, GROUP_W:2 * GROUP_W]), bias_k[:, sl]).astype(BF16)
        fv_ref[:, sl] = jnp.where(low, head_tile(fox[:, 2 * GROUP_W:]), 1.0).astype(BF16)

    hg_ref[...] = seg(SEG_HG, SEG_S5)
    su_ref[...] = seg(SEG_S5, SEG_FF)


def _permute_inproj(w):
    def zeros(n):
        return jnp.zeros(w.shape[:-1] + (n,), w.dtype)

    cols = [w[..., 0:384], zeros(ROPE_LANE0), w[..., 384:416], zeros(HEAD_PAD - ROPE_LANE0 - MLA_ROPE),
            w[..., 416:1184], w[..., 1188:N_IN], w[..., 1184:1188], zeros(HEAD_PAD - N_HEADS)]
    out = jnp.concatenate(cols, axis=-1)
    assert out.shape[-1] == N_PERM
    return out


def _bias_selectors():
    selq = np.zeros((HEAD_PAD, N_HEADS * HEAD_PAD), np.float32)
    selk = np.zeros((HEAD_PAD, N_HEADS * HEAD_PAD), np.float32)
    one = 3 * N_HEADS
    for hd in range(N_HEADS):
        for j in range(3):
            selq[N_HEADS * j + hd, hd * HEAD_PAD + BIAS_LANE0 + j] = 1.0
            selq[one, hd * HEAD_PAD + BIAS_LANE0 + 3 + j] = 1.0
            selk[one, hd * HEAD_PAD + BIAS_LANE0 + j] = 1.0
            selk[N_HEADS * j + hd, hd * HEAD_PAD + BIAS_LANE0 + 3 + j] = -1.0
    return jnp.asarray(selq, BF16), jnp.asarray(selk, BF16)


def _layer_block(a, layer):
    shape = a.shape[1:]
    return pl.BlockSpec((None,) + shape, lambda *_: (layer,) + (0,) * len(shape), pipeline_mode=pl.Buffered(1))


def _inproj_params(attn_g, w_in, q_g, w_uq, kv_g, w_ukv, b_f):
    depth = w_in.shape[0]
    wuq = jnp.pad(w_uq.reshape(depth, MLA_Q_RANK, N_HEADS, MLA_NOPE + MLA_ROPE),
                  ((0, 0), (0, 0), (0, 0), (0, HEAD_PAD - MLA_NOPE - MLA_ROPE)))
    wkv = w_ukv.reshape(depth, MLA_KV_RANK, N_HEADS, 2 * HEAD_DIM)
    head_pad = ((0, 0), (0, 0), (0, 0), (0, HEAD_PAD - HEAD_DIM))
    wk = jnp.pad(wkv[..., :MLA_NOPE], head_pad).reshape(depth, MLA_KV_RANK, -1)
    wv = jnp.pad(wkv[..., MLA_NOPE:], head_pad).reshape(depth, MLA_KV_RANK, -1)
    bf = jnp.pad(b_f.astype(F32), ((0, 0), (0, HEAD_PAD - N_HEADS)))

    def row(v):
        return v.reshape(depth, 1, -1)

    return (row(attn_g), _permute_inproj(w_in.astype(BF16)), row(q_g),
            wuq.reshape(depth, MLA_Q_RANK, -1).astype(BF16), row(kv_g),
            jnp.concatenate([wk, wv], axis=2).astype(BF16), row(bf))


def _inproj(h, seq, layer, stacked, tables, tm):
    t = h.shape[0]
    selq, selk = _bias_selectors()
    ct, s1, s2 = tables
    hp4 = N_HEADS * HEAD_PAD
    w_perm = stacked[1]

    def rows(width):
        return pl.BlockSpec((tm, width), lambda i: (i, 0))

    def whole(a):
        return pl.BlockSpec(a.shape, lambda i: (0,) * a.ndim)

    args = (h,) + tuple(stacked) + (ct, s1, s2, selq, selk)
    in_specs = ([rows(D_MODEL)] + [_layer_block(a, layer) for a in stacked]
                + [rows(HEAD_PAD)] * 3 + [whole(selq), whole(selk)])
    out_widths = (hp4,) * 6
    out_shape = tuple(jax.ShapeDtypeStruct((t, w), BF16) for w in out_widths) + (
        jax.ShapeDtypeStruct((t, 4 * GROUP_W), F32), jax.ShapeDtypeStruct((t, GROUP_W), F32))
    out_specs = tuple(rows(w) for w in out_widths) + (rows(4 * GROUP_W), rows(GROUP_W))
    vmem = _vmem_limit(_nbytes(w_perm.shape[1:], BF16), 2 * _nbytes((tm, D_MODEL), F32),
                       2 * sum(_nbytes((tm, w), BF16) for w in out_widths),
                       2 * _nbytes((tm, 5 * GROUP_W), F32), _nbytes((tm, N_PERM), F32),
                       _nbytes((tm, D_MODEL), F32))
    return pl.pallas_call(
        functools.partial(_inproj_kernel, seq // tm),
        out_shape=out_shape,
        grid=(t // tm,),
        in_specs=in_specs,
        out_specs=out_specs,
        scratch_shapes=[pltpu.VMEM((1, HEAD_PAD), F32)],
        name="inproj",
        compiler_params=pltpu.CompilerParams(dimension_semantics=("arbitrary",), vmem_limit_bytes=vmem),
    )(*args)


def _attn_kernel(tq, tkv, q_ref, k_ref, v_ref, o_ref, m_ref, acc_ref):
    qi = pl.program_id(1)
    m_ref[...] = jnp.full_like(m_ref, NEG_BIG)
    acc_ref[...] = jnp.zeros_like(acc_ref)
    per_q = tq // tkv
    heads = [slice(hd * HEAD_PAD, (hd + 1) * HEAD_PAD) for hd in range(N_HEADS)]

    def key_rows(start, width):
        return pl.ds(pl.multiple_of(start, tkv), width)

    def logits_of(start, width):
        rows = key_rows(start, width)
        return [lax.dot_general(q_ref[:, sl], k_ref[rows, sl], (((1,), (1,)), ((), ())),
                                preferred_element_type=F32) for sl in heads]

    def finish(logits, start, width, diag_offset):
        rows = key_rows(start, width)
        probs, rescale = [], []
        for hd in range(N_HEADS):
            s = logits[hd]
            if diag_offset is not None:
                visible = _iota(s.shape, 1) + diag_offset <= _iota(s.shape, 0)
                s = jnp.where(visible, s, NEG_BIG)
            m_old = m_ref[hd]
            m_new = jnp.maximum(m_old, jnp.max(s, axis=1, keepdims=True))
            probs.append(jnp.exp2(s - jnp.concatenate([m_new] * (width // HEAD_PAD), axis=1)).astype(BF16))
            rescale.append(jnp.exp2(m_old - m_new))
            m_ref[hd] = m_new
        for hd, sl in enumerate(heads):
            acc_ref[hd] = (rescale[hd] * acc_ref[hd]
                           + jnp.dot(probs[hd], v_ref[rows, sl], preferred_element_type=F32))

    def block(start, width, diag_offset):
        finish(logits_of(start, width), start, width, diag_offset)

    n_full = qi * per_q

    def wide_block(jj, carry):
        block(jj * (3 * tkv), 3 * tkv, None)
        return carry

    n_wide = n_full // 3
    lax.fori_loop(0, n_wide, wide_block, 0)
    left = n_full - 3 * n_wide

    @pl.when(left == 2)
    def _():
        block((n_full - 2) * tkv, 2 * tkv, None)

    @pl.when(left == 1)
    def _():
        block((n_full - 1) * tkv, tkv, None)

    for r in range(per_q):
        block((n_full + r) * tkv, tkv, r * tkv)

    low = _iota((tq, HEAD_PAD), 1) < HEAD_DIM
    for pair in range(N_HEADS // 2):
        a0, a1 = acc_ref[2 * pair], acc_ref[2 * pair + 1]
        n0 = a0 / pltpu.roll(a0, HEAD_DIM, 1)
        n1 = a1 / pltpu.roll(a1, HEAD_DIM, 1)
        o_ref[:, pair * HEAD_PAD:(pair + 1) * HEAD_PAD] = jnp.where(low, n0, pltpu.roll(n1, HEAD_DIM, 1))


def _attention(q, k, v, batch, seq, tq, tkv):
    t = q.shape[0]
    hp4 = N_HEADS * HEAD_PAD
    vmem = _vmem_limit(2 * _nbytes((seq, hp4), BF16), 2 * _nbytes((tq, hp4), BF16),
                       2 * _nbytes((tq, GROUP_W), F32), 2 * N_HEADS * _nbytes((tq, HEAD_PAD), F32),
                       6 * _nbytes((tq, 2 * tkv), F32))
    resident = pl.BlockSpec((seq, hp4), lambda b, i: (b, 0), pipeline_mode=pl.Buffered(1))
    return pl.pallas_call(
        functools.partial(_attn_kernel, tq, tkv),
        out_shape=jax.ShapeDtypeStruct((t, GROUP_W), F32),
        grid=(batch, seq // tq),
        in_specs=[pl.BlockSpec((tq, hp4), lambda b, i: (b * (seq // tq) + i, 0)), resident, resident],
        out_specs=pl.BlockSpec((tq, GROUP_W), lambda b, i: (b * (seq // tq) + i, 0)),
        scratch_shapes=[pltpu.VMEM((N_HEADS, tq, HEAD_PAD), F32), pltpu.VMEM((N_HEADS, tq, HEAD_PAD), F32)],
        name="causal_attention",
        compiler_params=pltpu.CompilerParams(dimension_semantics=("arbitrary", "arbitrary"),
                                             vmem_limit_bytes=vmem),
    )(q, k, v)


def _hgrn_kernel(win, q0_ref, q1_ref, f0_ref, f1_ref, v0_ref, v1_ref, lb_ref, ee_ref, o_ref,
                 st_ref, bc_ref, kk_ref, vs_ref, qe_ref, ke_ref, od_ref, dec_ref, vt_ref):
    @pl.when(pl.program_id(1) == 0)
    def _():
        st_ref[...] = jnp.zeros_like(st_ref)

    q_ref, f_ref, v_ref = (q0_ref, q1_ref), (f0_ref, f1_ref), (v0_ref, v1_ref)
    r = q0_ref.shape[0]
    g = r // SUB
    lb = lb_ref[...]
    log_lb = jnp.log(lb)
    log_1m = jnp.log1p(-lb)

    def slab(halves, s):
        return jnp.concatenate([h[pl.ds(s, g, stride=SUB), :] for h in halves], axis=1)

    def put_rows(ref, s, x):
        for half in range(2):
            ref[half, pl.ds(s, g, stride=SUB), :] = x[:, half * V7X_LANES:(half + 1) * V7X_LANES]

    run = None
    for s in range(SUB):
        z = slab(f_ref, s)
        b = log_1m + _log_sigmoid(z)
        log_f = jnp.maximum(log_lb, b) + jnp.log1p(jnp.exp(-jnp.abs(log_lb - b)))
        run = log_f if s == 0 else run + log_f
        bc_ref[s] = run
        kk_ref[s] = (1.0 - lb) * _sigmoid(-z)
        vs_ref[s] = slab(v_ref, s)
    total = bc_ref[SUB - 1]
    dec_ref[...] = jnp.exp(total)

    for s in range(SUB):
        qs = slab(q_ref, s)
        bcs = bc_ref[s]
        prods = [(qs * kk_ref[j] * jnp.exp(bcs - bc_ref[j])).astype(BF16) for j in range(s)]
        prods.append((qs * kk_ref[s]).astype(BF16))
        red = jnp.dot(jnp.concatenate(prods, axis=0), ee_ref[...], preferred_element_type=F32)
        od = red[0:g] * vs_ref[0]
        for j in range(1, s + 1):
            od = od + red[j * g:(j + 1) * g] * vs_ref[j]
        put_rows(od_ref, s, od)
        put_rows(qe_ref, s, qs * jnp.exp(bcs))
        put_rows(ke_ref, s, kk_ref[s] * jnp.exp(total - bcs))
    for w in range(r // win):
        for half in range(2):
            vt_ref[w, half * V7X_LANES:(half + 1) * V7X_LANES, :] = (
                v_ref[half][w * win:(w + 1) * win, :].T.astype(BF16))

    hl = V7X_LANES
    same_head = (_iota((hl, hl), 0) // HEAD_DIM) == (_iota((hl, hl), 1) // HEAD_DIM)
    per_win = win // SUB

    def window(w, carry):
        halves = (0, 1)
        lanes = [slice(half * hl, (half + 1) * hl) for half in halves]

        def rows_of(c):
            return pl.ds(pl.multiple_of(w * win + c * SUB, SUB), SUB)

        upds = [[], []]
        for half in halves:
            vt = vt_ref[w, lanes[half], :]
            for c in range(per_win):
                pieces = [ke_ref[half, rows_of(c), :].astype(BF16)]
                if c:
                    pieces.insert(0, jnp.zeros((c * SUB, hl), BF16))
                if c < per_win - 1:
                    pieces.append(jnp.zeros(((per_win - 1 - c) * SUB, hl), BF16))
                upd = jnp.dot(vt, jnp.concatenate(pieces, axis=0), preferred_element_type=F32)
                upds[half].append(jnp.where(same_head, upd, 0.0))
        st = [st_ref[half] for half in halves]
        for c in range(per_win):
            rows = rows_of(c)
            dec = dec_ref[pl.ds(w * per_win + c, 1), :]
            for half in halves:
                o_state = lax.dot_general(qe_ref[half, rows, :].astype(BF16), st[half].astype(BF16),
                                          (((1,), (1,)), ((), ())), preferred_element_type=F32)
                o_ref[rows, lanes[half]] = od_ref[half, rows, :] + o_state
                st[half] = st[half] * dec[:, lanes[half]] + upds[half][c]
        for half in halves:
            st_ref[half] = st[half]
        return carry

    lax.fori_loop(0, r // win, window, 0)


def _hgrn(hg, layer, lb, batch, seq, r):
    t = hg.shape[0]
    win = min(r, V7X_LANES)
    ee = jnp.asarray(np.kron(np.eye(N_HEADS, dtype=np.float32), np.ones((HEAD_DIM, HEAD_DIM), np.float32)), BF16)
    nblk = seq // r

    def half(c):
        return pl.BlockSpec((r, V7X_LANES), lambda b, i: (b * nblk + i, c))

    tile = _nbytes((r, GROUP_W), F32)
    vmem = _vmem_limit(8 * tile, 7 * tile, 8 * tile)
    slabs = pltpu.VMEM((SUB, r // SUB, GROUP_W), F32)
    rows = pltpu.VMEM((2, r, V7X_LANES), F32)
    return pl.pallas_call(
        functools.partial(_hgrn_kernel, win),
        out_shape=jax.ShapeDtypeStruct((t, GROUP_W), F32),
        grid=(batch, nblk),
        in_specs=[half(0), half(1), half(2), half(3), half(4), half(5),
                  _layer_block(lb, layer),
                  pl.BlockSpec((GROUP_W, GROUP_W), lambda b, i: (0, 0))],
        out_specs=pl.BlockSpec((r, GROUP_W), lambda b, i: (b * nblk + i, 0)),
        scratch_shapes=[pltpu.VMEM((2, V7X_LANES, V7X_LANES), F32), slabs, slabs, slabs, rows, rows, rows,
                        pltpu.VMEM((r // SUB, GROUP_W), F32), pltpu.VMEM((r // win, GROUP_W, win), BF16)],
        name="hgrn2",
        compiler_params=pltpu.CompilerParams(dimension_semantics=("arbitrary", "arbitrary"),
                                             vmem_limit_bytes=vmem),
    )(hg, hg, hg, hg, hg, hg, lb, ee)


def _s5_local_kernel(u0_ref, u1_ref, bd_ref, bbd_ref, we_ref, d_ref, y_ref, e_ref, ua_ref):
    r = u0_ref.shape[0]
    g = r // SUB
    n = S5_GROUPS * S5_P
    for s in range(SUB):
        ua_ref[s * g:(s + 1) * g, :] = jnp.concatenate(
            [u0_ref[pl.ds(s, g, stride=SUB), :], u1_ref[pl.ds(s, g, stride=SUB), :]], axis=1)
    ua = ua_ref[...]
    ub = ua.astype(BF16)
    y_ref[...] = d_ref[...] * ua
    for j in range(SUB):
        y_ref[j * g:, :] += jnp.dot(ub[:(SUB - j) * g], bd_ref[j], preferred_element_type=F32)
    bu = jnp.dot(ub, bbd_ref[...], preferred_element_type=F32)
    er = jnp.zeros((g, n), F32)
    ei = jnp.zeros((g, n), F32)
    for s in range(SUB):
        bur, bui = bu[s * g:(s + 1) * g, :n], bu[s * g:(s + 1) * g, n:]
        wr, wi = we_ref[s:s + 1, :n], we_ref[s:s + 1, n:]
        er = er + (wr * bur - wi * bui)
        ei = ei + (wr * bui + wi * bur)
    e_ref[:, :n] = er
    e_ref[:, n:] = ei


def _s5_scan_kernel(e_ref, a_ref, x_ref):
    nchunk = e_ref.shape[0]
    n = S5_GROUPS * S5_P
    xr, xi = e_ref[:, :n], e_ref[:, n:]
    pr, pi = a_ref[:, :n], a_ref[:, n:]
    row = _iota((nchunk, n), 0)
    k = 1
    while k < nchunk:
        sr = jnp.where(row >= k, pltpu.roll(xr, k, 0), 0.0)
        si = jnp.where(row >= k, pltpu.roll(xi, k, 0), 0.0)
        xr, xi = xr + pr * sr - pi * si, xi + pr * si + pi * sr
        pr, pi = pr * pr - pi * pi, 2.0 * pr * pi
        k *= 2
    x_ref[:, :n] = jnp.where(row >= 1, pltpu.roll(xr, 1, 0), 0.0)
    x_ref[:, n:] = jnp.where(row >= 1, pltpu.roll(xi, 1, 0), 0.0)


def _s5_out_kernel(y1_ref, x_ref, a1_ref, cbd_ref, wg_ref, bg_ref, o_ref, xs_ref, nat_ref):
    r = y1_ref.shape[0]
    g = r // SUB
    n = S5_GROUPS * S5_P
    xr, xi = x_ref[:, :n], x_ref[:, n:]
    for s in range(SUB):
        ar, ai = a1_ref[s:s + 1, :n], a1_ref[s:s + 1, n:]
        xs_ref[s * g:(s + 1) * g, :n] = (ar * xr - ai * xi).astype(BF16)
        xs_ref[s * g:(s + 1) * g, n:] = (ar * xi + ai * xr).astype(BF16)
    y = y1_ref[...] + jnp.dot(xs_ref[...], cbd_ref[...], preferred_element_type=F32)
    zact = 0.5 * y * (1.0 + jnp.tanh(math.sqrt(2.0 / math.pi) * (y + 0.044715 * (y * y * y))))
    gate = jnp.dot(zact.astype(BF16), wg_ref[...], preferred_element_type=F32) + bg_ref[...]
    out = zact * _sigmoid(gate)
    for s in range(SUB):
        for half in range(2):
            nat_ref[half, pl.ds(s, g, stride=SUB), :] = out[s * g:(s + 1) * g,
                                                            half * V7X_LANES:(half + 1) * V7X_LANES]
    for half in range(2):
        o_ref[:, half * V7X_LANES:(half + 1) * V7X_LANES] = nat_ref[half]


def _s5_prep_kernel(ar_ref, ai_ref, bre_ref, bim_ref, cre_ref, cim_ref,
                    bd_ref, bbd_ref, cbd_ref, wend_ref, ain_ref, achunk_ref):
    n = S5_GROUPS * S5_P
    ar, ai = ar_ref[...], ai_ref[...]
    bre, bim, cre, cim = bre_ref[...], bim_ref[...], cre_ref[...], cim_ref[...]
    bbd_ref[:, :n] = bre.astype(BF16)
    bbd_ref[:, n:] = bim.astype(BF16)
    cbd_ref[:n, :] = cre.astype(BF16)
    cbd_ref[n:, :] = (-cim).astype(BF16)

    def split2(x):
        hi = x.astype(BF16)
        return hi, (x - hi.astype(F32)).astype(BF16)

    def dot3(x, w_hi, w_lo):
        x_hi, x_lo = split2(x)
        return (jnp.dot(x_hi, w_hi, preferred_element_type=F32) + jnp.dot(x_hi, w_lo, preferred_element_type=F32)
                + jnp.dot(x_lo, w_hi, preferred_element_type=F32))

    cre_parts, cim_parts = split2(cre), split2(cim)
    pr, pi = jnp.ones_like(ar), jnp.zeros_like(ar)
    for j in range(SUB):
        k_lag = dot3(bre * pr - bim * pi, *cre_parts) - dot3(bre * pi + bim * pr, *cim_parts)
        bd_ref[j] = k_lag.astype(BF16)
        wend_ref[SUB - 1 - j:SUB - j, :n] = pr
        wend_ref[SUB - 1 - j:SUB - j, n:] = pi
        pr, pi = pr * ar - pi * ai, pr * ai + pi * ar
        ain_ref[j:j + 1, :n] = pr
        ain_ref[j:j + 1, n:] = pi
    achunk_ref[:, :n] = pr
    achunk_ref[:, n:] = pi


def _s5_operators(lam_re, lam_im, log_step, b_re, b_im, c_re, c_im):
    depth = lam_re.shape[0]
    step = jnp.exp(log_step.astype(F32))[..., None]
    lre = jnp.minimum(lam_re.astype(F32), -1e-4)
    lim = lam_im.astype(F32)
    mag = jnp.exp(lre * step)
    a_re, a_im = mag * jnp.cos(lim * step), mag * jnp.sin(lim * step)
    den = lre * lre + lim * lim
    coef_re = ((a_re - 1.0) * lre + a_im * lim) / den
    coef_im = (a_im * lre - (a_re - 1.0) * lim) / den
    br, bi = b_re.astype(F32), b_im.astype(F32)
    bb_re = coef_re[..., None] * br - coef_im[..., None] * bi
    bb_im = coef_re[..., None] * bi + coef_im[..., None] * br
    cr, ci = c_re.astype(F32), c_im.astype(F32)
    n = S5_GROUPS * S5_P
    same_group = (np.arange(GROUP_W)[:, None] // S5_CH) == (np.arange(n)[None, :] // S5_P)
    in_mask = jnp.asarray(same_group, F32)
    out_mask = jnp.asarray(same_group.T, F32)

    def in_side(b):
        return jnp.tile(b.transpose(0, 1, 3, 2).reshape(depth, GROUP_W, S5_P), (1, 1, S5_GROUPS)) * in_mask

    def out_side(c):
        return jnp.tile(c.transpose(0, 1, 3, 2).reshape(depth, n, S5_CH), (1, 1, S5_GROUPS)) * out_mask

    args = (a_re.reshape(depth, 1, n), a_im.reshape(depth, 1, n), in_side(bb_re), in_side(bb_im),
            out_side(cr), out_side(ci))
    out_dims = (((SUB, GROUP_W, GROUP_W), BF16),
                ((GROUP_W, 2 * n), BF16),
                ((2 * n, GROUP_W), BF16),
                ((SUB, 2 * n), F32),
                ((SUB, 2 * n), F32),
                ((1, 2 * n), F32))

    def per_layer(shape):
        return pl.BlockSpec((None,) + shape, lambda l: (l,) + (0,) * len(shape))

    return pl.pallas_call(
        _s5_prep_kernel,
        out_shape=tuple(jax.ShapeDtypeStruct((depth,) + s, d) for s, d in out_dims),
        grid=(depth,),
        in_specs=[per_layer(a.shape[1:]) for a in args],
        out_specs=tuple(per_layer(s) for s, _ in out_dims),
        name="s5_prep",
        compiler_params=pltpu.CompilerParams(
            dimension_semantics=("arbitrary",),
            vmem_limit_bytes=_vmem_limit(8 * _nbytes((GROUP_W, n), F32), 10 * _nbytes((GROUP_W, n), F32),
                                         4 * _nbytes((SUB, GROUP_W, GROUP_W), BF16))),
    )(*args)


def _s5(u, layer, ops, d_skip, w_glu, b_glu, batch, seq, r):
    t = u.shape[0]
    bd, bbd, cbd, w_end, a_in, a_chunk = ops
    n2 = 2 * S5_GROUPS * S5_P
    nck = seq // SUB

    def whole(a):
        return _layer_block(a, layer)

    rows = pl.BlockSpec((r, GROUP_W), lambda i: (i, 0))
    crow = pl.BlockSpec((r // SUB, n2), lambda i: (i, 0))
    y1, e = pl.pallas_call(
        _s5_local_kernel,
        out_shape=(jax.ShapeDtypeStruct((t, GROUP_W), F32), jax.ShapeDtypeStruct((t // SUB, n2), F32)),
        grid=(t // r,),
        in_specs=[pl.BlockSpec((r, V7X_LANES), lambda i: (i, 0)), pl.BlockSpec((r, V7X_LANES), lambda i: (i, 1)),
                  whole(bd), whole(bbd), whole(w_end), whole(d_skip)],
        out_specs=(rows, crow),
        scratch_shapes=[pltpu.VMEM((r, GROUP_W), F32)],
        name="s5_local",
        compiler_params=pltpu.CompilerParams(
            dimension_semantics=("arbitrary",),
            vmem_limit_bytes=_vmem_limit(_nbytes(bd.shape[1:], BF16), _nbytes(bbd.shape[1:], BF16),
                                         6 * _nbytes((r, n2), F32), 8 * _nbytes((r, GROUP_W), F32))),
    )(u, u, bd, bbd, w_end, d_skip)
    xin = pl.pallas_call(
        _s5_scan_kernel,
        out_shape=jax.ShapeDtypeStruct((t // SUB, n2), F32),
        grid=(batch,),
        in_specs=[pl.BlockSpec((nck, n2), lambda b: (b, 0)), whole(a_chunk)],
        out_specs=pl.BlockSpec((nck, n2), lambda b: (b, 0)),
        name="s5_scan",
        compiler_params=pltpu.CompilerParams(
            dimension_semantics=("arbitrary",), vmem_limit_bytes=_vmem_limit(10 * _nbytes((nck, n2), F32))),
    )(e, a_chunk)
    return pl.pallas_call(
        _s5_out_kernel,
        out_shape=jax.ShapeDtypeStruct((t, GROUP_W), F32),
        grid=(t // r,),
        in_specs=[rows, crow, whole(a_in), whole(cbd), whole(w_glu), whole(b_glu)],
        out_specs=rows,
        scratch_shapes=[pltpu.VMEM((r, n2), BF16), pltpu.VMEM((2, r, V7X_LANES), F32)],
        name="s5_out",
        compiler_params=pltpu.CompilerParams(
            dimension_semantics=("arbitrary",),
            vmem_limit_bytes=_vmem_limit(_nbytes(cbd.shape[1:], BF16), 3 * _nbytes((r, n2), BF16),
                                         10 * _nbytes((r, GROUP_W), F32))),
    )(y1, xin, a_in, cbd, w_glu, b_glu)


HALO = 2 * V7X_SUBLANES


def _post_kernel(tiles_per_seq, final, ya_ref, yb_ref, yc_ref, gate_ref, yd_ref, h_ref, p_ref,
                 gn_ref, wo_ref, fg_ref, wu_ref, cw_ref, cb_ref, wd_ref, pg_ref, wpg_ref, wp_ref, ng_ref,
                 o_ref, xn_ref, up_ref, act_ref):
    tm = h_ref.shape[0]
    gw = GROUP_W

    @pl.when(pl.program_id(0) % tiles_per_seq == 0)
    def _():
        xn_ref[0:HALO, :] = jnp.zeros((HALO, D_MODEL), BF16)

    parts = (_rms(ya_ref[...], gn_ref[:, 0:gw]),
             _rms(yb_ref[...], gn_ref[:, gw:2 * gw]),
             _rms(yc_ref[...], gn_ref[:, 2 * gw:3 * gw]) * _sigmoid(gate_ref[...]),
             _rms(yd_ref[...], gn_ref[:, 3 * gw:4 * gw]))
    h1 = h_ref[...]
    for g, part in enumerate(parts):
        h1 = h1 + jnp.dot(part.astype(BF16), wo_ref[g * gw:(g + 1) * gw, :], preferred_element_type=F32)
    o_ref[...] = h1
    xn_ref[HALO:, :] = _rms(h1, fg_ref[...]).astype(BF16)

    nchunk = D_FF // MXU_TILE

    def conv(slot, c):
        cols = slice(c * MXU_TILE, (c + 1) * MXU_TILE)
        up_ref[slot] = jnp.dot(xn_ref[...], wu_ref[:, cols], preferred_element_type=F32)
        return (cb_ref[:, cols] + cw_ref[0:1, cols] * up_ref[slot, pl.ds(HALO - 2, tm), :]
                + cw_ref[1:2, cols] * up_ref[slot, pl.ds(HALO - 1, tm), :]
                + cw_ref[2:3, cols] * up_ref[slot, pl.ds(HALO, tm), :])

    for c in range(nchunk):
        slot = 2 * (c % 2)
        gate = conv(slot, c)
        val = conv(slot + 1, c + nchunk)
        act_ref[:, c * MXU_TILE:(c + 1) * MXU_TILE] = (gate * _sigmoid(gate) * val).astype(BF16)
    xn_ref[0:HALO, :] = xn_ref[tm:tm + HALO, :]
    h2 = o_ref[...] + jnp.dot(act_ref[...], wd_ref[...], preferred_element_type=F32)

    pgate = _sigmoid(jnp.dot(_rms(h2, pg_ref[...]).astype(BF16), wpg_ref[...], preferred_element_type=F32))
    out = h2 + pgate * jnp.dot(p_ref[...].astype(BF16), wp_ref[...], preferred_element_type=F32)
    o_ref[...] = _rms(out, ng_ref[...]) if final else out


def _post(ya, yb, yc, hg, yd, h, p, seq, layer, stacked, final, tm):
    t = h.shape[0]
    rows = pl.BlockSpec((tm, GROUP_W), lambda i: (i, 0))
    wide = pl.BlockSpec((tm, D_MODEL), lambda i: (i, 0))
    consts = tuple(stacked)
    vmem = _vmem_limit(sum(_nbytes(c.shape[1:], c.dtype) for c in consts), 12 * _nbytes((tm, GROUP_W), F32),
                       4 * _nbytes((tm, D_MODEL), F32), _nbytes((tm + HALO, D_MODEL), BF16),
                       4 * _nbytes((tm + HALO, MXU_TILE), F32), _nbytes((tm, D_FF), BF16),
                       3 * _nbytes((tm, D_MODEL), F32))
    return pl.pallas_call(
        functools.partial(_post_kernel, seq // tm, final),
        out_shape=jax.ShapeDtypeStruct((t, D_MODEL), F32),
        grid=(t // tm,),
        in_specs=[rows, rows, rows, pl.BlockSpec((tm, GROUP_W), lambda i: (i, 3)), rows, wide,
                  pl.BlockSpec((tm, PLE_DIM), lambda i: (layer * (t // tm) + i, 0))]
        + [_layer_block(c, layer) for c in consts],
        out_specs=wide,
        scratch_shapes=[pltpu.VMEM((tm + HALO, D_MODEL), BF16), pltpu.VMEM((4, tm + HALO, MXU_TILE), F32),
                        pltpu.VMEM((tm, D_FF), BF16)],
        name="post_mixer",
        compiler_params=pltpu.CompilerParams(dimension_semantics=("arbitrary",), vmem_limit_bytes=vmem),
    )(ya, yb, yc, hg, yd, h, p, *consts)


def _tiles(seq):
    return dict(tin=min(512, seq), tm=min(512, seq), tq=min(512, seq), tkv=min(512, seq), r=min(512, seq),
                rs5=min(1024, seq))


def kernel(x, p, positions, attn_norm_g, w_in, mla_q_norm_g, mla_w_uq, mla_kv_norm_g, mla_w_ukv, fox_b_f,
           hgrn_lb_param, s5_lam_re, s5_lam_im, s5_log_step, s5_b_re, s5_b_im, s5_c_re, s5_c_im, s5_d,
           s5_w_glu, s5_b_glu, group_norm_g, w_out, ffn_norm_g, w_up, conv_w, conv_b, w_down, ple_norm_g,
           w_ple_gate, w_ple, final_norm_g):
    batch, seq, _ = x.shape
    depth = w_in.shape[0]
    t = batch * seq
    ts = _tiles(seq)
    assert all(seq % ts[k] == 0 for k in ts) and ts["tq"] % ts["tkv"] == 0
    assert (seq // SUB) & (seq // SUB - 1) == 0, "chunk scan assumes a power-of-two chunk count"

    lb_all = jnp.cumsum(jax.nn.softmax(hgrn_lb_param.astype(F32), axis=0), axis=0)
    lb_all = lb_all - lb_all[0:1]
    tables = _rope_tables(positions, ts["tin"])

    def row(v):
        return v.reshape(v.shape[0], 1, -1)

    in_params = _inproj_params(attn_norm_g, w_in, mla_q_norm_g, mla_w_uq, mla_kv_norm_g, mla_w_ukv, fox_b_f)
    s5_ops = _s5_operators(s5_lam_re, s5_lam_im, s5_log_step, s5_b_re, s5_b_im, s5_c_re, s5_c_im)
    s5_tail = (row(s5_d.astype(F32)), s5_w_glu.astype(BF16), row(s5_b_glu.astype(F32)))
    post_params = (row(group_norm_g), w_out.astype(BF16), row(ffn_norm_g), w_up.astype(BF16), conv_w,
                   row(conv_b), w_down.astype(BF16), row(ple_norm_g), w_ple_gate.astype(BF16),
                   w_ple.astype(BF16), jnp.broadcast_to(final_norm_g.reshape(1, 1, -1), (depth, 1, D_MODEL)))
    h = x.reshape(t, D_MODEL)
    for i in range(depth):
        mq, mk, mv, fq, fk, fv, hg, su = _inproj(h, seq, i, in_params, tables, ts["tin"])
        y_a = _attention(mq, mk, mv, batch, seq, ts["tq"], ts["tkv"])
        y_b = _attention(fq, fk, fv, batch, seq, ts["tq"], ts["tkv"])
        y_c = _hgrn(hg, i, row(lb_all), batch, seq, ts["r"])
        y_d = _s5(su, i, s5_ops, *s5_tail, batch, seq, ts["rs5"])
        h = _post(y_a, y_b, y_c, hg, y_d, h, p.reshape(depth * t, PLE_DIM), seq, i, post_params,
                  i == depth - 1, ts["tm"])
    return h.reshape(batch, seq, D_MODEL)
```

```python
import functools
import math

import numpy as np
import jax
import jax.numpy as jnp
from jax import lax
from jax.experimental import pallas as pl
from jax.experimental.pallas import tpu as pltpu

F32 = jnp.float32
BF16 = jnp.bfloat16

D_MODEL = 1024
N_HEADS = 4
HEAD_DIM = 64
GROUP_W = 256
MLA_Q_RANK = 256
MLA_KV_RANK = 128
MLA_NOPE = 64
MLA_ROPE = 32
ROPE_THETA = 10000.0
S5_GROUPS = 16
S5_CH = 16
S5_P = 64
D_FF = 2816
PLE_DIM = 256
EPS = 1e-6
N_IN = 2468

V7X_LANES = 128
V7X_SUBLANES = 8
V7X_VMEM_BYTES = 64 * 1024 * 1024
MXU_TILE = 256
VMEM_CAP_BYTES = 58 * 1024 * 1024

HEAD_PAD = V7X_LANES
SUB = 16
NEG_BIG = -1e30

SEG_CQ = 0
SEG_CKV = 256
SEG_KR = 384
SEG_FOX = 512
SEG_HG = 1280
SEG_S5 = 2304
SEG_FF = 2560
N_PERM = 2688
ROPE_LANE0 = 64
BIAS_LANE0 = 64
LOG2E = math.log2(math.e)


def _vmem_limit(*byte_counts):
    need = int(sum(byte_counts))
    return int(min(VMEM_CAP_BYTES, need + need // 4 + (4 << 20)))


def _nbytes(shape, dtype):
    return int(np.prod(shape)) * jnp.dtype(dtype).itemsize


def _rms(x, g):
    return x * lax.rsqrt(jnp.mean(x * x, axis=-1, keepdims=True) + EPS) * g


def _log_sigmoid(z):
    return jnp.minimum(z, 0.0) - jnp.log1p(jnp.exp(-jnp.abs(z)))


def _sigmoid(z):
    return 1.0 / (1.0 + jnp.exp(-z))


def _iota(shape, dim):
    return lax.broadcasted_iota(jnp.int32, shape, dim)


ROPE_PACK = HEAD_PAD // MLA_ROPE


def _rope_kernel(pos_ref, freq_ref, ct_ref, s1_ref, s2_ref):
    ang = pos_ref[...].astype(F32) * freq_ref[...]
    cos, sin = jnp.cos(ang), jnp.sin(ang)
    g = ang.shape[0]
    lane = _iota(ang.shape, 1)
    half = MLA_ROPE // 2
    first = (lane >= ROPE_LANE0) & (lane < ROPE_LANE0 + half)
    second = (lane >= ROPE_LANE0 + half) & (lane < ROPE_LANE0 + 2 * half)
    for a in range(ROPE_PACK):
        shift = (ROPE_LANE0 - MLA_ROPE * a) % HEAD_PAD
        ca = pltpu.roll(cos, shift, 1) if shift else cos
        sa = pltpu.roll(sin, shift, 1) if shift else sin
        rows = pl.ds(a, g, stride=ROPE_PACK)
        ct_ref[rows, :] = jnp.where(first | second, ca, 1.0)
        s1_ref[rows, :] = jnp.where(first, -sa, 0.0)
        s2_ref[rows, :] = jnp.where(second, sa, 0.0)


def _rope_tables(positions, tm):
    t = positions.size
    half = MLA_ROPE // 2
    inv_freq = ROPE_THETA ** (-jnp.arange(half, dtype=F32) / half)
    freq = jnp.tile(inv_freq, 2 * ROPE_PACK).reshape(1, HEAD_PAD)
    pos = jnp.repeat(positions.reshape(t // ROPE_PACK, ROPE_PACK), MLA_ROPE, axis=1)
    out = jax.ShapeDtypeStruct((t, HEAD_PAD), F32)
    spec = pl.BlockSpec((tm, HEAD_PAD), lambda i: (i, 0))
    return pl.pallas_call(
        _rope_kernel,
        out_shape=(out, out, out),
        grid=(t // tm,),
        in_specs=[pl.BlockSpec((tm // ROPE_PACK, HEAD_PAD), lambda i: (i, 0)),
                  pl.BlockSpec((1, HEAD_PAD), lambda i: (0, 0))],
        out_specs=(spec, spec, spec),
        name="rope_tables",
        compiler_params=pltpu.CompilerParams(dimension_semantics=("arbitrary",)),
    )(pos, freq)


def _inproj_kernel(tiles_per_seq, h_ref, g_ref, w_ref, qg_ref, wuq_ref, kvg_ref, wukv_ref, bf_ref,
                   ct_ref, s1_ref, s2_ref, selq_ref, selk_ref,
                   mq_ref, mk_ref, mv_ref, fq_ref, fk_ref, fv_ref, hg_ref, su_ref, carry_ref):
    i = pl.program_id(0)

    @pl.when(i % tiles_per_seq == 0)
    def _():
        carry_ref[...] = jnp.zeros_like(carry_ref)

    tm = h_ref.shape[0]
    xn = _rms(h_ref[...], g_ref[...]).astype(BF16)

    proj = jnp.dot(xn, w_ref[...], preferred_element_type=F32)

    def seg(a, b):
        return proj[:, a:b]

    q = jnp.dot(_rms(seg(SEG_CQ, SEG_CKV), qg_ref[...]).astype(BF16), wuq_ref[...],
                preferred_element_type=F32)
    ckv_kr = seg(SEG_CKV, SEG_FOX)
    kv = jnp.dot(_rms(ckv_kr[:, :MLA_KV_RANK], kvg_ref[...]).astype(BF16), wukv_ref[...],
                 preferred_element_type=F32)
    ct, s1, s2 = ct_ref[...], s1_ref[...], s2_ref[...]
    half = MLA_ROPE // 2

    def rope(t):
        return t * ct + pltpu.roll(t, HEAD_PAD - half, 1) * s1 + pltpu.roll(t, half, 1) * s2

    k_pe = rope(ckv_kr[:, MLA_KV_RANK:])
    mla_scale = (MLA_NOPE + MLA_ROPE) ** -0.5 * LOG2E
    for hd in range(N_HEADS):
        sl = slice(hd * HEAD_PAD, (hd + 1) * HEAD_PAD)
        mq_ref[:, sl] = (rope(q[:, sl]) * mla_scale).astype(BF16)
        mk_ref[:, sl] = (kv[:, sl] + k_pe).astype(BF16)
    hp4 = N_HEADS * HEAD_PAD
    ones_pad = jnp.where((_iota((1, hp4), 1) & (HEAD_PAD - 1)) >= HEAD_DIM, 1.0, 0.0)
    mv_ref[...] = (kv[:, hp4:] + ones_pad).astype(BF16)

    lane = _iota((tm, HEAD_PAD), 1)
    keep = lane < N_HEADS

    def parts3(x):
        a = x.astype(BF16).astype(F32)
        r = x - a
        b = r.astype(BF16).astype(F32)
        c = (r - b).astype(BF16).astype(F32)
        return (jnp.where(keep, a, 0.0) + pltpu.roll(jnp.where(keep, b, 0.0), N_HEADS, 1)
                + pltpu.roll(jnp.where(keep, c, 0.0), 2 * N_HEADS, 1))

    lf = _log_sigmoid(seg(SEG_FF, N_PERM) + bf_ref[...])
    tril = (_iota((tm, tm), 0) >= _iota((tm, tm), 1)).astype(BF16)
    cum3 = jnp.dot(tril, parts3(lf).astype(BF16), preferred_element_type=F32)
    cum = (cum3 + pltpu.roll(cum3, HEAD_PAD - N_HEADS, 1) + pltpu.roll(cum3, HEAD_PAD - 2 * N_HEADS, 1)
           + carry_ref[...])
    carry_ref[...] = cum[tm - 1:tm, :]
    parts = (parts3(cum * LOG2E) + jnp.where(lane == 3 * N_HEADS, 1.0, 0.0)).astype(BF16)
    bias_q = jnp.dot(parts, selq_ref[...], preferred_element_type=F32)
    bias_k = jnp.dot(parts, selk_ref[...], preferred_element_type=F32)
    fox = seg(SEG_FOX, SEG_HG)
    fox_scale = HEAD_DIM ** -0.5 * LOG2E
    low = lane < HEAD_DIM
    for hd in range(N_HEADS):
        sl = slice(hd * HEAD_PAD, (hd + 1) * HEAD_PAD)
        src = slice((hd // 2) * HEAD_PAD, (hd // 2 + 1) * HEAD_PAD)

        def head_tile(x):
            tile = x[:, src]
            return pltpu.roll(tile, HEAD_DIM, 1) if hd % 2 else tile

        fq_ref[:, sl] = jnp.where(low, head_tile(fox[:, 0:GROUP_W]) * fox_scale, bias_q[:, sl]).astype(BF16)
        fk_ref[:, sl] = jnp.where(low, head_tile(fox[:, GROUP_W:2 * GROUP_W]), bias_k[:, sl]).astype(BF16)
        fv_ref[:, sl] = jnp.where(low, head_tile(fox[:, 2 * GROUP_W:]), 1.0).astype(BF16)

    hg_ref[...] = seg(SEG_HG, SEG_S5)
    su_ref[...] = seg(SEG_S5, SEG_FF)


def _permute_inproj(w):
    def zeros(n):
        return jnp.zeros(w.shape[:-1] + (n,), w.dtype)

    cols = [w[..., 0:384], zeros(ROPE_LANE0), w[..., 384:416], zeros(HEAD_PAD - ROPE_LANE0 - MLA_ROPE),
            w[..., 416:1184], w[..., 1188:N_IN], w[..., 1184:1188], zeros(HEAD_PAD - N_HEADS)]
    out = jnp.concatenate(cols, axis=-1)
    assert out.shape[-1] == N_PERM
    return out


def _bias_selectors():
    selq = np.zeros((HEAD_PAD, N_HEADS * HEAD_PAD), np.float32)
    selk = np.zeros((HEAD_PAD, N_HEADS * HEAD_PAD), np.float32)
    one = 3 * N_HEADS
    for hd in range(N_HEADS):
        for j in range(3):
            selq[N_HEADS * j + hd, hd * HEAD_PAD + BIAS_LANE0 + j] = 1.0
            selq[one, hd * HEAD_PAD + BIAS_LANE0 + 3 + j] = 1.0
            selk[one, hd * HEAD_PAD + BIAS_LANE0 + j] = 1.0
            selk[N_HEADS * j + hd, hd * HEAD_PAD + BIAS_LANE0 + 3 + j] = -1.0
    return jnp.asarray(selq, BF16), jnp.asarray(selk, BF16)


def _layer_block(a, layer):
    shape = a.shape[1:]
    return pl.BlockSpec((None,) + shape, lambda *_: (layer,) + (0,) * len(shape), pipeline_mode=pl.Buffered(1))


def _inproj_params(attn_g, w_in, q_g, w_uq, kv_g, w_ukv, b_f):
    depth = w_in.shape[0]
    wuq = jnp.pad(w_uq.reshape(depth, MLA_Q_RANK, N_HEADS, MLA_NOPE + MLA_ROPE),
                  ((0, 0), (0, 0), (0, 0), (0, HEAD_PAD - MLA_NOPE - MLA_ROPE)))
    wkv = w_ukv.reshape(depth, MLA_KV_RANK, N_HEADS, 2 * HEAD_DIM)
    head_pad = ((0, 0), (0, 0), (0, 0), (0, HEAD_PAD - HEAD_DIM))
    wk = jnp.pad(wkv[..., :MLA_NOPE], head_pad).reshape(depth, MLA_KV_RANK, -1)
    wv = jnp.pad(wkv[..., MLA_NOPE:], head_pad).reshape(depth, MLA_KV_RANK, -1)
    bf = jnp.pad(b_f.astype(F32), ((0, 0), (0, HEAD_PAD - N_HEADS)))

    def row(v):
        return v.reshape(depth, 1, -1)

    return (row(attn_g), _permute_inproj(w_in.astype(BF16)), row(q_g),
            wuq.reshape(depth, MLA_Q_RANK, -1).astype(BF16), row(kv_g),
            jnp.concatenate([wk, wv], axis=2).astype(BF16), row(bf))


def _inproj(h, seq, layer, stacked, tables, tm):
    t = h.shape[0]
    selq, selk = _bias_selectors()
    ct, s1, s2 = tables
    hp4 = N_HEADS * HEAD_PAD
    w_perm = stacked[1]

    def rows(width):
        return pl.BlockSpec((tm, width), lambda i: (i, 0))

    def whole(a):
        return pl.BlockSpec(a.shape, lambda i: (0,) * a.ndim)

    args = (h,) + tuple(stacked) + (ct, s1, s2, selq, selk)
    in_specs = ([rows(D_MODEL)] + [_layer_block(a, layer) for a in stacked]
                + [rows(HEAD_PAD)] * 3 + [whole(selq), whole(selk)])
    out_widths = (hp4,) * 6
    out_shape = tuple(jax.ShapeDtypeStruct((t, w), BF16) for w in out_widths) + (
        jax.ShapeDtypeStruct((t, 4 * GROUP_W), F32), jax.ShapeDtypeStruct((t, GROUP_W), F32))
    out_specs = tuple(rows(w) for w in out_widths) + (rows(4 * GROUP_W), rows(GROUP_W))
    vmem = _vmem_limit(_nbytes(w_perm.shape[1:], BF16), 2 * _nbytes((tm, D_MODEL), F32),
                       2 * sum(_nbytes((tm, w), BF16) for w in out_widths),
                       2 * _nbytes((tm, 5 * GROUP_W), F32), _nbytes((tm, N_PERM), F32),
                       _nbytes((tm, D_MODEL), F32))
    return pl.pallas_call(
        functools.partial(_inproj_kernel, seq // tm),
        out_shape=out_shape,
        grid=(t // tm,),
        in_specs=in_specs,
        out_specs=out_specs,
        scratch_shapes=[pltpu.VMEM((1, HEAD_PAD), F32)],
        name="inproj",
        compiler_params=pltpu.CompilerParams(dimension_semantics=("arbitrary",), vmem_limit_bytes=vmem),
    )(*args)


def _attn_kernel(tq, tkv, q_ref, k_ref, v_ref, o_ref, m_ref, acc_ref):
    qi = pl.program_id(1)
    m_ref[...] = jnp.full_like(m_ref, NEG_BIG)
    acc_ref[...] = jnp.zeros_like(acc_ref)
    per_q = tq // tkv
    heads = [slice(hd * HEAD_PAD, (hd + 1) * HEAD_PAD) for hd in range(N_HEADS)]

    def key_rows(start, width):
        return pl.ds(pl.multiple_of(start, tkv), width)

    def logits_of(start, width):
        rows = key_rows(start, width)
        return [lax.dot_general(q_ref[:, sl], k_ref[rows, sl], (((1,), (1,)), ((), ())),
                                preferred_element_type=F32) for sl in heads]

    def finish(logits, start, width, diag_offset):
        rows = key_rows(start, width)
        probs, rescale = [], []
        for hd in range(N_HEADS):
            s = logits[hd]
            if diag_offset is not None:
                visible = _iota(s.shape, 1) + diag_offset <= _iota(s.shape, 0)
                s = jnp.where(visible, s, NEG_BIG)
            m_old = m_ref[hd]
            m_new = jnp.maximum(m_old, jnp.max(s, axis=1, keepdims=True))
            probs.append(jnp.exp2(s - jnp.concatenate([m_new] * (width // HEAD_PAD), axis=1)).astype(BF16))
            rescale.append(jnp.exp2(m_old - m_new))
            m_ref[hd] = m_new
        for hd, sl in enumerate(heads):
            acc_ref[hd] = (rescale[hd] * acc_ref[hd]
                           + jnp.dot(probs[hd], v_ref[rows, sl], preferred_element_type=F32))

    def block(start, width, diag_offset):
        finish(logits_of(start, width), start, width, diag_offset)

    n_full = qi * per_q

    def wide_block(jj, carry):
        block(jj * (3 * tkv), 3 * tkv, None)
        return carry

    n_wide = n_full // 3
    lax.fori_loop(0, n_wide, wide_block, 0)
    left = n_full - 3 * n_wide

    @pl.when(left == 2)
    def _():
        block((n_full - 2) * tkv, 2 * tkv, None)

    @pl.when(left == 1)
    def _():
        block((n_full - 1) * tkv, tkv, None)

    for r in range(per_q):
        block((n_full + r) * tkv, tkv, r * tkv)

    low = _iota((tq, HEAD_PAD), 1) < HEAD_DIM
    for pair in range(N_HEADS // 2):
        a0, a1 = acc_ref[2 * pair], acc_ref[2 * pair + 1]
        n0 = a0 / pltpu.roll(a0, HEAD_DIM, 1)
        n1 = a1 / pltpu.roll(a1, HEAD_DIM, 1)
        o_ref[:, pair * HEAD_PAD:(pair + 1) * HEAD_PAD] = jnp.where(low, n0, pltpu.roll(n1, HEAD_DIM, 1))


def _attention(q, k, v, batch, seq, tq, tkv):
    t = q.shape[0]
    hp4 = N_HEADS * HEAD_PAD
    vmem = _vmem_limit(2 * _nbytes((seq, hp4), BF16), 2 * _nbytes((tq, hp4), BF16),
                       2 * _nbytes((tq, GROUP_W), F32), 2 * N_HEADS * _nbytes((tq, HEAD_PAD), F32),
                       6 * _nbytes((tq, 2 * tkv), F32))
    resident = pl.BlockSpec((seq, hp4), lambda b, i: (b, 0), pipeline_mode=pl.Buffered(1))
    return pl.pallas_call(
        functools.partial(_attn_kernel, tq, tkv),
        out_shape=jax.ShapeDtypeStruct((t, GROUP_W), F32),
        grid=(batch, seq // tq),
        in_specs=[pl.BlockSpec((tq, hp4), lambda b, i: (b * (seq // tq) + i, 0)), resident, resident],
        out_specs=pl.BlockSpec((tq, GROUP_W), lambda b, i: (b * (seq // tq) + i, 0)),
        scratch_shapes=[pltpu.VMEM((N_HEADS, tq, HEAD_PAD), F32), pltpu.VMEM((N_HEADS, tq, HEAD_PAD), F32)],
        name="causal_attention",
        compiler_params=pltpu.CompilerParams(dimension_semantics=("arbitrary", "arbitrary"),
                                             vmem_limit_bytes=vmem),
    )(q, k, v)


def _hgrn_kernel(win, q0_ref, q1_ref, f0_ref, f1_ref, v0_ref, v1_ref, lb_ref, ee_ref, o_ref,
                 st_ref, bc_ref, kk_ref, vs_ref, qe_ref, ke_ref, od_ref, dec_ref, vt_ref):
    @pl.when(pl.program_id(1) == 0)
    def _():
        st_ref[...] = jnp.zeros_like(st_ref)

    q_ref, f_ref, v_ref = (q0_ref, q1_ref), (f0_ref, f1_ref), (v0_ref, v1_ref)
    r = q0_ref.shape[0]
    g = r // SUB
    lb = lb_ref[...]
    log_lb = jnp.log(lb)
    log_1m = jnp.log1p(-lb)

    def slab(halves, s):
        return jnp.concatenate([h[pl.ds(s, g, stride=SUB), :] for h in halves], axis=1)

    def put_rows(ref, s, x):
        for half in range(2):
            ref[half, pl.ds(s, g, stride=SUB), :] = x[:, half * V7X_LANES:(half + 1) * V7X_LANES]

    run = None
    for s in range(SUB):
        z = slab(f_ref, s)
        b = log_1m + _log_sigmoid(z)
        log_f = jnp.maximum(log_lb, b) + jnp.log1p(jnp.exp(-jnp.abs(log_lb - b)))
        run = log_f if s == 0 else run + log_f
        bc_ref[s] = run
        kk_ref[s] = (1.0 - lb) * _sigmoid(-z)
        vs_ref[s] = slab(v_ref, s)
    total = bc_ref[SUB - 1]
    dec_ref[...] = jnp.exp(total)

    for s in range(SUB):
        qs = slab(q_ref, s)
        bcs = bc_ref[s]
        prods = [(qs * kk_ref[j] * jnp.exp(bcs - bc_ref[j])).astype(BF16) for j in range(s)]
        prods.append((qs * kk_ref[s]).astype(BF16))
        red = jnp.dot(jnp.concatenate(prods, axis=0), ee_ref[...], preferred_element_type=F32)
        od = red[0:g] * vs_ref[0]
        for j in range(1, s + 1):
            od = od + red[j * g:(j + 1) * g] * vs_ref[j]
        put_rows(od_ref, s, od)
        put_rows(qe_ref, s, qs * jnp.exp(bcs))
        put_rows(ke_ref, s, kk_ref[s] * jnp.exp(total - bcs))
    for w in range(r // win):
        for half in range(2):
            vt_ref[w, half * V7X_LANES:(half + 1) * V7X_LANES, :] = (
                v_ref[half][w * win:(w + 1) * win, :].T.astype(BF16))

    hl = V7X_LANES
    same_head = (_iota((hl, hl), 0) // HEAD_DIM) == (_iota((hl, hl), 1) // HEAD_DIM)
    per_win = win // SUB

    def window(w, carry):
        halves = (0, 1)
        lanes = [slice(half * hl, (half + 1) * hl) for half in halves]

        def rows_of(c):
            return pl.ds(pl.multiple_of(w * win + c * SUB, SUB), SUB)

        upds = [[], []]
        for half in halves:
            vt = vt_ref[w, lanes[half], :]
            for c in range(per_win):
                pieces = [ke_ref[half, rows_of(c), :].astype(BF16)]
                if c:
                    pieces.insert(0, jnp.zeros((c * SUB, hl), BF16))
                if c < per_win - 1:
                    pieces.append(jnp.zeros(((per_win - 1 - c) * SUB, hl), BF16))
                upd = jnp.dot(vt, jnp.concatenate(pieces, axis=0), preferred_element_type=F32)
                upds[half].append(jnp.where(same_head, upd, 0.0))
        st = [st_ref[half] for half in halves]
        for c in range(per_win):
            rows = rows_of(c)
            dec = dec_ref[pl.ds(w * per_win + c, 1), :]
            for half in halves:
                o_state = lax.dot_general(qe_ref[half, rows, :].astype(BF16), st[half].astype(BF16),
                                          (((1,), (1,)), ((), ())), preferred_element_type=F32)
                o_ref[rows, lanes[half]] = od_ref[half, rows, :] + o_state
                st[half] = st[half] * dec[:, lanes[half]] + upds[half][c]
        for half in halves:
            st_ref[half] = st[half]
        return carry

    lax.fori_loop(0, r // win, window, 0)


def _hgrn(hg, layer, lb, batch, seq, r):
    t = hg.shape[0]
    win = min(r, V7X_LANES)
    ee = jnp.asarray(np.kron(np.eye(N_HEADS, dtype=np.float32), np.ones((HEAD_DIM, HEAD_DIM), np.float32)), BF16)
    nblk = seq // r

    def half(c):
        return pl.BlockSpec((r, V7X_LANES), lambda b, i: (b * nblk + i, c))

    tile = _nbytes((r, GROUP_W), F32)
    vmem = _vmem_limit(8 * tile, 7 * tile, 8 * tile)
    slabs = pltpu.VMEM((SUB, r // SUB, GROUP_W), F32)
    rows = pltpu.VMEM((2, r, V7X_LANES), F32)
    return pl.pallas_call(
        functools.partial(_hgrn_kernel, win),
        out_shape=jax.ShapeDtypeStruct((t, GROUP_W), F32),
        grid=(batch, nblk),
        in_specs=[half(0), half(1), half(2), half(3), half(4), half(5),
                  _layer_block(lb, layer),
                  pl.BlockSpec((GROUP_W, GROUP_W), lambda b, i: (0, 0))],
        out_specs=pl.BlockSpec((r, GROUP_W), lambda b, i: (b * nblk + i, 0)),
        scratch_shapes=[pltpu.VMEM((2, V7X_LANES, V7X_LANES), F32), slabs, slabs, slabs, rows, rows, rows,
                        pltpu.VMEM((r // SUB, GROUP_W), F32), pltpu.VMEM((r // win, GROUP_W, win), BF16)],
        name="hgrn2",
        compiler_params=pltpu.CompilerParams(dimension_semantics=("arbitrary", "arbitrary"),
                                             vmem_limit_bytes=vmem),
    )(hg, hg, hg, hg, hg, hg, lb, ee)


def _s5_local_kernel(u0_ref, u1_ref, bd_ref, bbd_ref, we_ref, d_ref, y_ref, e_ref, ua_ref):
    r = u0_ref.shape[0]
    g = r // SUB
    n = S5_GROUPS * S5_P
    for s in range(SUB):
        ua_ref[s * g:(s + 1) * g, :] = jnp.concatenate(
            [u0_ref[pl.ds(s, g, stride=SUB), :], u1_ref[pl.ds(s, g, stride=SUB), :]], axis=1)
    ua = ua_ref[...]
    ub = ua.astype(BF16)
    y_ref[...] = d_ref[...] * ua
    for j in range(SUB):
        y_ref[j * g:, :] += jnp.dot(ub[:(SUB - j) * g], bd_ref[j], preferred_element_type=F32)
    bu = jnp.dot(ub, bbd_ref[...], preferred_element_type=F32)
    er = jnp.zeros((g, n), F32)
    ei = jnp.zeros((g, n), F32)
    for s in range(SUB):
        bur, bui = bu[s * g:(s + 1) * g, :n], bu[s * g:(s + 1) * g, n:]
        wr, wi = we_ref[s:s + 1, :n], we_ref[s:s + 1, n:]
        er = er + (wr * bur - wi * bui)
        ei = ei + (wr * bui + wi * bur)
    e_ref[:, :n] = er
    e_ref[:, n:] = ei


def _s5_scan_kernel(e_ref, a_ref, x_ref):
    nchunk = e_ref.shape[0]
    n = S5_GROUPS * S5_P
    xr, xi = e_ref[:, :n], e_ref[:, n:]
    pr, pi = a_ref[:, :n], a_ref[:, n:]
    row = _iota((nchunk, n), 0)
    k = 1
    while k < nchunk:
        sr = jnp.where(row >= k, pltpu.roll(xr, k, 0), 0.0)
        si = jnp.where(row >= k, pltpu.roll(xi, k, 0), 0.0)
        xr, xi = xr + pr * sr - pi * si, xi + pr * si + pi * sr
        pr, pi = pr * pr - pi * pi, 2.0 * pr * pi
        k *= 2
    x_ref[:, :n] = jnp.where(row >= 1, pltpu.roll(xr, 1, 0), 0.0)
    x_ref[:, n:] = jnp.where(row >= 1, pltpu.roll(xi, 1, 0), 0.0)


def _s5_out_kernel(y1_ref, x_ref, a1_ref, cbd_ref, wg_ref, bg_ref, o_ref, xs_ref, nat_ref):
    r = y1_ref.shape[0]
    g = r // SUB
    n = S5_GROUPS * S5_P
    xr, xi = x_ref[:, :n], x_ref[:, n:]
    for s in range(SUB):
        ar, ai = a1_ref[s:s + 1, :n], a1_ref[s:s + 1, n:]
        xs_ref[s * g:(s + 1) * g, :n] = (ar * xr - ai * xi).astype(BF16)
        xs_ref[s * g:(s + 1) * g, n:] = (ar * xi + ai * xr).astype(BF16)
    y = y1_ref[...] + jnp.dot(xs_ref[...], cbd_ref[...], preferred_element_type=F32)
    zact = 0.5 * y * (1.0 + jnp.tanh(math.sqrt(2.0 / math.pi) * (y + 0.044715 * (y * y * y))))
    gate = jnp.dot(zact.astype(BF16), wg_ref[...], preferred_element_type=F32) + bg_ref[...]
    out = zact * _sigmoid(gate)
    for s in range(SUB):
        for half in range(2):
            nat_ref[half, pl.ds(s, g, stride=SUB), :] = out[s * g:(s + 1) * g,
                                                            half * V7X_LANES:(half + 1) * V7X_LANES]
    for half in range(2):
        o_ref[:, half * V7X_LANES:(half + 1) * V7X_LANES] = nat_ref[half]


def _s5_prep_kernel(ar_ref, ai_ref, bre_ref, bim_ref, cre_ref, cim_ref,
                    bd_ref, bbd_ref, cbd_ref, wend_ref, ain_ref, achunk_ref):
    n = S5_GROUPS * S5_P
    ar, ai = ar_ref[...], ai_ref[...]
    bre, bim, cre, cim = bre_ref[...], bim_ref[...], cre_ref[...], cim_ref[...]
    bbd_ref[:, :n] = bre.astype(BF16)
    bbd_ref[:, n:] = bim.astype(BF16)
    cbd_ref[:n, :] = cre.astype(BF16)
    cbd_ref[n:, :] = (-cim).astype(BF16)

    def split2(x):
        hi = x.astype(BF16)
        return hi, (x - hi.astype(F32)).astype(BF16)

    def dot3(x, w_hi, w_lo):
        x_hi, x_lo = split2(x)
        return (jnp.dot(x_hi, w_hi, preferred_element_type=F32) + jnp.dot(x_hi, w_lo, preferred_element_type=F32)
                + jnp.dot(x_lo, w_hi, preferred_element_type=F32))

    cre_parts, cim_parts = split2(cre), split2(cim)
    pr, pi = jnp.ones_like(ar), jnp.zeros_like(ar)
    for j in range(SUB):
        k_lag = dot3(bre * pr - bim * pi, *cre_parts) - dot3(bre * pi + bim * pr, *cim_parts)
        bd_ref[j] = k_lag.astype(BF16)
        wend_ref[SUB - 1 - j:SUB - j, :n] = pr
        wend_ref[SUB - 1 - j:SUB - j, n:] = pi
        pr, pi = pr * ar - pi * ai, pr * ai + pi * ar
        ain_ref[j:j + 1, :n] = pr
        ain_ref[j:j + 1, n:] = pi
    achunk_ref[:, :n] = pr
    achunk_ref[:, n:] = pi


def _s5_operators(lam_re, lam_im, log_step, b_re, b_im, c_re, c_im):
    depth = lam_re.shape[0]
    step = jnp.exp(log_step.astype(F32))[..., None]
    lre = jnp.minimum(lam_re.astype(F32), -1e-4)
    lim = lam_im.astype(F32)
    mag = jnp.exp(lre * step)
    a_re, a_im = mag * jnp.cos(lim * step), mag * jnp.sin(lim * step)
    den = lre * lre + lim * lim
    coef_re = ((a_re - 1.0) * lre + a_im * lim) / den
    coef_im = (a_im * lre - (a_re - 1.0) * lim) / den
    br, bi = b_re.astype(F32), b_im.astype(F32)
    bb_re = coef_re[..., None] * br - coef_im[..., None] * bi
    bb_im = coef_re[..., None] * bi + coef_im[..., None] * br
    cr, ci = c_re.astype(F32), c_im.astype(F32)
    n = S5_GROUPS * S5_P
    same_group = (np.arange(GROUP_W)[:, None] // S5_CH) == (np.arange(n)[None, :] // S5_P)
    in_mask = jnp.asarray(same_group, F32)
    out_mask = jnp.asarray(same_group.T, F32)

    def in_side(b):
        return jnp.tile(b.transpose(0, 1, 3, 2).reshape(depth, GROUP_W, S5_P), (1, 1, S5_GROUPS)) * in_mask

    def out_side(c):
        return jnp.tile(c.transpose(0, 1, 3, 2).reshape(depth, n, S5_CH), (1, 1, S5_GROUPS)) * out_mask

    args = (a_re.reshape(depth, 1, n), a_im.reshape(depth, 1, n), in_side(bb_re), in_side(bb_im),
            out_side(cr), out_side(ci))
    out_dims = (((SUB, GROUP_W, GROUP_W), BF16),
                ((GROUP_W, 2 * n), BF16),
                ((2 * n, GROUP_W), BF16),
                ((SUB, 2 * n), F32),
                ((SUB, 2 * n), F32),
                ((1, 2 * n), F32))

    def per_layer(shape):
        return pl.BlockSpec((None,) + shape, lambda l: (l,) + (0,) * len(shape))

    return pl.pallas_call(
        _s5_prep_kernel,
        out_shape=tuple(jax.ShapeDtypeStruct((depth,) + s, d) for s, d in out_dims),
        grid=(depth,),
        in_specs=[per_layer(a.shape[1:]) for a in args],
        out_specs=tuple(per_layer(s) for s, _ in out_dims),
        name="s5_prep",
        compiler_params=pltpu.CompilerParams(
            dimension_semantics=("arbitrary",),
            vmem_limit_bytes=_vmem_limit(8 * _nbytes((GROUP_W, n), F32), 10 * _nbytes((GROUP_W, n), F32),
                                         4 * _nbytes((SUB, GROUP_W, GROUP_W), BF16))),
    )(*args)


def _s5(u, layer, ops, d_skip, w_glu, b_glu, batch, seq, r):
    t = u.shape[0]
    bd, bbd, cbd, w_end, a_in, a_chunk = ops
    n2 = 2 * S5_GROUPS * S5_P
    nck = seq // SUB

    def whole(a):
        return _layer_block(a, layer)

    rows = pl.BlockSpec((r, GROUP_W), lambda i: (i, 0))
    crow = pl.BlockSpec((r // SUB, n2), lambda i: (i, 0))
    y1, e = pl.pallas_call(
        _s5_local_kernel,
        out_shape=(jax.ShapeDtypeStruct((t, GROUP_W), F32), jax.ShapeDtypeStruct((t // SUB, n2), F32)),
        grid=(t // r,),
        in_specs=[pl.BlockSpec((r, V7X_LANES), lambda i: (i, 0)), pl.BlockSpec((r, V7X_LANES), lambda i: (i, 1)),
                  whole(bd), whole(bbd), whole(w_end), whole(d_skip)],
        out_specs=(rows, crow),
        scratch_shapes=[pltpu.VMEM((r, GROUP_W), F32)],
        name="s5_local",
        compiler_params=pltpu.CompilerParams(
            dimension_semantics=("arbitrary",),
            vmem_limit_bytes=_vmem_limit(_nbytes(bd.shape[1:], BF16), _nbytes(bbd.shape[1:], BF16),
                                         6 * _nbytes((r, n2), F32), 8 * _nbytes((r, GROUP_W), F32))),
    )(u, u, bd, bbd, w_end, d_skip)
    xin = pl.pallas_call(
        _s5_scan_kernel,
        out_shape=jax.ShapeDtypeStruct((t // SUB, n2), F32),
        grid=(batch,),
        in_specs=[pl.BlockSpec((nck, n2), lambda b: (b, 0)), whole(a_chunk)],
        out_specs=pl.BlockSpec((nck, n2), lambda b: (b, 0)),
        name="s5_scan",
        compiler_params=pltpu.CompilerParams(
            dimension_semantics=("arbitrary",), vmem_limit_bytes=_vmem_limit(10 * _nbytes((nck, n2), F32))),
    )(e, a_chunk)
    return pl.pallas_call(
        _s5_out_kernel,
        out_shape=jax.ShapeDtypeStruct((t, GROUP_W), F32),
        grid=(t // r,),
        in_specs=[rows, crow, whole(a_in), whole(cbd), whole(w_glu), whole(b_glu)],
        out_specs=rows,
        scratch_shapes=[pltpu.VMEM((r, n2), BF16), pltpu.VMEM((2, r, V7X_LANES), F32)],
        name="s5_out",
        compiler_params=pltpu.CompilerParams(
            dimension_semantics=("arbitrary",),
            vmem_limit_bytes=_vmem_limit(_nbytes(cbd.shape[1:], BF16), 3 * _nbytes((r, n2), BF16),
                                         10 * _nbytes((r, GROUP_W), F32))),
    )(y1, xin, a_in, cbd, w_glu, b_glu)


HALO = 2 * V7X_SUBLANES


def _post_kernel(tiles_per_seq, final, ya_ref, yb_ref, yc_ref, gate_ref, yd_ref, h_ref, p_ref,
                 gn_ref, wo_ref, fg_ref, wu_ref, cw_ref, cb_ref, wd_ref, pg_ref, wpg_ref, wp_ref, ng_ref,
                 o_ref, xn_ref, up_ref, act_ref):
    tm = h_ref.shape[0]
    gw = GROUP_W

    @pl.when(pl.program_id(0) % tiles_per_seq == 0)
    def _():
        xn_ref[0:HALO, :] = jnp.zeros((HALO, D_MODEL), BF16)

    parts = (_rms(ya_ref[...], gn_ref[:, 0:gw]),
             _rms(yb_ref[...], gn_ref[:, gw:2 * gw]),
             _rms(yc_ref[...], gn_ref[:, 2 * gw:3 * gw]) * _sigmoid(gate_ref[...]),
             _rms(yd_ref[...], gn_ref[:, 3 * gw:4 * gw]))
    h1 = h_ref[...]
    for g, part in enumerate(parts):
        h1 = h1 + jnp.dot(part.astype(BF16), wo_ref[g * gw:(g + 1) * gw, :], preferred_element_type=F32)
    o_ref[...] = h1
    xn_ref[HALO:, :] = _rms(h1, fg_ref[...]).astype(BF16)

    nchunk = D_FF // MXU_TILE

    def conv(slot, c):
        cols = slice(c * MXU_TILE, (c + 1) * MXU_TILE)
        up_ref[slot] = jnp.dot(xn_ref[...], wu_ref[:, cols], preferred_element_type=F32)
        return (cb_ref[:, cols] + cw_ref[0:1, cols] * up_ref[slot, pl.ds(HALO - 2, tm), :]
                + cw_ref[1:2, cols] * up_ref[slot, pl.ds(HALO - 1, tm), :]
                + cw_ref[2:3, cols] * up_ref[slot, pl.ds(HALO, tm), :])

    for c in range(nchunk):
        slot = 2 * (c % 2)
        gate = conv(slot, c)
        val = conv(slot + 1, c + nchunk)
        act_ref[:, c * MXU_TILE:(c + 1) * MXU_TILE] = (gate * _sigmoid(gate) * val).astype(BF16)
    xn_ref[0:HALO, :] = xn_ref[tm:tm + HALO, :]
    h2 = o_ref[...] + jnp.dot(act_ref[...], wd_ref[...], preferred_element_type=F32)

    pgate = _sigmoid(jnp.dot(_rms(h2, pg_ref[...]).astype(BF16), wpg_ref[...], preferred_element_type=F32))
    out = h2 + pgate * jnp.dot(p_ref[...].astype(BF16), wp_ref[...], preferred_element_type=F32)
    o_ref[...] = _rms(out, ng_ref[...]) if final else out


def _post(ya, yb, yc, hg, yd, h, p, seq, layer, stacked, final, tm):
    t = h.shape[0]
    rows = pl.BlockSpec((tm, GROUP_W), lambda i: (i, 0))
    wide = pl.BlockSpec((tm, D_MODEL), lambda i: (i, 0))
    consts = tuple(stacked)
    vmem = _vmem_limit(sum(_nbytes(c.shape[1:], c.dtype) for c in consts), 12 * _nbytes((tm, GROUP_W), F32),
                       4 * _nbytes((tm, D_MODEL), F32), _nbytes((tm + HALO, D_MODEL), BF16),
                       4 * _nbytes((tm + HALO, MXU_TILE), F32), _nbytes((tm, D_FF), BF16),
                       3 * _nbytes((tm, D_MODEL), F32))
    return pl.pallas_call(
        functools.partial(_post_kernel, seq // tm, final),
        out_shape=jax.ShapeDtypeStruct((t, D_MODEL), F32),
        grid=(t // tm,),
        in_specs=[rows, rows, rows, pl.BlockSpec((tm, GROUP_W), lambda i: (i, 3)), rows, wide,
                  pl.BlockSpec((tm, PLE_DIM), lambda i: (layer * (t // tm) + i, 0))]
        + [_layer_block(c, layer) for c in consts],
        out_specs=wide,
        scratch_shapes=[pltpu.VMEM((tm + HALO, D_MODEL), BF16), pltpu.VMEM((4, tm + HALO, MXU_TILE), F32),
                        pltpu.VMEM((tm, D_FF), BF16)],
        name="post_mixer",
        compiler_params=pltpu.CompilerParams(dimension_semantics=("arbitrary",), vmem_limit_bytes=vmem),
    )(ya, yb, yc, hg, yd, h, p, *consts)


def _tiles(seq):
    return dict(tin=min(512, seq), tm=min(512, seq), tq=min(512, seq), tkv=min(512, seq), r=min(512, seq),
                rs5=min(1024, seq))


def kernel(x, p, positions, attn_norm_g, w_in, mla_q_norm_g, mla_w_uq, mla_kv_norm_g, mla_w_ukv, fox_b_f,
           hgrn_lb_param, s5_lam_re, s5_lam_im, s5_log_step, s5_b_re, s5_b_im, s5_c_re, s5_c_im, s5_d,
           s5_w_glu, s5_b_glu, group_norm_g, w_out, ffn_norm_g, w_up, conv_w, conv_b, w_down, ple_norm_g,
           w_ple_gate, w_ple, final_norm_g):
    batch, seq, _ = x.shape
    depth = w_in.shape[0]
    t = batch * seq
    ts = _tiles(seq)
    assert all(seq % ts[k] == 0 for k in ts) and ts["tq"] % ts["tkv"] == 0
    assert (seq // SUB) & (seq // SUB - 1) == 0, "chunk scan assumes a power-of-two chunk count"

    lb_all = jnp.cumsum(jax.nn.softmax(hgrn_lb_param.astype(F32), axis=0), axis=0)
    lb_all = lb_all - lb_all[0:1]
    tables = _rope_tables(positions, ts["tin"])

    def row(v):
        return v.reshape(v.shape[0], 1, -1)

    in_params = _inproj_params(attn_norm_g, w_in, mla_q_norm_g, mla_w_uq, mla_kv_norm_g, mla_w_ukv, fox_b_f)
    s5_ops = _s5_operators(s5_lam_re, s5_lam_im, s5_log_step, s5_b_re, s5_b_im, s5_c_re, s5_c_im)
    s5_tail = (row(s5_d.astype(F32)), s5_w_glu.astype(BF16), row(s5_b_glu.astype(F32)))
    post_params = (row(group_norm_g), w_out.astype(BF16), row(ffn_norm_g), w_up.astype(BF16), conv_w,
                   row(conv_b), w_down.astype(BF16), row(ple_norm_g), w_ple_gate.astype(BF16),
                   w_ple.astype(BF16), jnp.broadcast_to(final_norm_g.reshape(1, 1, -1), (depth, 1, D_MODEL)))
    h = x.reshape(t, D_MODEL)
    for i in range(depth):
        mq, mk, mv, fq, fk, fv, hg, su = _inproj(h, seq, i, in_params, tables, ts["tin"])
        y_a = _attention(mq, mk, mv, batch, seq, ts["tq"], ts["tkv"])
        y_b = _attention(fq, fk, fv, batch, seq, ts["tq"], ts["tkv"])
        y_c = _hgrn(hg, i, row(lb_all), batch, seq, ts["r"])
        y_d = _s5(su, i, s5_ops, *s5_tail, batch, seq, ts["rs5"])
        h = _post(y_a, y_b, y_c, hg, y_d, h, p.reshape(depth * t, PLE_DIM), seq, i, post_params,
                  i == depth - 1, ts["tm"])
    return h.reshape(batch, seq, D_MODEL)
```

```python
import functools
import math

import numpy as np
import jax
import jax.numpy as jnp
from jax import lax
from jax.experimental import pallas as pl
from jax.experimental.pallas import tpu as pltpu

F32 = jnp.float32
BF16 = jnp.bfloat16

D_MODEL = 1024
N_HEADS = 4
HEAD_DIM = 64
GROUP_W = 256
MLA_Q_RANK = 256
MLA_KV_RANK = 128
MLA_NOPE = 64
MLA_ROPE = 32
ROPE_THETA = 10000.0
S5_GROUPS = 16
S5_CH = 16
S5_P = 64
D_FF = 2816
PLE_DIM = 256
EPS = 1e-6
N_IN = 2468

V7X_LANES = 128
V7X_SUBLANES = 8
V7X_VMEM_BYTES = 64 * 1024 * 1024
MXU_TILE = 256
VMEM_CAP_BYTES = 58 * 1024 * 1024

HEAD_PAD = V7X_LANES
SUB = 16
NEG_BIG = -1e30

SEG_CQ = 0
SEG_CKV = 256
SEG_KR = 384
SEG_FOX = 512
SEG_HG = 1280
SEG_S5 = 2304
SEG_FF = 2560
N_PERM = 2688
ROPE_LANE0 = 64
BIAS_LANE0 = 64
LOG2E = math.log2(math.e)


def _vmem_limit(*byte_counts):
    need = int(sum(byte_counts))
    return int(min(VMEM_CAP_BYTES, need + need // 4 + (4 << 20)))


def _nbytes(shape, dtype):
    return int(np.prod(shape)) * jnp.dtype(dtype).itemsize


def _rms(x, g):
    return x * lax.rsqrt(jnp.mean(x * x, axis=-1, keepdims=True) + EPS) * g


def _log_sigmoid(z):
    return jnp.minimum(z, 0.0) - jnp.log1p(jnp.exp(-jnp.abs(z)))


def _sigmoid(z):
    return 1.0 / (1.0 + jnp.exp(-z))


def _iota(shape, dim):
    return lax.broadcasted_iota(jnp.int32, shape, dim)


ROPE_PACK = HEAD_PAD // MLA_ROPE


def _rope_kernel(pos_ref, freq_ref, ct_ref, s1_ref, s2_ref):
    ang = pos_ref[...].astype(F32) * freq_ref[...]
    cos, sin = jnp.cos(ang), jnp.sin(ang)
    g = ang.shape[0]
    lane = _iota(ang.shape, 1)
    half = MLA_ROPE // 2
    first = (lane >= ROPE_LANE0) & (lane < ROPE_LANE0 + half)
    second = (lane >= ROPE_LANE0 + half) & (lane < ROPE_LANE0 + 2 * half)
    for a in range(ROPE_PACK):
        shift = (ROPE_LANE0 - MLA_ROPE * a) % HEAD_PAD
        ca = pltpu.roll(cos, shift, 1) if shift else cos
        sa = pltpu.roll(sin, shift, 1) if shift else sin
        rows = pl.ds(a, g, stride=ROPE_PACK)
        ct_ref[rows, :] = jnp.where(first | second, ca, 1.0)
        s1_ref[rows, :] = jnp.where(first, -sa, 0.0)
        s2_ref[rows, :] = jnp.where(second, sa, 0.0)


def _rope_tables(positions, tm):
    t = positions.size
    half = MLA_ROPE // 2
    inv_freq = ROPE_THETA ** (-jnp.arange(half, dtype=F32) / half)
    freq = jnp.tile(inv_freq, 2 * ROPE_PACK).reshape(1, HEAD_PAD)
    pos = jnp.repeat(positions.reshape(t // ROPE_PACK, ROPE_PACK), MLA_ROPE, axis=1)
    out = jax.ShapeDtypeStruct((t, HEAD_PAD), F32)
    spec = pl.BlockSpec((tm, HEAD_PAD), lambda i: (i, 0))
    return pl.pallas_call(
        _rope_kernel,
        out_shape=(out, out, out),
        grid=(t // tm,),
        in_specs=[pl.BlockSpec((tm // ROPE_PACK, HEAD_PAD), lambda i: (i, 0)),
                  pl.BlockSpec((1, HEAD_PAD), lambda i: (0, 0))],
        out_specs=(spec, spec, spec),
        name="rope_tables",
        compiler_params=pltpu.CompilerParams(dimension_semantics=("arbitrary",)),
    )(pos, freq)


def _inproj_kernel(tiles_per_seq, h_ref, g_ref, w_ref, qg_ref, wuq_ref, kvg_ref, wukv_ref, bf_ref,
                   ct_ref, s1_ref, s2_ref, selq_ref, selk_ref,
                   mq_ref, mk_ref, mv_ref, fq_ref, fk_ref, fv_ref, hg_ref, su_ref, carry_ref):
    i = pl.program_id(0)

    @pl.when(i % tiles_per_seq == 0)
    def _():
        carry_ref[...] = jnp.zeros_like(carry_ref)

    tm = h_ref.shape[0]
    xn = _rms(h_ref[...], g_ref[...]).astype(BF16)

    proj = jnp.dot(xn, w_ref[...], preferred_element_type=F32)

    def seg(a, b):
        return proj[:, a:b]

    q = jnp.dot(_rms(seg(SEG_CQ, SEG_CKV), qg_ref[...]).astype(BF16), wuq_ref[...],
                preferred_element_type=F32)
    ckv_kr = seg(SEG_CKV, SEG_FOX)
    kv = jnp.dot(_rms(ckv_kr[:, :MLA_KV_RANK], kvg_ref[...]).astype(BF16), wukv_ref[...],
                 preferred_element_type=F32)
    ct, s1, s2 = ct_ref[...], s1_ref[...], s2_ref[...]
    half = MLA_ROPE // 2

    def rope(t):
        return t * ct + pltpu.roll(t, HEAD_PAD - half, 1) * s1 + pltpu.roll(t, half, 1) * s2

    k_pe = rope(ckv_kr[:, MLA_KV_RANK:])
    mla_scale = (MLA_NOPE + MLA_ROPE) ** -0.5 * LOG2E
    for hd in range(N_HEADS):
        sl = slice(hd * HEAD_PAD, (hd + 1) * HEAD_PAD)
        mq_ref[:, sl] = (rope(q[:, sl]) * mla_scale).astype(BF16)
        mk_ref[:, sl] = (kv[:, sl] + k_pe).astype(BF16)
    hp4 = N_HEADS * HEAD_PAD
    ones_pad = jnp.where((_iota((1, hp4), 1) & (HEAD_PAD - 1)) >= HEAD_DIM, 1.0, 0.0)
    mv_ref[...] = (kv[:, hp4:] + ones_pad).astype(BF16)

    lane = _iota((tm, HEAD_PAD), 1)
    keep = lane < N_HEADS

    def parts3(x):
        a = x.astype(BF16).astype(F32)
        r = x - a
        b = r.astype(BF16).astype(F32)
        c = (r - b).astype(BF16).astype(F32)
        return (jnp.where(keep, a, 0.0) + pltpu.roll(jnp.where(keep, b, 0.0), N_HEADS, 1)
                + pltpu.roll(jnp.where(keep, c, 0.0), 2 * N_HEADS, 1))

    lf = _log_sigmoid(seg(SEG_FF, N_PERM) + bf_ref[...])
    tril = (_iota((tm, tm), 0) >= _iota((tm, tm), 1)).astype(BF16)
    cum3 = jnp.dot(tril, parts3(lf).astype(BF16), preferred_element_type=F32)
    cum = (cum3 + pltpu.roll(cum3, HEAD_PAD - N_HEADS, 1) + pltpu.roll(cum3, HEAD_PAD - 2 * N_HEADS, 1)
           + carry_ref[...])
    carry_ref[...] = cum[tm - 1:tm, :]
    parts = (parts3(cum * LOG2E) + jnp.where(lane == 3 * N_HEADS, 1.0, 0.0)).astype(BF16)
    bias_q = jnp.dot(parts, selq_ref[...], preferred_element_type=F32)
    bias_k = jnp.dot(parts, selk_ref[...], preferred_element_type=F32)
    fox = seg(SEG_FOX, SEG_HG)
    fox_scale = HEAD_DIM ** -0.5 * LOG2E
    low = lane < HEAD_DIM
    for hd in range(N_HEADS):
        sl = slice(hd * HEAD_PAD, (hd + 1) * HEAD_PAD)
        src = slice((hd // 2) * HEAD_PAD, (hd // 2 + 1) * HEAD_PAD)

        def head_tile(x):
            tile = x[:, src]
            return pltpu.roll(tile, HEAD_DIM, 1) if hd % 2 else tile

        fq_ref[:, sl] = jnp.where(low, head_tile(fox[:, 0:GROUP_W]) * fox_scale, bias_q[:, sl]).astype(BF16)
        fk_ref[:, sl] = jnp.where(low, head_tile(fox[:, GROUP_W:2 * GROUP_W]), bias_k[:, sl]).astype(BF16)
        fv_ref[:, sl] = jnp.where(low, head_tile(fox[:, 2 * GROUP_W:]), 1.0).astype(BF16)

    hg_ref[...] = seg(SEG_HG, SEG_S5)
    su_ref[...] = seg(SEG_S5, SEG_FF)


def _permute_inproj(w):
    def zeros(n):
        return jnp.zeros(w.shape[:-1] + (n,), w.dtype)

    cols = [w[..., 0:384], zeros(ROPE_LANE0), w[..., 384:416], zeros(HEAD_PAD - ROPE_LANE0 - MLA_ROPE),
            w[..., 416:1184], w[..., 1188:N_IN], w[..., 1184:1188], zeros(HEAD_PAD - N_HEADS)]
    out = jnp.concatenate(cols, axis=-1)
    assert out.shape[-1] == N_PERM
    return out


def _bias_selectors():
    selq = np.zeros((HEAD_PAD, N_HEADS * HEAD_PAD), np.float32)
    selk = np.zeros((HEAD_PAD, N_HEADS * HEAD_PAD), np.float32)
    one = 3 * N_HEADS
    for hd in range(N_HEADS):
        for j in range(3):
            selq[N_HEADS * j + hd, hd * HEAD_PAD + BIAS_LANE0 + j] = 1.0
            selq[one, hd * HEAD_PAD + BIAS_LANE0 + 3 + j] = 1.0
            selk[one, hd * HEAD_PAD + BIAS_LANE0 + j] = 1.0
            selk[N_HEADS * j + hd, hd * HEAD_PAD + BIAS_LANE0 + 3 + j] = -1.0
    return jnp.asarray(selq, BF16), jnp.asarray(selk, BF16)


def _layer_block(a, layer):
    shape = a.shape[1:]
    return pl.BlockSpec((None,) + shape, lambda *_: (layer,) + (0,) * len(shape), pipeline_mode=pl.Buffered(1))


def _inproj_params(attn_g, w_in, q_g, w_uq, kv_g, w_ukv, b_f):
    depth = w_in.shape[0]
    wuq = jnp.pad(w_uq.reshape(depth, MLA_Q_RANK, N_HEADS, MLA_NOPE + MLA_ROPE),
                  ((0, 0), (0, 0), (0, 0), (0, HEAD_PAD - MLA_NOPE - MLA_ROPE)))
    wkv = w_ukv.reshape(depth, MLA_KV_RANK, N_HEADS, 2 * HEAD_DIM)
    head_pad = ((0, 0), (0, 0), (0, 0), (0, HEAD_PAD - HEAD_DIM))
    wk = jnp.pad(wkv[..., :MLA_NOPE], head_pad).reshape(depth, MLA_KV_RANK, -1)
    wv = jnp.pad(wkv[..., MLA_NOPE:], head_pad).reshape(depth, MLA_KV_RANK, -1)
    bf = jnp.pad(b_f.astype(F32), ((0, 0), (0, HEAD_PAD - N_HEADS)))

    def row(v):
        return v.reshape(depth, 1, -1)

    return (row(attn_g), _permute_inproj(w_in.astype(BF16)), row(q_g),
            wuq.reshape(depth, MLA_Q_RANK, -1).astype(BF16), row(kv_g),
            jnp.concatenate([wk, wv], axis=2).astype(BF16), row(bf))


def _inproj(h, seq, layer, stacked, tables, tm):
    t = h.shape[0]
    selq, selk = _bias_selectors()
    ct, s1, s2 = tables
    hp4 = N_HEADS * HEAD_PAD
    w_perm = stacked[1]

    def rows(width):
        return pl.BlockSpec((tm, width), lambda i: (i, 0))

    def whole(a):
        return pl.BlockSpec(a.shape, lambda i: (0,) * a.ndim)

    args = (h,) + tuple(stacked) + (ct, s1, s2, selq, selk)
    in_specs = ([rows(D_MODEL)] + [_layer_block(a, layer) for a in stacked]
                + [rows(HEAD_PAD)] * 3 + [whole(selq), whole(selk)])
    out_widths = (hp4,) * 6
    out_shape = tuple(jax.ShapeDtypeStruct((t, w), BF16) for w in out_widths) + (
        jax.ShapeDtypeStruct((t, 4 * GROUP_W), F32), jax.ShapeDtypeStruct((t, GROUP_W), F32))
    out_specs = tuple(rows(w) for w in out_widths) + (rows(4 * GROUP_W), rows(GROUP_W))
    vmem = _vmem_limit(_nbytes(w_perm.shape[1:], BF16), 2 * _nbytes((tm, D_MODEL), F32),
                       2 * sum(_nbytes((tm, w), BF16) for w in out_widths),
                       2 * _nbytes((tm, 5 * GROUP_W), F32), _nbytes((tm, N_PERM), F32),
                       _nbytes((tm, D_MODEL), F32))
    return pl.pallas_call(
        functools.partial(_inproj_kernel, seq // tm),
        out_shape=out_shape,
        grid=(t // tm,),
        in_specs=in_specs,
        out_specs=out_specs,
        scratch_shapes=[pltpu.VMEM((1, HEAD_PAD), F32)],
        name="inproj",
        compiler_params=pltpu.CompilerParams(dimension_semantics=("arbitrary",), vmem_limit_bytes=vmem),
    )(*args)


def _attn_kernel(tq, tkv, q_ref, k_ref, v_ref, o_ref, m_ref, acc_ref):
    qi = pl.program_id(1)
    m_ref[...] = jnp.full_like(m_ref, NEG_BIG)
    acc_ref[...] = jnp.zeros_like(acc_ref)
    per_q = tq // tkv
    heads = [slice(hd * HEAD_PAD, (hd + 1) * HEAD_PAD) for hd in range(N_HEADS)]

    def key_rows(start, width):
        return pl.ds(pl.multiple_of(start, tkv), width)

    def logits_of(start, width):
        rows = key_rows(start, width)
        return [lax.dot_general(q_ref[:, sl], k_ref[rows, sl], (((1,), (1,)), ((), ())),
                                preferred_element_type=F32) for sl in heads]

    def finish(logits, start, width, diag_offset):
        rows = key_rows(start, width)
        probs, rescale = [], []
        for hd in range(N_HEADS):
            s = logits[hd]
            if diag_offset is not None:
                visible = _iota(s.shape, 1) + diag_offset <= _iota(s.shape, 0)
                s = jnp.where(visible, s, NEG_BIG)
            m_old = m_ref[hd]
            m_new = jnp.maximum(m_old, jnp.max(s, axis=1, keepdims=True))
            probs.append(jnp.exp2(s - jnp.concatenate([m_new] * (width // HEAD_PAD), axis=1)).astype(BF16))
            rescale.append(jnp.exp2(m_old - m_new))
            m_ref[hd] = m_new
        for hd, sl in enumerate(heads):
            acc_ref[hd] = (rescale[hd] * acc_ref[hd]
                           + jnp.dot(probs[hd], v_ref[rows, sl], preferred_element_type=F32))

    def block(start, width, diag_offset):
        finish(logits_of(start, width), start, width, diag_offset)

    n_full = qi * per_q

    def wide_block(jj, carry):
        block(jj * (3 * tkv), 3 * tkv, None)
        return carry

    n_wide = n_full // 3
    lax.fori_loop(0, n_wide, wide_block, 0)
    left = n_full - 3 * n_wide

    @pl.when(left == 2)
    def _():
        block((n_full - 2) * tkv, 2 * tkv, None)

    @pl.when(left == 1)
    def _():
        block((n_full - 1) * tkv, tkv, None)

    for r in range(per_q):
        block((n_full + r) * tkv, tkv, r * tkv)

    low = _iota((tq, HEAD_PAD), 1) < HEAD_DIM
    for pair in range(N_HEADS // 2):
        a0, a1 = acc_ref[2 * pair], acc_ref[2 * pair + 1]
        n0 = a0 / pltpu.roll(a0, HEAD_DIM, 1)
        n1 = a1 / pltpu.roll(a1, HEAD_DIM, 1)
        o_ref[:, pair * HEAD_PAD:(pair + 1) * HEAD_PAD] = jnp.where(low, n0, pltpu.roll(n1, HEAD_DIM, 1))


def _attention(q, k, v, batch, seq, tq, tkv):
    t = q.shape[0]
    hp4 = N_HEADS * HEAD_PAD
    vmem = _vmem_limit(2 * _nbytes((seq, hp4), BF16), 2 * _nbytes((tq, hp4), BF16),
                       2 * _nbytes((tq, GROUP_W), F32), 2 * N_HEADS * _nbytes((tq, HEAD_PAD), F32),
                       6 * _nbytes((tq, 2 * tkv), F32))
    resident = pl.BlockSpec((seq, hp4), lambda b, i: (b, 0), pipeline_mode=pl.Buffered(1))
    return pl.pallas_call(
        functools.partial(_attn_kernel, tq, tkv),
        out_shape=jax.ShapeDtypeStruct((t, GROUP_W), F32),
        grid=(batch, seq // tq),
        in_specs=[pl.BlockSpec((tq, hp4), lambda b, i: (b * (seq // tq) + i, 0)), resident, resident],
        out_specs=pl.BlockSpec((tq, GROUP_W), lambda b, i: (b * (seq // tq) + i, 0)),
        scratch_shapes=[pltpu.VMEM((N_HEADS, tq, HEAD_PAD), F32), pltpu.VMEM((N_HEADS, tq, HEAD_PAD), F32)],
        name="causal_attention",
        compiler_params=pltpu.CompilerParams(dimension_semantics=("arbitrary", "arbitrary"),
                                             vmem_limit_bytes=vmem),
    )(q, k, v)


def _hgrn_kernel(win, q0_ref, q1_ref, f0_ref, f1_ref, v0_ref, v1_ref, lb_ref, ee_ref, o_ref,
                 st_ref, bc_ref, kk_ref, vs_ref, qe_ref, ke_ref, od_ref, dec_ref, vt_ref):
    @pl.when(pl.program_id(1) == 0)
    def _():
        st_ref[...] = jnp.zeros_like(st_ref)

    q_ref, f_ref, v_ref = (q0_ref, q1_ref), (f0_ref, f1_ref), (v0_ref, v1_ref)
    r = q0_ref.shape[0]
    g = r // SUB
    lb = lb_ref[...]
    log_lb = jnp.log(lb)
    log_1m = jnp.log1p(-lb)

    def slab(halves, s):
        return jnp.concatenate([h[pl.ds(s, g, stride=SUB), :] for h in halves], axis=1)

    def put_rows(ref, s, x):
        for half in range(2):
            ref[half, pl.ds(s, g, stride=SUB), :] = x[:, half * V7X_LANES:(half + 1) * V7X_LANES]

    run = None
    for s in range(SUB):
        z = slab(f_ref, s)
        b = log_1m + _log_sigmoid(z)
        log_f = jnp.maximum(log_lb, b) + jnp.log1p(jnp.exp(-jnp.abs(log_lb - b)))
        run = log_f if s == 0 else run + log_f
        bc_ref[s] = run
        kk_ref[s] = (1.0 - lb) * _sigmoid(-z)
        vs_ref[s] = slab(v_ref, s)
    total = bc_ref[SUB - 1]
    dec_ref[...] = jnp.exp(total)

    for s in range(SUB):
        qs = slab(q_ref, s)
        bcs = bc_ref[s]
        prods = [(qs * kk_ref[j] * jnp.exp(bcs - bc_ref[j])).astype(BF16) for j in range(s)]
        prods.append((qs * kk_ref[s]).astype(BF16))
        red = jnp.dot(jnp.concatenate(prods, axis=0), ee_ref[...], preferred_element_type=F32)
        od = red[0:g] * vs_ref[0]
        for j in range(1, s + 1):
            od = od + red[j * g:(j + 1) * g] * vs_ref[j]
        put_rows(od_ref, s, od)
        put_rows(qe_ref, s, qs * jnp.exp(bcs))
        put_rows(ke_ref, s, kk_ref[s] * jnp.exp(total - bcs))
    for w in range(r // win):
        for half in range(2):
            vt_ref[w, half * V7X_LANES:(half + 1) * V7X_LANES, :] = (
                v_ref[half][w * win:(w + 1) * win, :].T.astype(BF16))

    hl = V7X_LANES
    same_head = (_iota((hl, hl), 0) // HEAD_DIM) == (_iota((hl, hl), 1) // HEAD_DIM)
    per_win = win // SUB

    def window(w, carry):
        halves = (0, 1)
        lanes = [slice(half * hl, (half + 1) * hl) for half in halves]

        def rows_of(c):
            return pl.ds(pl.multiple_of(w * win + c * SUB, SUB), SUB)

        upds = [[], []]
        for half in halves:
            vt = vt_ref[w, lanes[half], :]
            for c in range(per_win):
                pieces = [ke_ref[half, rows_of(c), :].astype(BF16)]
                if c:
                    pieces.insert(0, jnp.zeros((c * SUB, hl), BF16))
                if c < per_win - 1:
                    pieces.append(jnp.zeros(((per_win - 1 - c) * SUB, hl), BF16))
                upd = jnp.dot(vt, jnp.concatenate(pieces, axis=0), preferred_element_type=F32)
                upds[half].append(jnp.where(same_head, upd, 0.0))
        st = [st_ref[half] for half in halves]
        for c in range(per_win):
            rows = rows_of(c)
            dec = dec_ref[pl.ds(w * per_win + c, 1), :]
            for half in halves:
                o_state = lax.dot_general(qe_ref[half, rows, :].astype(BF16), st[half].astype(BF16),
                                          (((1,), (1,)), ((), ())), preferred_element_type=F32)
                o_ref[rows, lanes[half]] = od_ref[half, rows, :] + o_state
                st[half] = st[half] * dec[:, lanes[half]] + upds[half][c]
        for half in halves:
            st_ref[half] = st[half]
        return carry

    lax.fori_loop(0, r // win, window, 0)


def _hgrn(hg, layer, lb, batch, seq, r):
    t = hg.shape[0]
    win = min(r, V7X_LANES)
    ee = jnp.asarray(np.kron(np.eye(N_HEADS, dtype=np.float32), np.ones((HEAD_DIM, HEAD_DIM), np.float32)), BF16)
    nblk = seq // r

    def half(c):
        return pl.BlockSpec((r, V7X_LANES), lambda b, i: (b * nblk + i, c))

    tile = _nbytes((r, GROUP_W), F32)
    vmem = _vmem_limit(8 * tile, 7 * tile, 8 * tile)
    slabs = pltpu.VMEM((SUB, r // SUB, GROUP_W), F32)
    rows = pltpu.VMEM((2, r, V7X_LANES), F32)
    return pl.pallas_call(
        functools.partial(_hgrn_kernel, win),
        out_shape=jax.ShapeDtypeStruct((t, GROUP_W), F32),
        grid=(batch, nblk),
        in_specs=[half(0), half(1), half(2), half(3), half(4), half(5),
                  _layer_block(lb, layer),
                  pl.BlockSpec((GROUP_W, GROUP_W), lambda b, i: (0, 0))],
        out_specs=pl.BlockSpec((r, GROUP_W), lambda b, i: (b * nblk + i, 0)),
        scratch_shapes=[pltpu.VMEM((2, V7X_LANES, V7X_LANES), F32), slabs, slabs, slabs, rows, rows, rows,
                        pltpu.VMEM((r // SUB, GROUP_W), F32), pltpu.VMEM((r // win, GROUP_W, win), BF16)],
        name="hgrn2",
        compiler_params=pltpu.CompilerParams(dimension_semantics=("arbitrary", "arbitrary"),
                                             vmem_limit_bytes=vmem),
    )(hg, hg, hg, hg, hg, hg, lb, ee)


def _s5_local_kernel(u0_ref, u1_ref, bd_ref, bbd_ref, we_ref, d_ref, y_ref, e_ref, ua_ref):
    r = u0_ref.shape[0]
    g = r // SUB
    n = S5_GROUPS * S5_P
    for s in range(SUB):
        ua_ref[s * g:(s + 1) * g, :] = jnp.concatenate(
            [u0_ref[pl.ds(s, g, stride=SUB), :], u1_ref[pl.ds(s, g, stride=SUB), :]], axis=1)
    ua = ua_ref[...]
    ub = ua.astype(BF16)
    y_ref[...] = d_ref[...] * ua
    for j in range(SUB):
        y_ref[j * g:, :] += jnp.dot(ub[:(SUB - j) * g], bd_ref[j], preferred_element_type=F32)
    bu = jnp.dot(ub, bbd_ref[...], preferred_element_type=F32)
    er = jnp.zeros((g, n), F32)
    ei = jnp.zeros((g, n), F32)
    for s in range(SUB):
        bur, bui = bu[s * g:(s + 1) * g, :n], bu[s * g:(s + 1) * g, n:]
        wr, wi = we_ref[s:s + 1, :n], we_ref[s:s + 1, n:]
        er = er + (wr * bur - wi * bui)
        ei = ei + (wr * bui + wi * bur)
    e_ref[:, :n] = er
    e_ref[:, n:] = ei


def _s5_scan_kernel(e_ref, a_ref, x_ref):
    nchunk = e_ref.shape[0]
    n = S5_GROUPS * S5_P
    xr, xi = e_ref[:, :n], e_ref[:, n:]
    pr, pi = a_ref[:, :n], a_ref[:, n:]
    row = _iota((nchunk, n), 0)
    k = 1
    while k < nchunk:
        sr = jnp.where(row >= k, pltpu.roll(xr, k, 0), 0.0)
        si = jnp.where(row >= k, pltpu.roll(xi, k, 0), 0.0)
        xr, xi = xr + pr * sr - pi * si, xi + pr * si + pi * sr
        pr, pi = pr * pr - pi * pi, 2.0 * pr * pi
        k *= 2
    x_ref[:, :n] = jnp.where(row >= 1, pltpu.roll(xr, 1, 0), 0.0)
    x_ref[:, n:] = jnp.where(row >= 1, pltpu.roll(xi, 1, 0), 0.0)


def _s5_out_kernel(y1_ref, x_ref, a1_ref, cbd_ref, wg_ref, bg_ref, o_ref, xs_ref, nat_ref):
    r = y1_ref.shape[0]
    g = r // SUB
    n = S5_GROUPS * S5_P
    xr, xi = x_ref[:, :n], x_ref[:, n:]
    for s in range(SUB):
        ar, ai = a1_ref[s:s + 1, :n], a1_ref[s:s + 1, n:]
        xs_ref[s * g:(s + 1) * g, :n] = (ar * xr - ai * xi).astype(BF16)
        xs_ref[s * g:(s + 1) * g, n:] = (ar * xi + ai * xr).astype(BF16)
    y = y1_ref[...] + jnp.dot(xs_ref[...], cbd_ref[...], preferred_element_type=F32)
    zact = 0.5 * y * (1.0 + jnp.tanh(math.sqrt(2.0 / math.pi) * (y + 0.044715 * (y * y * y))))
    gate = jnp.dot(zact.astype(BF16), wg_ref[...], preferred_element_type=F32) + bg_ref[...]
    out = zact * _sigmoid(gate)
    for s in range(SUB):
        for half in range(2):
            nat_ref[half, pl.ds(s, g, stride=SUB), :] = out[s * g:(s + 1) * g,
                                                            half * V7X_LANES:(half + 1) * V7X_LANES]
    for half in range(2):
        o_ref[:, half * V7X_LANES:(half + 1) * V7X_LANES] = nat_ref[half]


def _s5_prep_kernel(ar_ref, ai_ref, bre_ref, bim_ref, cre_ref, cim_ref,
                    bd_ref, bbd_ref, cbd_ref, wend_ref, ain_ref, achunk_ref):
    n = S5_GROUPS * S5_P
    ar, ai = ar_ref[...], ai_ref[...]
    bre, bim, cre, cim = bre_ref[...], bim_ref[...], cre_ref[...], cim_ref[...]
    bbd_ref[:, :n] = bre.astype(BF16)
    bbd_ref[:, n:] = bim.astype(BF16)
    cbd_ref[:n, :] = cre.astype(BF16)
    cbd_ref[n:, :] = (-cim).astype(BF16)

    def split2(x):
        hi = x.astype(BF16)
        return hi, (x - hi.astype(F32)).astype(BF16)

    def dot3(x, w_hi, w_lo):
        x_hi, x_lo = split2(x)
        return (jnp.dot(x_hi, w_hi, preferred_element_type=F32) + jnp.dot(x_hi, w_lo, preferred_element_type=F32)
                + jnp.dot(x_lo, w_hi, preferred_element_type=F32))

    cre_parts, cim_parts = split2(cre), split2(cim)
    pr, pi = jnp.ones_like(ar), jnp.zeros_like(ar)
    for j in range(SUB):
        k_lag = dot3(bre * pr - bim * pi, *cre_parts) - dot3(bre * pi + bim * pr, *cim_parts)
        bd_ref[j] = k_lag.astype(BF16)
        wend_ref[SUB - 1 - j:SUB - j, :n] = pr
        wend_ref[SUB - 1 - j:SUB - j, n:] = pi
        pr, pi = pr * ar - pi * ai, pr * ai + pi * ar
        ain_ref[j:j + 1, :n] = pr
        ain_ref[j:j + 1, n:] = pi
    achunk_ref[:, :n] = pr
    achunk_ref[:, n:] = pi


def _s5_operators(lam_re, lam_im, log_step, b_re, b_im, c_re, c_im):
    depth = lam_re.shape[0]
    step = jnp.exp(log_step.astype(F32))[..., None]
    lre = jnp.minimum(lam_re.astype(F32), -1e-4)
    lim = lam_im.astype(F32)
    mag = jnp.exp(lre * step)
    a_re, a_im = mag * jnp.cos(lim * step), mag * jnp.sin(lim * step)
    den = lre * lre + lim * lim
    coef_re = ((a_re - 1.0) * lre + a_im * lim) / den
    coef_im = (a_im * lre - (a_re - 1.0) * lim) / den
    br, bi = b_re.astype(F32), b_im.astype(F32)
    bb_re = coef_re[..., None] * br - coef_im[..., None] * bi
    bb_im = coef_re[..., None] * bi + coef_im[..., None] * br
    cr, ci = c_re.astype(F32), c_im.astype(F32)
    n = S5_GROUPS * S5_P
    same_group = (np.arange(GROUP_W)[:, None] // S5_CH) == (np.arange(n)[None, :] // S5_P)
    in_mask = jnp.asarray(same_group, F32)
    out_mask = jnp.asarray(same_group.T, F32)

    def in_side(b):
        return jnp.tile(b.transpose(0, 1, 3, 2).reshape(depth, GROUP_W, S5_P), (1, 1, S5_GROUPS)) * in_mask

    def out_side(c):
        return jnp.tile(c.transpose(0, 1, 3, 2).reshape(depth, n, S5_CH), (1, 1, S5_GROUPS)) * out_mask

    args = (a_re.reshape(depth, 1, n), a_im.reshape(depth, 1, n), in_side(bb_re), in_side(bb_im),
            out_side(cr), out_side(ci))
    out_dims = (((SUB, GROUP_W, GROUP_W), BF16),
                ((GROUP_W, 2 * n), BF16),
                ((2 * n, GROUP_W), BF16),
                ((SUB, 2 * n), F32),
                ((SUB, 2 * n), F32),
                ((1, 2 * n), F32))

    def per_layer(shape):
        return pl.BlockSpec((None,) + shape, lambda l: (l,) + (0,) * len(shape))

    return pl.pallas_call(
        _s5_prep_kernel,
        out_shape=tuple(jax.ShapeDtypeStruct((depth,) + s, d) for s, d in out_dims),
        grid=(depth,),
        in_specs=[per_layer(a.shape[1:]) for a in args],
        out_specs=tuple(per_layer(s) for s, _ in out_dims),
        name="s5_prep",
        compiler_params=pltpu.CompilerParams(
            dimension_semantics=("arbitrary",),
            vmem_limit_bytes=_vmem_limit(8 * _nbytes((GROUP_W, n), F32), 10 * _nbytes((GROUP_W, n), F32),
                                         4 * _nbytes((SUB, GROUP_W, GROUP_W), BF16))),
    )(*args)


def _s5(u, layer, ops, d_skip, w_glu, b_glu, batch, seq, r):
    t = u.shape[0]
    bd, bbd, cbd, w_end, a_in, a_chunk = ops
    n2 = 2 * S5_GROUPS * S5_P
    nck = seq // SUB

    def whole(a):
        return _layer_block(a, layer)

    rows = pl.BlockSpec((r, GROUP_W), lambda i: (i, 0))
    crow = pl.BlockSpec((r // SUB, n2), lambda i: (i, 0))
    y1, e = pl.pallas_call(
        _s5_local_kernel,
        out_shape=(jax.ShapeDtypeStruct((t, GROUP_W), F32), jax.ShapeDtypeStruct((t // SUB, n2), F32)),
        grid=(t // r,),
        in_specs=[pl.BlockSpec((r, V7X_LANES), lambda i: (i, 0)), pl.BlockSpec((r, V7X_LANES), lambda i: (i, 1)),
                  whole(bd), whole(bbd), whole(w_end), whole(d_skip)],
        out_specs=(rows, crow),
        scratch_shapes=[pltpu.VMEM((r, GROUP_W), F32)],
        name="s5_local",
        compiler_params=pltpu.CompilerParams(
            dimension_semantics=("arbitrary",),
            vmem_limit_bytes=_vmem_limit(_nbytes(bd.shape[1:], BF16), _nbytes(bbd.shape[1:], BF16),
                                         6 * _nbytes((r, n2), F32), 8 * _nbytes((r, GROUP_W), F32))),
    )(u, u, bd, bbd, w_end, d_skip)
    xin = pl.pallas_call(
        _s5_scan_kernel,
        out_shape=jax.ShapeDtypeStruct((t // SUB, n2), F32),
        grid=(batch,),
        in_specs=[pl.BlockSpec((nck, n2), lambda b: (b, 0)), whole(a_chunk)],
        out_specs=pl.BlockSpec((nck, n2), lambda b: (b, 0)),
        name="s5_scan",
        compiler_params=pltpu.CompilerParams(
            dimension_semantics=("arbitrary",), vmem_limit_bytes=_vmem_limit(10 * _nbytes((nck, n2), F32))),
    )(e, a_chunk)
    return pl.pallas_call(
        _s5_out_kernel,
        out_shape=jax.ShapeDtypeStruct((t, GROUP_W), F32),
        grid=(t // r,),
        in_specs=[rows, crow, whole(a_in), whole(cbd), whole(w_glu), whole(b_glu)],
        out_specs=rows,
        scratch_shapes=[pltpu.VMEM((r, n2), BF16), pltpu.VMEM((2, r, V7X_LANES), F32)],
        name="s5_out",
        compiler_params=pltpu.CompilerParams(
            dimension_semantics=("arbitrary",),
            vmem_limit_bytes=_vmem_limit(_nbytes(cbd.shape[1:], BF16), 3 * _nbytes((r, n2), BF16),
                                         10 * _nbytes((r, GROUP_W), F32))),
    )(y1, xin, a_in, cbd, w_glu, b_glu)


HALO = 2 * V7X_SUBLANES


def _post_kernel(tiles_per_seq, final, ya_ref, yb_ref, yc_ref, gate_ref, yd_ref, h_ref, p_ref,
                 gn_ref, wo_ref, fg_ref, wu_ref, cw_ref, cb_ref, wd_ref, pg_ref, wpg_ref, wp_ref, ng_ref,
                 o_ref, xn_ref, up_ref, act_ref):
    tm = h_ref.shape[0]
    gw = GROUP_W

    @pl.when(pl.program_id(0) % tiles_per_seq == 0)
    def _():
        xn_ref[0:HALO, :] = jnp.zeros((HALO, D_MODEL), BF16)

    parts = (_rms(ya_ref[...], gn_ref[:, 0:gw]),
             _rms(yb_ref[...], gn_ref[:, gw:2 * gw]),
             _rms(yc_ref[...], gn_ref[:, 2 * gw:3 * gw]) * _sigmoid(gate_ref[...]),
             _rms(yd_ref[...], gn_ref[:, 3 * gw:4 * gw]))
    h1 = h_ref[...]
    for g, part in enumerate(parts):
        h1 = h1 + jnp.dot(part.astype(BF16), wo_ref[g * gw:(g + 1) * gw, :], preferred_element_type=F32)
    o_ref[...] = h1
    xn_ref[HALO:, :] = _rms(h1, fg_ref[...]).astype(BF16)

    nchunk = D_FF // MXU_TILE

    def conv(slot, c):
        cols = slice(c * MXU_TILE, (c + 1) * MXU_TILE)
        up_ref[slot] = jnp.dot(xn_ref[...], wu_ref[:, cols], preferred_element_type=F32)
        return (cb_ref[:, cols] + cw_ref[0:1, cols] * up_ref[slot, pl.ds(HALO - 2, tm), :]
                + cw_ref[1:2, cols] * up_ref[slot, pl.ds(HALO - 1, tm), :]
                + cw_ref[2:3, cols] * up_ref[slot, pl.ds(HALO, tm), :])

    for c in range(nchunk):
        slot = 2 * (c % 2)
        gate = conv(slot, c)
        val = conv(slot + 1, c + nchunk)
        act_ref[:, c * MXU_TILE:(c + 1) * MXU_TILE] = (gate * _sigmoid(gate) * val).astype(BF16)
    xn_ref[0:HALO, :] = xn_ref[tm:tm + HALO, :]
    h2 = o_ref[...] + jnp.dot(act_ref[...], wd_ref[...], preferred_element_type=F32)

    pgate = _sigmoid(jnp.dot(_rms(h2, pg_ref[...]).astype(BF16), wpg_ref[...], preferred_element_type=F32))
    out = h2 + pgate * jnp.dot(p_ref[...].astype(BF16), wp_ref[...], preferred_element_type=F32)
    o_ref[...] = _rms(out, ng_ref[...]) if final else out


def _post(ya, yb, yc, hg, yd, h, p, seq, layer, stacked, final, tm):
    t = h.shape[0]
    rows = pl.BlockSpec((tm, GROUP_W), lambda i: (i, 0))
    wide = pl.BlockSpec((tm, D_MODEL), lambda i: (i, 0))
    consts = tuple(stacked)
    vmem = _vmem_limit(sum(_nbytes(c.shape[1:], c.dtype) for c in consts), 12 * _nbytes((tm, GROUP_W), F32),
                       4 * _nbytes((tm, D_MODEL), F32), _nbytes((tm + HALO, D_MODEL), BF16),
                       4 * _nbytes((tm + HALO, MXU_TILE), F32), _nbytes((tm, D_FF), BF16),
                       3 * _nbytes((tm, D_MODEL), F32))
    return pl.pallas_call(
        functools.partial(_post_kernel, seq // tm, final),
        out_shape=jax.ShapeDtypeStruct((t, D_MODEL), F32),
        grid=(t // tm,),
        in_specs=[rows, rows, rows, pl.BlockSpec((tm, GROUP_W), lambda i: (i, 3)), rows, wide,
                  pl.BlockSpec((tm, PLE_DIM), lambda i: (layer * (t // tm) + i, 0))]
        + [_layer_block(c, layer) for c in consts],
        out_specs=wide,
        scratch_shapes=[pltpu.VMEM((tm + HALO, D_MODEL), BF16), pltpu.VMEM((4, tm + HALO, MXU_TILE), F32),
                        pltpu.VMEM((tm, D_FF), BF16)],
        name="post_mixer",
        compiler_params=pltpu.CompilerParams(dimension_semantics=("arbitrary",), vmem_limit_bytes=vmem),
    )(ya, yb, yc, hg, yd, h, p, *consts)


def _tiles(seq):
    return dict(tin=min(512, seq), tm=min(256, seq), tq=min(512, seq), tkv=min(512, seq), r=min(512, seq),
                rs5=min(1024, seq))


def kernel(x, p, positions, attn_norm_g, w_in, mla_q_norm_g, mla_w_uq, mla_kv_norm_g, mla_w_ukv, fox_b_f,
           hgrn_lb_param, s5_lam_re, s5_lam_im, s5_log_step, s5_b_re, s5_b_im, s5_c_re, s5_c_im, s5_d,
           s5_w_glu, s5_b_glu, group_norm_g, w_out, ffn_norm_g, w_up, conv_w, conv_b, w_down, ple_norm_g,
           w_ple_gate, w_ple, final_norm_g):
    batch, seq, _ = x.shape
    depth = w_in.shape[0]
    t = batch * seq
    ts = _tiles(seq)
    assert all(seq % ts[k] == 0 for k in ts) and ts["tq"] % ts["tkv"] == 0
    assert (seq // SUB) & (seq // SUB - 1) == 0, "chunk scan assumes a power-of-two chunk count"

    lb_all = jnp.cumsum(jax.nn.softmax(hgrn_lb_param.astype(F32), axis=0), axis=0)
    lb_all = lb_all - lb_all[0:1]
    tables = _rope_tables(positions, ts["tin"])

    def row(v):
        return v.reshape(v.shape[0], 1, -1)

    in_params = _inproj_params(attn_norm_g, w_in, mla_q_norm_g, mla_w_uq, mla_kv_norm_g, mla_w_ukv, fox_b_f)
    s5_ops = _s5_operators(s5_lam_re, s5_lam_im, s5_log_step, s5_b_re, s5_b_im, s5_c_re, s5_c_im)
    s5_tail = (row(s5_d.astype(F32)), s5_w_glu.astype(BF16), row(s5_b_glu.astype(F32)))
    post_params = (row(group_norm_g), w_out.astype(BF16), row(ffn_norm_g), w_up.astype(BF16), conv_w,
                   row(conv_b), w_down.astype(BF16), row(ple_norm_g), w_ple_gate.astype(BF16),
                   w_ple.astype(BF16), jnp.broadcast_to(final_norm_g.reshape(1, 1, -1), (depth, 1, D_MODEL)))
    h = x.reshape(t, D_MODEL)
    for i in range(depth):
        mq, mk, mv, fq, fk, fv, hg, su = _inproj(h, seq, i, in_params, tables, ts["tin"])
        y_a = _attention(mq, mk, mv, batch, seq, ts["tq"], ts["tkv"])
        y_b = _attention(fq, fk, fv, batch, seq, ts["tq"], ts["tkv"])
        y_c = _hgrn(hg, i, row(lb_all), batch, seq, ts["r"])
        y_d = _s5(su, i, s5_ops, *s5_tail, batch, seq, ts["rs5"])
        h = _post(y_a, y_b, y_c, hg, y_d, h, p.reshape(depth * t, PLE_DIM), seq, i, post_params,
                  i == depth - 1, ts["tm"])
    return h.reshape(batch, seq, D_MODEL)
```
